```python
import math
import jax, jax.numpy as jnp
from jax import lax
import numpy as np

D_MODEL = 1024
BATCH = 8
SEQ = 2048
DEPTH = 2
DEC_BATCH = 32
DEC_SEQ = 4
PAST_LEN = 8192
PAGE_SIZE = 128

HEAD_DIM = 128
H_A = D_MODEL // (2 * HEAD_DIM)
DK_A = HEAD_DIM
DV_A = HEAD_DIM
H_B = D_MODEL // (2 * HEAD_DIM)
DK_B = HEAD_DIM
DV_B = HEAD_DIM
H_C = D_MODEL // (2 * HEAD_DIM)
DH_C = HEAD_DIM
D_FF = 4 * D_MODEL
N_EVEN = (DEPTH + 1) // 2
N_ODD = DEPTH // 2
CHUNK_A = 16
CHUNK_B = 128
Q_BLOCK = 128
ROPE_BASE = 10000.0
NORM_EPS = 1e-6
AB_WIDTHS = (H_A * DK_A, H_A * DK_A, H_A * DV_A, H_A * DV_A,
             H_B * DK_B, H_B * DK_B, H_B * DV_B, H_B * DV_B)
D_IN_AB = sum(AB_WIDTHS)
D_MIX_AB = H_A * DV_A + H_B * DV_B
C_WIDTHS = (H_C * 2 * DH_C, H_C * 2 * DH_C, H_C * 2 * DH_C)
D_IN_C = sum(C_WIDTHS)
D_MIX_C = H_C * 2 * DH_C

kernel_name = 'hgrn2_retention_diffattn_adaln_decoder_step'


def rmsnorm(x, w=None):
    xf = x.astype(jnp.float32)
    y = xf * lax.rsqrt(jnp.mean(xf * xf, axis=-1, keepdims=True) + NORM_EPS)
    if w is not None:
        y = y * w.astype(jnp.float32)
    return y.astype(x.dtype)


def split_cols(x, widths):
    out, off = [], 0
    for w in widths:
        out.append(x[..., off:off + w])
        off += w
    return out


def rotary(x, pos):
    d = x.shape[-1]
    inv = 1.0 / (ROPE_BASE ** jnp.linspace(0.0, 1.0, d // 2, dtype=jnp.float32))
    ang = pos.astype(jnp.float32)[:, None] * inv[None, :]
    cos = jnp.cos(ang)[None, :, None, :]
    sin = jnp.sin(ang)[None, :, None, :]
    xp = x.astype(jnp.float32).reshape(*x.shape[:-1], d // 2, 2)
    x0, x1 = xp[..., 0], xp[..., 1]
    out = jnp.stack([x0 * cos - x1 * sin, x1 * cos + x0 * sin], axis=-1)
    return out.reshape(x.shape).astype(x.dtype)


def chunked_gated_linear_attn(q, k, v, log_f, state0, chunk):
    f32 = jnp.float32
    B, T, H, K = q.shape
    n = T // chunk
    scalar = log_f.shape[-1] == 1

    def blocks(a):
        return jnp.moveaxis(a.astype(f32).reshape(B, n, chunk, *a.shape[2:]), 1, 0)

    causal = jnp.tril(jnp.ones((chunk, chunk), dtype=bool))[None, :, :, None, None]

    def step(S, inp):
        qc, kc, vc, lc = inp
        G = jnp.cumsum(lc, axis=1)
        rel = jnp.exp(jnp.where(causal, G[:, :, None] - G[:, None, :], -jnp.inf))
        if scalar:
            A = jnp.einsum('bthk,bshk->btsh', qc, kc) * rel[..., 0]
        else:
            A = jnp.einsum('bthk,bshk,btshk->btsh', qc, kc, rel)
        o = jnp.einsum('btsh,bshv->bthv', A, vc) + jnp.einsum('bthk,bhkv->bthv', qc * jnp.exp(G), S)
        G_end = G[:, -1]
        k_dec = kc * jnp.exp(G_end[:, None] - G)
        S = S * jnp.exp(G_end)[..., None] + jnp.einsum('bshk,bshv->bhkv', k_dec, vc)
        return S, o

    S, o = lax.scan(step, state0.astype(f32), (blocks(q), blocks(k), blocks(v), blocks(log_f)))
    o = jnp.moveaxis(o, 0, 1).reshape(B, T, H, v.shape[-1])
    return o.astype(v.dtype), S.astype(state0.dtype)


def hgrn2_retention_mixer(h, pos0, s_a0, s_b0, w_in, w_out, lb, norm_a):
    B, T, _ = h.shape
    f32 = jnp.float32
    qa, fa, ia, ga, qb, kb, vb, gb = split_cols(h @ w_in, AB_WIDTHS)
    fz = lb + (1.0 - lb) * jax.nn.sigmoid(fa.astype(f32))
    log_fa = jnp.log(fz).reshape(B, T, H_A, DK_A)
    ka = (1.0 - fz).reshape(B, T, H_A, DK_A)
    qa = jax.nn.silu(qa).reshape(B, T, H_A, DK_A) * DK_A ** -0.5
    oa, s_a = chunked_gated_linear_attn(qa, ka, ia.reshape(B, T, H_A, DV_A), log_fa, s_a0,
                                        math.gcd(T, CHUNK_A))
    oa = rmsnorm(oa, norm_a) * jax.nn.silu(ga.reshape(B, T, H_A, DV_A))
    pos = pos0 + jnp.arange(T)
    qb = rotary(qb.reshape(B, T, H_B, DK_B), pos)
    kb = rotary(kb.reshape(B, T, H_B, DK_B), pos) * DK_B ** -0.5
    log_gamma = jnp.log(1.0 - 2.0 ** (-5.0 - jnp.arange(H_B, dtype=f32)))
    log_g = jnp.broadcast_to(log_gamma[:, None], (B, T, H_B, 1))
    ob, s_b = chunked_gated_linear_attn(qb, kb, vb.reshape(B, T, H_B, DV_B), log_g, s_b0,
                                        math.gcd(T, CHUNK_B))
    ob = rmsnorm(ob) * jax.nn.silu(gb.reshape(B, T, H_B, DV_B))
    o = jnp.concatenate([oa.reshape(B, T, -1), ob.reshape(B, T, -1)], axis=-1).astype(h.dtype)
    return o @ w_out, s_a, s_b


def diff_attn_causal(q, k, v, lam):
    B, T, H, _, dh = q.shape
    qb = math.gcd(T, Q_BLOCK)
    nb = T // qb
    scale = DH_C ** -0.5
    q_blocks = jnp.moveaxis(q.reshape(B, nb, qb, H, 2, dh), 1, 0)
    kpos = jnp.arange(T)

    def block(args):
        qi, i = args
        s = jnp.einsum('bqhjd,bkhjd->bhjqk', qi, k).astype(jnp.float32) * scale
        qpos = i * qb + jnp.arange(qb)
        s = jnp.where(kpos[None, :] <= qpos[:, None], s, -jnp.inf)
        p = jax.nn.softmax(s, axis=-1).astype(v.dtype)
        o = jnp.einsum('bhjqk,bkhe->bqhje', p, v)
        return o[:, :, :, 0] - lam * o[:, :, :, 1]

    o = lax.map(block, (q_blocks, jnp.arange(nb)))
    return jnp.moveaxis(o, 0, 1).reshape(B, T, H, v.shape[-1])


def diff_attn_with_past(q, k, v, k_past, v_past, lam):
    T = q.shape[1]
    P = k_past.shape[1]
    scale = DH_C ** -0.5
    s_past = jnp.einsum('bqhjd,bkhjd->bhjqk', q, k_past).astype(jnp.float32) * scale
    s_new = jnp.einsum('bqhjd,bkhjd->bhjqk', q, k).astype(jnp.float32) * scale
    s_new = jnp.where(jnp.tril(jnp.ones((T, T), dtype=bool)), s_new, -jnp.inf)
    p = jax.nn.softmax(jnp.concatenate([s_past, s_new], axis=-1), axis=-1).astype(v.dtype)
    o = (jnp.einsum('bhjqk,bkhe->bqhje', p[..., :P], v_past)
         + jnp.einsum('bhjqk,bkhe->bqhje', p[..., P:], v))
    return o[:, :, :, 0] - lam * o[:, :, :, 1]


def diff_attention(h, w_in, w_out, lam_p, subln_w, lam_init, k_past, v_past):
    B, T, _ = h.shape
    q, k, v = split_cols(h @ w_in, C_WIDTHS)
    q = q.reshape(B, T, H_C, 2, DH_C)
    k = k.reshape(B, T, H_C, 2, DH_C)
    v = v.reshape(B, T, H_C, 2 * DH_C)
    lp = lam_p.astype(jnp.float32)
    lam = jnp.exp(jnp.sum(lp[0] * lp[1])) - jnp.exp(jnp.sum(lp[2] * lp[3])) + lam_init
    if k_past is None:
        o = diff_attn_causal(q, k, v, lam)
    else:
        o = diff_attn_with_past(q, k, v, k_past, v_past, lam)
    o = rmsnorm(o, subln_w) * (1.0 - lam_init)
    return o.reshape(B, T, D_MIX_C).astype(h.dtype) @ w_out, k, v


def squared_relu_mlp(h, w_up, w_down):
    u = jax.nn.relu(h @ w_up)
    return (u * u) @ w_down


def trunk(x, c, pos0, init_hgrn, init_ret, cache_k, cache_v, page_table,
          w_ada, b_ada, norm_w, w_in_ab, w_out_ab, hgrn_lb_logits, hgrn_norm_w,
          w_in_c, w_out_c, diff_lambda, diff_subln_w, w_mlp_up, w_mlp_down, final_norm_w):
    B, T, _ = x.shape
    lb_all = jnp.cumsum(jax.nn.softmax(hgrn_lb_logits.astype(jnp.float32), axis=0), axis=0)
    sc = jax.nn.silu(c)
    hg_out, rt_out, k_out, v_out = [], [], [], []
    for l in range(DEPTH):
        mod = sc @ w_ada[l] + b_ada[l]
        sh1, sc1, g1, sh2, sc2, g2 = [m[:, None, :] for m in jnp.split(mod, 6, axis=-1)]
        h = rmsnorm(x, norm_w[l, 0]) * (1.0 + sc1) + sh1
        if l % 2 == 0:
            e = l // 2
            y, s_a, s_b = hgrn2_retention_mixer(h, pos0, init_hgrn[e], init_ret[e], w_in_ab[e],
                                                w_out_ab[e], lb_all[e], hgrn_norm_w[e])
            hg_out.append(s_a)
            rt_out.append(s_b)
        else:
            o = l // 2
            lam_init = 0.8 - 0.6 * math.exp(-0.3 * l)
            if cache_k is None:
                k_past, v_past = None, None
            else:
                k_past = cache_k[o, page_table].reshape(B, -1, H_C, 2, DH_C)
                v_past = cache_v[o, page_table].reshape(B, -1, H_C, 2 * DH_C)
            y, k_new, v_new = diff_attention(h, w_in_c[o], w_out_c[o], diff_lambda[o],
                                             diff_subln_w[o], lam_init, k_past, v_past)
            k_out.append(k_new)
            v_out.append(v_new)
        x = x + g1 * y
        h = rmsnorm(x, norm_w[l, 1]) * (1.0 + sc2) + sh2
        x = x + g2 * squared_relu_mlp(h, w_mlp_up[l], w_mlp_down[l])
    return (rmsnorm(x, final_norm_w), jnp.stack(hg_out), jnp.stack(rt_out),
            jnp.stack(k_out), jnp.stack(v_out))


def setup_inputs(seed: int = 0) -> dict:
    key = jax.random.key(seed)
    ks = jax.random.split(key, 24)
    f32 = jnp.float32

    def nrm(k, shape, s):
        return jax.random.normal(k, shape, f32) * s

    n_pages = PAST_LEN // PAGE_SIZE
    n_used = DEC_BATCH * n_pages
    n_phys = n_used + max(1, n_used // 4)
    page_table = jax.random.permutation(ks[6], n_phys)[:n_used].reshape(DEC_BATCH, n_pages).astype(jnp.int32)
    return {
        'x_prompt': nrm(ks[0], (BATCH, SEQ, D_MODEL), 1.0),
        'x_sample': nrm(ks[1], (DEC_BATCH, DEC_SEQ, D_MODEL), 1.0),
        'state_hgrn': nrm(ks[2], (N_EVEN, DEC_BATCH, H_A, DK_A, DV_A), 0.5),
        'state_ret': nrm(ks[3], (N_EVEN, DEC_BATCH, H_B, DK_B, DV_B), 1.0),
        'cache_k': nrm(ks[4], (N_ODD, n_phys, PAGE_SIZE, H_C, 2, DH_C), 1.0),
        'cache_v': nrm(ks[5], (N_ODD, n_phys, PAGE_SIZE, H_C, 2 * DH_C), 1.0),
        'page_table': page_table,
        'c_prompt': nrm(ks[7], (BATCH, D_MODEL), 1.0),
        'c_sample': nrm(ks[8], (DEC_BATCH, D_MODEL), 1.0),
        'w_ada': nrm(ks[9], (DEPTH, D_MODEL, 6 * D_MODEL), 0.5 * D_MODEL ** -0.5),
        'b_ada': nrm(ks[10], (DEPTH, 6 * D_MODEL), 0.02),
        'norm_w': 1.0 + nrm(ks[11], (DEPTH, 2, D_MODEL), 0.02),
        'w_in_ab': nrm(ks[12], (N_EVEN, D_MODEL, D_IN_AB), D_MODEL ** -0.5),
        'w_out_ab': nrm(ks[13], (N_EVEN, D_MIX_AB, D_MODEL), D_MIX_AB ** -0.5),
        'hgrn_lb_logits': nrm(ks[14], (N_EVEN + 1, H_A * DK_A), 0.1),
        'hgrn_norm_w': 1.0 + nrm(ks[15], (N_EVEN, DV_A), 0.02),
        'w_in_c': nrm(ks[16], (N_ODD, D_MODEL, D_IN_C), D_MODEL ** -0.5),
        'w_out_c': nrm(ks[17], (N_ODD, D_MIX_C, D_MODEL), D_MIX_C ** -0.5),
        'diff_lambda': nrm(ks[18], (N_ODD, 4, DH_C), 0.1),
        'diff_subln_w': 1.0 + nrm(ks[19], (N_ODD, 2 * DH_C), 0.02),
        'w_mlp_up': nrm(ks[20], (DEPTH, D_MODEL, D_FF), D_MODEL ** -0.5),
        'w_mlp_down': nrm(ks[21], (DEPTH, D_FF, D_MODEL), D_FF ** -0.5),
        'final_norm_w': 1.0 + nrm(ks[22], (D_MODEL,), 0.02),
    }


def reference(x_prompt, x_sample, state_hgrn, state_ret, cache_k, cache_v, page_table,
              c_prompt, c_sample, w_ada, b_ada, norm_w, w_in_ab, w_out_ab, hgrn_lb_logits,
              hgrn_norm_w, w_in_c, w_out_c, diff_lambda, diff_subln_w, w_mlp_up, w_mlp_down,
              final_norm_w):
    B, T, _ = x_prompt.shape
    zeros_a = jnp.zeros((N_EVEN, B, H_A, DK_A, DV_A), x_prompt.dtype)
    zeros_b = jnp.zeros((N_EVEN, B, H_B, DK_B, DV_B), x_prompt.dtype)
    y_prompt, hg_p, rt_p, k_p, v_p = trunk(
        x_prompt, c_prompt, 0, zeros_a, zeros_b, None, None, None,
        w_ada, b_ada, norm_w, w_in_ab, w_out_ab, hgrn_lb_logits, hgrn_norm_w,
        w_in_c, w_out_c, diff_lambda, diff_subln_w, w_mlp_up, w_mlp_down, final_norm_w)
    past_len = page_table.shape[1] * cache_k.shape[2]
    y_sample, hg_s, rt_s, k_s, v_s = trunk(
        x_sample, c_sample, past_len, state_hgrn, state_ret, cache_k, cache_v, page_table,
        w_ada, b_ada, norm_w, w_in_ab, w_out_ab, hgrn_lb_logits, hgrn_norm_w,
        w_in_c, w_out_c, diff_lambda, diff_subln_w, w_mlp_up, w_mlp_down, final_norm_w)
    k_p = k_p.reshape(N_ODD, B, T // PAGE_SIZE, PAGE_SIZE, H_C, 2, DH_C)
    v_p = v_p.reshape(N_ODD, B, T // PAGE_SIZE, PAGE_SIZE, H_C, 2 * DH_C)
    return (y_prompt, y_sample, hg_p, rt_p, k_p, v_p, hg_s, rt_s, k_s, v_s)
```

```python
import functools
import math

import jax
import jax.numpy as jnp
import numpy as np
from jax import lax
from jax.experimental import pallas as pl
from jax.experimental.pallas import tpu as pltpu

F32 = jnp.float32
BF16 = jnp.bfloat16

HEAD_DIM = 128
NORM_EPS = 1e-6
ROPE_BASE = 10000.0
V7X_VMEM_LIMIT = 52 * 1024 * 1024

_NT = (((1,), (1,)), ((), ()))
_TN = (((0,), (0,)), ((), ()))


def _params(*semantics):
    return pltpu.CompilerParams(dimension_semantics=semantics, vmem_limit_bytes=V7X_VMEM_LIMIT)


def _resident(shape):
    zeros = (0,) * len(shape)
    return pl.BlockSpec(shape, lambda *_: zeros, pipeline_mode=pl.Buffered(1))


def _sigmoid(x):
    return 1.0 / (1.0 + jnp.exp(-x))


def _silu(x):
    return x * _sigmoid(x)


def _rms(x):
    return x * lax.rsqrt(jnp.mean(x * x, axis=-1, keepdims=True) + NORM_EPS)


def _mm(a, b):
    return jnp.dot(a, b, preferred_element_type=F32)


def _ada_kernel(c_ref, w_ref, b_ref, o_ref):
    s = _silu(c_ref[...]).astype(BF16)
    o_ref[0] = _mm(s, w_ref[0].astype(BF16)) + b_ref[0]


def _ada(c, w_ada, b_ada):
    depth, d, n = w_ada.shape
    r = c.shape[0]
    tn = n // 4
    return pl.pallas_call(
        _ada_kernel,
        grid=(depth, n // tn),
        in_specs=[pl.BlockSpec((r, d), lambda l, j: (0, 0)),
                  pl.BlockSpec((1, d, tn), lambda l, j: (l, 0, j)),
                  pl.BlockSpec((1, 1, tn), lambda l, j: (l, 0, j))],
        out_specs=pl.BlockSpec((1, r, tn), lambda l, j: (l, 0, j)),
        out_shape=jax.ShapeDtypeStruct((depth, r, n), F32),
        compiler_params=_params("arbitrary", "arbitrary"),
        name="ada_mod",
    )(c, w_ada, b_ada.reshape(depth, 1, n))


def _modulated_norm(x, nw_ref, mod_ref, shift_col, scale_col):
    d = x.shape[-1]
    shift = mod_ref[0, :, shift_col * d:(shift_col + 1) * d]
    scale = mod_ref[0, :, scale_col * d:(scale_col + 1) * d]
    return _rms(x) * nw_ref[...] * (1.0 + scale) + shift


def _inproj_kernel(x_ref, mod_ref, nw_ref, w_ref, *out_refs, shift_col, scale_col, out_scales, tc):
    h = _modulated_norm(x_ref[...], nw_ref, mod_ref, shift_col, scale_col).astype(BF16)
    off = 0
    for o_ref, out_scale in zip(out_refs, out_scales):
        width = o_ref.shape[-1]
        for c in range(0, width, tc):
            y = _mm(h, w_ref[:, off + c:off + c + tc])
            if out_scale != 1.0:
                y = y * out_scale
            o_ref[:, c:c + tc] = y.astype(o_ref.dtype)
        off += width


def _inproj(x, mod, nw, w, outs, *, tm, shift_col, scale_col):
    m, d = x.shape
    n = w.shape[1]
    rows_per_mod = m // mod.shape[0]
    tiles_per_mod = rows_per_mod // tm
    kern = functools.partial(_inproj_kernel, shift_col=shift_col, scale_col=scale_col,
                             out_scales=tuple(o[2] for o in outs), tc=512)
    return pl.pallas_call(
        kern,
        grid=(m // tm,),
        in_specs=[pl.BlockSpec((tm, d), lambda i: (i, 0)),
                  pl.BlockSpec((1, mod.shape[1], mod.shape[2]), lambda i: (i // tiles_per_mod, 0, 0)),
                  _resident((1, d)),
                  _resident((d, n))],
        out_specs=[pl.BlockSpec((tm, o[0]), lambda i: (i, 0)) for o in outs],
        out_shape=[jax.ShapeDtypeStruct((m, o[0]), o[1]) for o in outs],
        compiler_params=_params("parallel"),
        name="norm_mod_inproj",
    )(x, mod, nw.reshape(1, d), w)


def _outproj_kernel(o_ref, x_ref, mod_ref, w_ref, out_ref, *, gate_col):
    d = x_ref.shape[-1]
    gate = mod_ref[0, :, gate_col * d:(gate_col + 1) * d]
    out_ref[...] = x_ref[...] + gate * _mm(o_ref[...].astype(BF16), w_ref[...])


def _outproj(o, x, mod, w, *, tm, gate_col):
    m, d = x.shape
    k = o.shape[1]
    tiles_per_mod = (m // mod.shape[0]) // tm
    return pl.pallas_call(
        functools.partial(_outproj_kernel, gate_col=gate_col),
        grid=(m // tm,),
        in_specs=[pl.BlockSpec((tm, k), lambda i: (i, 0)),
                  pl.BlockSpec((tm, d), lambda i: (i, 0)),
                  pl.BlockSpec((1, mod.shape[1], mod.shape[2]), lambda i: (i // tiles_per_mod, 0, 0)),
                  _resident((k, d))],
        out_specs=pl.BlockSpec((tm, d), lambda i: (i, 0)),
        out_shape=jax.ShapeDtypeStruct((m, d), F32),
        compiler_params=_params("parallel"),
        name="outproj_residual",
    )(o, x, mod, w)


def _mlp_kernel(x_ref, mod_ref, nw_ref, wu_ref, wd_ref, fw_ref, out_ref, u_ref, *,
                shift_col, scale_col, gate_col, tf, final_norm):
    x = x_ref[...]
    d = x.shape[-1]
    h = _modulated_norm(x, nw_ref, mod_ref, shift_col, scale_col).astype(BF16)
    for c in range(0, wu_ref.shape[1], tf):
        u = jnp.maximum(_mm(h, wu_ref[:, c:c + tf]), 0.0)
        u_ref[:, c:c + tf] = (u * u).astype(BF16)
    gate = mod_ref[0, :, gate_col * d:(gate_col + 1) * d]
    y = x + gate * _mm(u_ref[...], wd_ref[...])
    if final_norm:
        y = _rms(y) * fw_ref[...]
    out_ref[...] = y


def _mlp(x, mod, nw, w_up, w_down, final_w, *, tm, final_norm):
    m, d = x.shape
    dff = w_up.shape[1]
    tiles_per_mod = (m // mod.shape[0]) // tm
    kern = functools.partial(_mlp_kernel, shift_col=3, scale_col=4, gate_col=5, tf=512,
                             final_norm=final_norm)
    return pl.pallas_call(
        kern,
        grid=(m // tm,),
        in_specs=[pl.BlockSpec((tm, d), lambda i: (i, 0)),
                  pl.BlockSpec((1, mod.shape[1], mod.shape[2]), lambda i: (i // tiles_per_mod, 0, 0)),
                  _resident((1, d)),
                  _resident((d, dff)),
                  _resident((dff, d)),
                  _resident((1, d))],
        out_specs=pl.BlockSpec((tm, d), lambda i: (i, 0)),
        out_shape=jax.ShapeDtypeStruct((m, d), F32),
        scratch_shapes=[pltpu.VMEM((tm, dff), BF16)],
        compiler_params=_params("parallel"),
        name="mlp_residual",
    )(x, mod, nw.reshape(1, d), w_up, w_down, final_w.reshape(1, d))


def _hgrn_lower_bound(logits_ref, layer_slot):
    lg = logits_ref[...]
    ex = jnp.exp(lg - jnp.max(lg, axis=0, keepdims=True))
    pr = ex / jnp.sum(ex, axis=0, keepdims=True)
    return jnp.sum(pr[:layer_slot + 1], axis=0, keepdims=True)


def _rotate_pairs(x, cos, sin_signed):
    n = x.shape[-1]
    lane = lax.broadcasted_iota(jnp.int32, x.shape, 1)
    swapped = jnp.where((lane & 1) == 0, pltpu.roll(x, n - 1, 1), pltpu.roll(x, 1, 1))
    return x * cos + swapped * sin_signed


def _cumsum_rows(x):
    n = x.shape[0]
    row = lax.broadcasted_iota(jnp.int32, x.shape, 0)
    d = 1
    while d < n:
        x = x + jnp.where(row >= d, pltpu.roll(x, d, 0), 0.0)
        d *= 2
    return x


def _split_row(g, b):
    n, lanes = g.shape
    if 2 * b >= 8:
        g3 = g.reshape(n // (2 * b), 2 * b, lanes)
        return jnp.broadcast_to(g3[:, b - 1:b, :], g3.shape).reshape(n, lanes)
    rmod = lax.broadcasted_iota(jnp.int32, g.shape, 0) & (2 * b - 1)
    out = g
    for r in range(2 * b):
        if r != b - 1:
            out = jnp.where(rmod == r, pltpu.roll(g, (r - (b - 1)) % n, 0), out)
    return out


def _level_matrix(n):
    t = np.arange(n)[:, None]
    s = np.arange(n)[None, :]
    hi = np.floor(np.log2(np.maximum(t ^ s, 1))).astype(np.int32)
    return jnp.asarray(np.where(s < t, hi, -1).astype(np.int32))


def _hgrn_chunk(q, k, v, g, lv, st):
    n = q.shape[0]
    row = lax.broadcasted_iota(jnp.int32, q.shape, 0)
    ri = lax.broadcasted_iota(jnp.int32, (n, n), 0)
    ci = lax.broadcasted_iota(jnp.int32, (n, n), 1)
    a = jnp.where(ri == ci, jnp.sum(q * k, axis=-1, keepdims=True), 0.0)
    b, level = 1, 0
    while b < n:
        gb = _split_row(g, b)
        is_query = (row & b) != 0
        e = jnp.exp(jnp.where(is_query, g - gb, gb - g))
        z = (jnp.where(is_query, q, k) * e).astype(BF16)
        a = jnp.where(lv == level, lax.dot_general(z, z, _NT, preferred_element_type=F32), a)
        b, level = 2 * b, level + 1
    carried = lax.dot_general((q * jnp.exp(g)).astype(BF16), st.astype(BF16), _NT,
                              preferred_element_type=F32)
    o = _mm(a.astype(BF16), v.astype(BF16)) + carried
    g_end = g[n - 1:n, :]
    k_dec = (k * jnp.exp(g_end - g)).astype(BF16)
    st_new = st * jnp.exp(g_end) + lax.dot_general(v.astype(BF16), k_dec, _TN,
                                                   preferred_element_type=F32)
    return o, st_new


def _retention_chunk(q, k, v, log_gamma, st):
    n = q.shape[0]
    rowf = lax.broadcasted_iota(jnp.int32, q.shape, 0).astype(F32)
    ri = lax.broadcasted_iota(jnp.int32, (n, n), 0)
    ci = lax.broadcasted_iota(jnp.int32, (n, n), 1)
    decay = jnp.where(ci <= ri, jnp.exp((ri - ci).astype(F32) * log_gamma), 0.0)
    a = lax.dot_general(q.astype(BF16), k.astype(BF16), _NT, preferred_element_type=F32) * decay
    q_dec = (q * jnp.exp((rowf + 1.0) * log_gamma)).astype(BF16)
    o = _mm(a.astype(BF16), v.astype(BF16)) + _mm(q_dec, st.astype(BF16))
    k_dec = (k * jnp.exp((n - 1.0 - rowf) * log_gamma)).astype(BF16)
    st_new = st * math.exp(n * log_gamma) + lax.dot_general(k_dec, v.astype(BF16), _TN,
                                                            preferred_element_type=F32)
    return o, st_new


def _retention_log_gamma(h):
    return math.log(1.0 - 2.0 ** (-5.0 - h))


def _mixer_prompt_kernel(proj_ref, cos_ref, sin_ref, lv_ref, lbl_ref, na_ref,
                         o_ref, sa_out, sb_out, sat_ref, sb_ref, *, layer_slot, n_heads):
    t = pl.program_id(1)
    hd = HEAD_DIM
    w = n_heads * hd

    @pl.when(t == 0)
    def _():
        sat_ref[...] = jnp.zeros_like(sat_ref)
        sb_ref[...] = jnp.zeros_like(sb_ref)

    def cols(group, h):
        return proj_ref[0, :, group * w + h * hd:group * w + (h + 1) * hd]

    lb = _hgrn_lower_bound(lbl_ref, layer_slot)
    fz = lb + (1.0 - lb) * _sigmoid(proj_ref[0, :, w:2 * w])
    g_all = _cumsum_rows(jnp.log(fz))
    k_all = 1.0 - fz
    lv = lv_ref[...]
    for h in range(n_heads):
        sl = slice(h * hd, (h + 1) * hd)
        q = _silu(cols(0, h)) * hd ** -0.5
        o, st = _hgrn_chunk(q, k_all[:, sl], cols(2, h), g_all[:, sl], lv, sat_ref[h])
        sat_ref[h] = st
        o_ref[0, :, sl] = (_rms(o) * na_ref[...] * _silu(cols(3, h))).astype(o_ref.dtype)

    cos = cos_ref[...]
    sin = sin_ref[...]
    for h in range(n_heads):
        q = _rotate_pairs(cols(4, h), cos, sin)
        k = _rotate_pairs(cols(5, h), cos, sin) * hd ** -0.5
        o, st = _retention_chunk(q, k, cols(6, h), _retention_log_gamma(h), sb_ref[h])
        sb_ref[h] = st
        o_ref[0, :, w + h * hd:w + (h + 1) * hd] = (_rms(o) * _silu(cols(7, h))).astype(o_ref.dtype)

    @pl.when(t == pl.num_programs(1) - 1)
    def _():
        for h in range(n_heads):
            sa_out[0, h] = sat_ref[h].T
            sb_out[0, h] = sb_ref[h]


def _mixer_prompt(proj, cos, sin, lb_logits, norm_a, *, layer_slot, chunk):
    bsz, tlen, n = proj.shape
    hd = HEAD_DIM
    n_heads = n // (8 * hd)
    kern = functools.partial(_mixer_prompt_kernel, layer_slot=layer_slot, n_heads=n_heads)
    state = jax.ShapeDtypeStruct((bsz, n_heads, hd, hd), F32)
    state_spec = pl.BlockSpec((1, n_heads, hd, hd), lambda b, t: (b, 0, 0, 0))
    return pl.pallas_call(
        kern,
        grid=(bsz, tlen // chunk),
        in_specs=[pl.BlockSpec((1, chunk, n), lambda b, t: (b, t, 0)),
                  pl.BlockSpec((chunk, hd), lambda b, t: (t, 0)),
                  pl.BlockSpec((chunk, hd), lambda b, t: (t, 0)),
                  pl.BlockSpec((chunk, chunk), lambda b, t: (0, 0)),
                  pl.BlockSpec(lb_logits.shape, lambda b, t: (0, 0)),
                  pl.BlockSpec((1, hd), lambda b, t: (0, 0))],
        out_specs=[pl.BlockSpec((1, chunk, 2 * n_heads * hd), lambda b, t: (b, t, 0)),
                   state_spec, state_spec],
        out_shape=[jax.ShapeDtypeStruct((bsz, tlen, 2 * n_heads * hd), BF16), state, state],
        scratch_shapes=[pltpu.VMEM((n_heads, hd, hd), F32), pltpu.VMEM((n_heads, hd, hd), F32)],
        compiler_params=_params("arbitrary", "arbitrary"),
        name="mixer_prompt",
    )(proj, cos, sin, _level_matrix(chunk), lb_logits, norm_a.reshape(1, hd))


def _pair_chunk(q, k, v, g, s0, dec, kpad_ref, vpad_ref):
    tl = q.shape[0]
    row = lax.broadcasted_iota(jnp.int32, q.shape, 0)
    o = _mm((q * jnp.exp(g)).astype(BF16), s0.astype(BF16))
    for s in range(tl):
        live = row >= s
        w = jnp.exp(jnp.where(live, g - g[s:s + 1, :], 0.0))
        a = jnp.sum(jnp.where(live, q * w * k[s:s + 1, :], 0.0), axis=-1, keepdims=True)
        o = o + a * v[s:s + 1, :]
    kpad_ref[0:tl, :] = k * jnp.exp(g[tl - 1:tl, :] - g)
    vpad_ref[0:tl, :] = v
    upd = lax.dot_general(kpad_ref[...].astype(BF16), vpad_ref[...].astype(BF16), _TN,
                          preferred_element_type=F32)
    return o, s0 * dec + upd


def _mixer_sample_kernel(proj_ref, cos_ref, sin_ref, lbl_ref, na_ref, sa_ref, sb_ref,
                         o_ref, sa_out, sb_out, kpad_ref, vpad_ref, *, layer_slot, n_heads):
    hd = HEAD_DIM
    w = n_heads * hd
    tl = proj_ref.shape[1]
    kpad_ref[...] = jnp.zeros_like(kpad_ref)
    vpad_ref[...] = jnp.zeros_like(vpad_ref)

    def cols(group, h):
        return proj_ref[0, :, group * w + h * hd:group * w + (h + 1) * hd]

    lb = _hgrn_lower_bound(lbl_ref, layer_slot)
    fz = lb + (1.0 - lb) * _sigmoid(proj_ref[0, :, w:2 * w])
    lf = jnp.log(fz)
    row = lax.broadcasted_iota(jnp.int32, lf.shape, 0)
    g_all = jnp.zeros_like(lf)
    for j in range(tl):
        g_all = g_all + jnp.where(row >= j, lf[j:j + 1, :], 0.0)
    k_all = 1.0 - fz
    for h in range(n_heads):
        sl = slice(h * hd, (h + 1) * hd)
        g = g_all[:, sl]
        q = _silu(cols(0, h)) * hd ** -0.5
        dec = jnp.broadcast_to(jnp.exp(g[tl - 1:tl, :]), (hd, hd)).T
        o, st = _pair_chunk(q, k_all[:, sl], cols(2, h), g, sa_ref[0, h], dec, kpad_ref, vpad_ref)
        sa_out[0, h] = st
        o_ref[0, :, sl] = _rms(o) * na_ref[...] * _silu(cols(3, h))

    cos = cos_ref[...]
    sin = sin_ref[...]
    rowf = lax.broadcasted_iota(jnp.int32, (tl, hd), 0).astype(F32)
    for h in range(n_heads):
        lg = _retention_log_gamma(h)
        q = _rotate_pairs(cols(4, h), cos, sin)
        k = _rotate_pairs(cols(5, h), cos, sin) * hd ** -0.5
        o, st = _pair_chunk(q, k, cols(6, h), (rowf + 1.0) * lg, sb_ref[0, h], math.exp(tl * lg),
                            kpad_ref, vpad_ref)
        sb_out[0, h] = st
        o_ref[0, :, w + h * hd:w + (h + 1) * hd] = _rms(o) * _silu(cols(7, h))


def _mixer_sample(proj, cos, sin, lb_logits, norm_a, state_a, state_b, *, layer_slot):
    bsz, tl, n = proj.shape
    hd = HEAD_DIM
    n_heads = n // (8 * hd)
    kern = functools.partial(_mixer_sample_kernel, layer_slot=layer_slot, n_heads=n_heads)
    state = jax.ShapeDtypeStruct((bsz, n_heads, hd, hd), F32)
    state_spec = pl.BlockSpec((1, n_heads, hd, hd), lambda b: (b, 0, 0, 0))
    return pl.pallas_call(
        kern,
        grid=(bsz,),
        in_specs=[pl.BlockSpec((1, tl, n), lambda b: (b, 0, 0)),
                  pl.BlockSpec((tl, hd), lambda b: (0, 0)),
                  pl.BlockSpec((tl, hd), lambda b: (0, 0)),
                  pl.BlockSpec(lb_logits.shape, lambda b: (0, 0)),
                  pl.BlockSpec((1, hd), lambda b: (0, 0)),
                  state_spec, state_spec],
        out_specs=[pl.BlockSpec((1, tl, 2 * n_heads * hd), lambda b: (b, 0, 0)), state_spec, state_spec],
        out_shape=[jax.ShapeDtypeStruct((bsz, tl, 2 * n_heads * hd), F32), state, state],
        scratch_shapes=[pltpu.VMEM((hd, hd), F32), pltpu.VMEM((hd, hd), F32)],
        compiler_params=_params("arbitrary"),
        name="mixer_sample",
    )(proj, cos, sin, lb_logits, norm_a.reshape(1, hd), state_a, state_b)


def _diff_lambda(lam_ref, lam_init):
    lp = lam_ref[...]
    return (jnp.exp(jnp.sum(lp[0:1] * lp[1:2], axis=-1, keepdims=True))
            - jnp.exp(jnp.sum(lp[2:3] * lp[3:4], axis=-1, keepdims=True)) + lam_init)


def _online_softmax_step(s, v, m_ref, l_ref, acc_ref, select=None):
    m_old = m_ref[...]
    m_new = jnp.maximum(m_old, jnp.max(s, axis=-1, keepdims=True))
    alpha = jnp.exp(m_old - m_new)
    p = jnp.exp(s - m_new)
    l_ref[...] = alpha * l_ref[...] + jnp.sum(p, axis=-1, keepdims=True)
    pv = _mm(p.astype(BF16), v)
    if select is not None:
        pv = select(pv)
    acc_ref[...] = alpha * acc_ref[...] + pv
    m_ref[...] = m_new


def _attn_prompt_kernel(lam_ref, sw_ref, q_ref, k_ref, v_ref, o_ref, m_ref, l_ref, acc_ref, *,
                        tq, lam_init):
    qi = pl.program_id(2)
    dh = HEAD_DIM
    m_ref[...] = jnp.full_like(m_ref, -jnp.inf)
    l_ref[...] = jnp.zeros_like(l_ref)
    acc_ref[...] = jnp.zeros_like(acc_ref)

    def kv_tile(kj, diagonal):
        start = pl.multiple_of(kj * tq, tq)
        v = v_ref[0, pl.ds(start, tq), :].astype(BF16)
        for j in range(2):
            k = k_ref[0, pl.ds(start, tq), j * dh:(j + 1) * dh].astype(BF16)
            s = lax.dot_general(q_ref[0, :, j * dh:(j + 1) * dh], k, _NT, preferred_element_type=F32)
            if diagonal:
                ri = lax.broadcasted_iota(jnp.int32, s.shape, 0)
                ci = lax.broadcasted_iota(jnp.int32, s.shape, 1)
                s = jnp.where(ci <= ri, s, -jnp.inf)
            _online_softmax_step(s, v, m_ref.at[j], l_ref.at[j], acc_ref.at[j])

    def body(kj, carry):
        kv_tile(kj, False)
        return carry

    lax.fori_loop(0, qi, body, 0)
    kv_tile(qi, True)

    o = acc_ref[0] / l_ref[0] - _diff_lambda(lam_ref, lam_init) * (acc_ref[1] / l_ref[1])
    o_ref[0] = (_rms(o) * sw_ref[...] * (1.0 - lam_init)).astype(o_ref.dtype)


def _attn_prompt(q, k, v, lam_p, subln_w, *, lam_init, tq):
    bsz, tlen, n = q.shape
    e = 2 * HEAD_DIM
    n_heads = n // e
    kern = functools.partial(_attn_prompt_kernel, tq=tq, lam_init=lam_init)
    return pl.pallas_call(
        kern,
        grid=(bsz, n_heads, tlen // tq),
        in_specs=[pl.BlockSpec(lam_p.shape, lambda b, h, i: (0, 0)),
                  pl.BlockSpec((1, e), lambda b, h, i: (0, 0)),
                  pl.BlockSpec((1, tq, e), lambda b, h, i: (b, i, h)),
                  pl.BlockSpec((1, tlen, e), lambda b, h, i: (b, 0, h)),
                  pl.BlockSpec((1, tlen, e), lambda b, h, i: (b, 0, h))],
        out_specs=pl.BlockSpec((1, tq, e), lambda b, h, i: (b, i, h)),
        out_shape=jax.ShapeDtypeStruct((bsz, tlen, n), BF16),
        scratch_shapes=[pltpu.VMEM((2, tq, 1), F32), pltpu.VMEM((2, tq, 1), F32),
                        pltpu.VMEM((2, tq, e), F32)],
        compiler_params=_params("parallel", "parallel", "arbitrary"),
        name="diff_attn_prompt",
    )(lam_p, subln_w.reshape(1, e), q, k, v)


def _attn_sample_kernel(pt_ref, lam_ref, sw_ref, qr_ref, kn_ref, vn_ref, *rest, pages_per_step, lam_init):
    del pt_ref
    k_pages = rest[:pages_per_step]
    v_pages = rest[pages_per_step:2 * pages_per_step]
    o_ref, m_ref, l_ref, acc_ref, kpad_ref, vpad_ref = rest[2 * pages_per_step:]
    p = pl.program_id(1)
    e = 2 * HEAD_DIM
    n_rows = qr_ref.shape[1]
    tl = kn_ref.shape[1]
    n_heads = n_rows // (2 * tl)

    @pl.when(p == 0)
    def _():
        m_ref[...] = jnp.full_like(m_ref, -jnp.inf)
        l_ref[...] = jnp.zeros_like(l_ref)
        acc_ref[...] = jnp.zeros_like(acc_ref)

    qr = qr_ref[0]
    head_of_row = (lax.broadcasted_iota(jnp.int32, (n_rows, e), 0) % (n_heads * tl)) // tl

    def own_head(pv):
        out = pv[:, 0:e]
        for h in range(1, n_heads):
            out = jnp.where(head_of_row == h, pv[:, h * e:(h + 1) * e], out)
        return out

    def fold(k_page, v_page, mask):
        s = lax.dot_general(qr, k_page.astype(BF16), _NT, preferred_element_type=F32)
        if mask is not None:
            s = jnp.where(mask, s, -jnp.inf)
        _online_softmax_step(s, v_page.astype(BF16), m_ref, l_ref, acc_ref, own_head)

    for i in range(pages_per_step):
        fold(k_pages[i][0], v_pages[i][0], None)

    @pl.when(p == pl.num_programs(1) - 1)
    def _():
        kpad_ref[...] = jnp.zeros_like(kpad_ref)
        vpad_ref[...] = jnp.zeros_like(vpad_ref)
        kpad_ref[0:tl, :] = kn_ref[0]
        vpad_ref[0:tl, :] = vn_ref[0]
        n_keys = kpad_ref.shape[0]
        key = lax.broadcasted_iota(jnp.int32, (n_rows, n_keys), 1)
        tok = lax.broadcasted_iota(jnp.int32, (n_rows, n_keys), 0) % tl
        fold(kpad_ref[...], vpad_ref[...], key <= tok)
        half = n_rows // 2
        o = (acc_ref[0:half, :] / l_ref[0:half, :]
             - _diff_lambda(lam_ref, lam_init) * (acc_ref[half:, :] / l_ref[half:, :]))
        o = _rms(o) * sw_ref[...] * (1.0 - lam_init)
        for h in range(n_heads):
            o_ref[0, :, h * e:(h + 1) * e] = o[h * tl:(h + 1) * tl, :]


def _attn_sample(q_rows, k_new, v_new, cache_k, cache_v, page_table, lam_p, subln_w, *,
                 layer_slot, slot_pages, lam_init, pages_per_step):
    bsz, n_rows, n = q_rows.shape
    tl = k_new.shape[1]
    e = 2 * HEAD_DIM
    n_pages = page_table.shape[1]
    page = cache_k.shape[1]
    base = layer_slot * slot_pages
    steps = n_pages // pages_per_step

    def page_spec(i):
        return pl.BlockSpec(
            (1, page, n),
            lambda b, p, pt: (base + pt[b * n_pages + p * pages_per_step + i], 0, 0))

    kern = functools.partial(_attn_sample_kernel, pages_per_step=pages_per_step, lam_init=lam_init)
    grid_spec = pltpu.PrefetchScalarGridSpec(
        num_scalar_prefetch=1,
        grid=(bsz, steps),
        in_specs=[pl.BlockSpec(lam_p.shape, lambda b, p, pt: (0, 0)),
                  pl.BlockSpec((1, e), lambda b, p, pt: (0, 0)),
                  pl.BlockSpec((1, n_rows, n), lambda b, p, pt: (b, 0, 0)),
                  pl.BlockSpec((1, tl, n), lambda b, p, pt: (b, 0, 0)),
                  pl.BlockSpec((1, tl, n), lambda b, p, pt: (b, 0, 0))]
                 + [page_spec(i) for i in range(pages_per_step)] * 2,
        out_specs=pl.BlockSpec((1, tl, n), lambda b, p, pt: (b, 0, 0)),
        scratch_shapes=[pltpu.VMEM((n_rows, 1), F32), pltpu.VMEM((n_rows, 1), F32),
                        pltpu.VMEM((n_rows, e), F32),
                        pltpu.VMEM((page, n), F32), pltpu.VMEM((page, n), F32)],
    )
    return pl.pallas_call(
        kern,
        grid_spec=grid_spec,
        out_shape=jax.ShapeDtypeStruct((bsz, tl, n), F32),
        compiler_params=_params("arbitrary", "arbitrary"),
        name="diff_attn_sample",
    )(page_table.reshape(-1), lam_p, subln_w.reshape(1, e), q_rows, k_new, v_new,
      *([cache_k] * pages_per_step), *([cache_v] * pages_per_step))


def _block_diag_queries(q, n_heads):
    tl = q.shape[1]
    r = np.arange(2 * n_heads * tl)
    block = ((r % (n_heads * tl)) // tl) * 2 + r // (n_heads * tl)
    keep = (np.arange(q.shape[2])[None, :] // HEAD_DIM) == block[:, None]
    return jnp.where(jnp.asarray(keep)[None], q[:, r % tl, :], jnp.zeros((), q.dtype))


def _rope_tables(pos0, tlen):
    inv = 1.0 / (ROPE_BASE ** jnp.linspace(0.0, 1.0, HEAD_DIM // 2, dtype=F32))
    ang = (pos0 + jnp.arange(tlen)).astype(F32)[:, None] * inv[None, :]
    sign = jnp.tile(jnp.asarray([-1.0, 1.0], F32), HEAD_DIM // 2)
    return jnp.repeat(jnp.cos(ang), 2, axis=1), jnp.repeat(jnp.sin(ang), 2, axis=1) * sign


def _trunk(x, mod, pos0, state_a, state_b, cache_k, cache_v, page_table, w, *, tm, prompt):
    bsz, tlen, d = x.shape
    depth = mod.shape[0]
    xf = x.reshape(bsz * tlen, d)
    hg_out, rt_out, k_out, v_out = [], [], [], []
    for l in range(depth):
        ml = mod[l]
        if l % 2 == 0:
            e = l // 2
            n_in = w["in_ab"].shape[2]
            (proj,) = _inproj(xf, ml, w["norm"][l, 0], w["in_ab"][e], [(n_in, F32, 1.0)],
                              tm=tm, shift_col=0, scale_col=1)
            proj = proj.reshape(bsz, tlen, n_in)
            cos, sin = _rope_tables(pos0, tlen)
            if prompt:
                o, s_a, s_b = _mixer_prompt(proj, cos, sin, w["lb_logits"], w["hgrn_norm"][e],
                                            layer_slot=e, chunk=128)
            else:
                o, s_a, s_b = _mixer_sample(proj, cos, sin, w["lb_logits"], w["hgrn_norm"][e],
                                            state_a[e], state_b[e], layer_slot=e)
            hg_out.append(s_a)
            rt_out.append(s_b)
            xf = _outproj(o.reshape(bsz * tlen, -1), xf, ml, w["out_ab"][e], tm=tm, gate_col=2)
        else:
            a = l // 2
            lam_init = 0.8 - 0.6 * math.exp(-0.3 * l)
            n_c = w["in_c"].shape[2] // 3
            n_heads = n_c // (2 * HEAD_DIM)
            q, k, v = _inproj(xf, ml, w["norm"][l, 0], w["in_c"][a],
                              [(n_c, BF16, HEAD_DIM ** -0.5), (n_c, F32, 1.0), (n_c, F32, 1.0)],
                              tm=tm, shift_col=0, scale_col=1)
            q, k, v = (t.reshape(bsz, tlen, n_c) for t in (q, k, v))
            if prompt:
                o = _attn_prompt(q, k, v, w["diff_lambda"][a], w["subln"][a], lam_init=lam_init,
                                 tq=min(512, tlen))
            else:
                o = _attn_sample(_block_diag_queries(q, n_heads), k, v, cache_k, cache_v, page_table,
                                 w["diff_lambda"][a], w["subln"][a], layer_slot=a,
                                 slot_pages=cache_k.shape[0] // (depth // 2), lam_init=lam_init,
                                 pages_per_step=4)
            k_out.append(k)
            v_out.append(v)
            xf = _outproj(o.reshape(bsz * tlen, -1), xf, ml, w["out_c"][a], tm=tm, gate_col=2)
        xf = _mlp(xf, ml, w["norm"][l, 1], w["mlp_up"][l], w["mlp_down"][l], w["final_norm"],
                  tm=tm, final_norm=(l == depth - 1))
    return (xf.reshape(bsz, tlen, d), jnp.stack(hg_out), jnp.stack(rt_out),
            jnp.stack(k_out), jnp.stack(v_out))


def kernel(x_prompt, x_sample, state_hgrn, state_ret, cache_k, cache_v, page_table, c_prompt, c_sample,
           w_ada, b_ada, norm_w, w_in_ab, w_out_ab, hgrn_lb_logits, hgrn_norm_w, w_in_c, w_out_c,
           diff_lambda, diff_subln_w, w_mlp_up, w_mlp_down, final_norm_w):
    bp, tp, d = x_prompt.shape
    bs, ts, _ = x_sample.shape
    n_slots, n_phys, page, n_heads_c, _, dh = cache_k.shape
    n_kv = n_heads_c * 2 * dh

    mod = _ada(jnp.concatenate([c_prompt, c_sample], axis=0), w_ada, b_ada)
    depth = mod.shape[0]
    mod_p = mod[:, :bp].reshape(depth, bp, 1, 6 * d)
    mod_s = jnp.repeat(mod[:, bp:], ts, axis=1).reshape(depth, 1, bs * ts, 6 * d)

    w = dict(norm=norm_w, in_ab=w_in_ab.astype(BF16), out_ab=w_out_ab.astype(BF16),
             lb_logits=hgrn_lb_logits, hgrn_norm=hgrn_norm_w, in_c=w_in_c.astype(BF16),
             out_c=w_out_c.astype(BF16), diff_lambda=diff_lambda, subln=diff_subln_w,
             mlp_up=w_mlp_up.astype(BF16), mlp_down=w_mlp_down.astype(BF16), final_norm=final_norm_w)

    y_p, hg_p, rt_p, k_p, v_p = _trunk(x_prompt, mod_p, 0, None, None, None, None, None, w,
                                       tm=512, prompt=True)
    past_len = page_table.shape[1] * page
    y_s, hg_s, rt_s, k_s, v_s = _trunk(
        x_sample, mod_s, past_len, state_hgrn, state_ret,
        cache_k.reshape(n_slots * n_phys, page, n_kv), cache_v.reshape(n_slots * n_phys, page, n_kv),
        page_table, w, tm=bs * ts, prompt=False)

    n_odd = k_p.shape[0]
    k_p = k_p.reshape(n_odd, bp, tp // page, page, n_heads_c, 2, dh)
    v_p = v_p.reshape(n_odd, bp, tp // page, page, n_heads_c, 2 * dh)
    k_s = k_s.reshape(n_odd, bs, ts, n_heads_c, 2, dh)
    v_s = v_s.reshape(n_odd, bs, ts, n_heads_c, 2 * dh)
    return (y_p, y_s, hg_p, rt_p, k_p, v_p, hg_s, rt_s, k_s, v_s)
```

```python
import functools
import math

import jax
import jax.numpy as jnp
import numpy as np
from jax import lax
from jax.experimental import pallas as pl
from jax.experimental.pallas import tpu as pltpu

F32 = jnp.float32
BF16 = jnp.bfloat16

HEAD_DIM = 128
SUBLANES = 8
NORM_EPS = 1e-6
ROPE_BASE = 10000.0
V7X_VMEM_LIMIT = 52 * 1024 * 1024

_NT = (((1,), (1,)), ((), ()))
_TN = (((0,), (0,)), ((), ()))


def _params(*semantics):
    return pltpu.CompilerParams(dimension_semantics=semantics, vmem_limit_bytes=V7X_VMEM_LIMIT)


def _resident(shape):
    zeros = (0,) * len(shape)
    return pl.BlockSpec(shape, lambda *_: zeros, pipeline_mode=pl.Buffered(1))


def _sigmoid(x):
    return 1.0 / (1.0 + jnp.exp(-x))


def _silu(x):
    return x * _sigmoid(x)


def _rms(x):
    return x * lax.rsqrt(jnp.mean(x * x, axis=-1, keepdims=True) + NORM_EPS)


def _mm(a, b):
    return jnp.dot(a, b, preferred_element_type=F32)


def _ada_kernel(c_ref, w_ref, b_ref, o_ref):
    s = _silu(c_ref[...]).astype(BF16)
    o_ref[0] = _mm(s, w_ref[0].astype(BF16)) + b_ref[0]


def _ada(c, w_ada, b_ada):
    depth, d, n = w_ada.shape
    r = c.shape[0]
    tn = n // 4
    return pl.pallas_call(
        _ada_kernel,
        grid=(depth, n // tn),
        in_specs=[pl.BlockSpec((r, d), lambda l, j: (0, 0)),
                  pl.BlockSpec((1, d, tn), lambda l, j: (l, 0, j)),
                  pl.BlockSpec((1, 1, tn), lambda l, j: (l, 0, j))],
        out_specs=pl.BlockSpec((1, r, tn), lambda l, j: (l, 0, j)),
        out_shape=jax.ShapeDtypeStruct((depth, r, n), F32),
        compiler_params=_params("arbitrary", "arbitrary"),
        name="ada_mod",
    )(c, w_ada, b_ada.reshape(depth, 1, n))


def _modulated_norm(x, nw_ref, mod_ref, shift_col, scale_col):
    d = x.shape[-1]
    shift = mod_ref[0, :, shift_col * d:(shift_col + 1) * d]
    scale = mod_ref[0, :, scale_col * d:(scale_col + 1) * d]
    return _rms(x) * nw_ref[...] * (1.0 + scale) + shift


def _inproj_kernel(x_ref, mod_ref, nw_ref, w_ref, *out_refs, shift_col, scale_col, out_scales, tc):
    h = _modulated_norm(x_ref[...], nw_ref, mod_ref, shift_col, scale_col).astype(BF16)
    off = 0
    for o_ref, out_scale in zip(out_refs, out_scales):
        width = o_ref.shape[-1]
        for c in range(0, width, tc):
            y = _mm(h, w_ref[:, off + c:off + c + tc])
            if out_scale != 1.0:
                y = y * out_scale
            o_ref[:, c:c + tc] = y.astype(o_ref.dtype)
        off += width


def _inproj(x, mod, nw, w, outs, *, tm, shift_col, scale_col):
    m, d = x.shape
    n = w.shape[1]
    rows_per_mod = m // mod.shape[0]
    tiles_per_mod = rows_per_mod // tm
    kern = functools.partial(_inproj_kernel, shift_col=shift_col, scale_col=scale_col,
                             out_scales=tuple(o[2] for o in outs), tc=512)
    return pl.pallas_call(
        kern,
        grid=(m // tm,),
        in_specs=[pl.BlockSpec((tm, d), lambda i: (i, 0)),
                  pl.BlockSpec((1, mod.shape[1], mod.shape[2]), lambda i: (i // tiles_per_mod, 0, 0)),
                  _resident((1, d)),
                  _resident((d, n))],
        out_specs=[pl.BlockSpec((tm, o[0]), lambda i: (i, 0)) for o in outs],
        out_shape=[jax.ShapeDtypeStruct((m, o[0]), o[1]) for o in outs],
        compiler_params=_params("parallel"),
        name="norm_mod_inproj",
    )(x, mod, nw.reshape(1, d), w)


def _outproj_kernel(o_ref, x_ref, mod_ref, w_ref, out_ref, *, gate_col):
    d = x_ref.shape[-1]
    gate = mod_ref[0, :, gate_col * d:(gate_col + 1) * d]
    out_ref[...] = x_ref[...] + gate * _mm(o_ref[...].astype(BF16), w_ref[...])


def _outproj(o, x, mod, w, *, tm, gate_col):
    m, d = x.shape
    k = o.shape[1]
    tiles_per_mod = (m // mod.shape[0]) // tm
    return pl.pallas_call(
        functools.partial(_outproj_kernel, gate_col=gate_col),
        grid=(m // tm,),
        in_specs=[pl.BlockSpec((tm, k), lambda i: (i, 0)),
                  pl.BlockSpec((tm, d), lambda i: (i, 0)),
                  pl.BlockSpec((1, mod.shape[1], mod.shape[2]), lambda i: (i // tiles_per_mod, 0, 0)),
                  _resident((k, d))],
        out_specs=pl.BlockSpec((tm, d), lambda i: (i, 0)),
        out_shape=jax.ShapeDtypeStruct((m, d), F32),
        compiler_params=_params("parallel"),
        name="outproj_residual",
    )(o, x, mod, w)


def _mlp_kernel(x_ref, mod_ref, nw_ref, wu_ref, wd_ref, fw_ref, out_ref, u_ref, *,
                shift_col, scale_col, gate_col, tf, final_norm):
    x = x_ref[...]
    d = x.shape[-1]
    h = _modulated_norm(x, nw_ref, mod_ref, shift_col, scale_col).astype(BF16)
    for c in range(0, wu_ref.shape[1], tf):
        u = jnp.maximum(_mm(h, wu_ref[:, c:c + tf]), 0.0)
        u_ref[:, c:c + tf] = (u * u).astype(BF16)
    gate = mod_ref[0, :, gate_col * d:(gate_col + 1) * d]
    y = x + gate * _mm(u_ref[...], wd_ref[...])
    if final_norm:
        y = _rms(y) * fw_ref[...]
    out_ref[...] = y


def _mlp(x, mod, nw, w_up, w_down, final_w, *, tm, final_norm):
    m, d = x.shape
    dff = w_up.shape[1]
    tiles_per_mod = (m // mod.shape[0]) // tm
    kern = functools.partial(_mlp_kernel, shift_col=3, scale_col=4, gate_col=5, tf=512,
                             final_norm=final_norm)
    return pl.pallas_call(
        kern,
        grid=(m // tm,),
        in_specs=[pl.BlockSpec((tm, d), lambda i: (i, 0)),
                  pl.BlockSpec((1, mod.shape[1], mod.shape[2]), lambda i: (i // tiles_per_mod, 0, 0)),
                  _resident((1, d)),
                  _resident((d, dff)),
                  _resident((dff, d)),
                  _resident((1, d))],
        out_specs=pl.BlockSpec((tm, d), lambda i: (i, 0)),
        out_shape=jax.ShapeDtypeStruct((m, d), F32),
        scratch_shapes=[pltpu.VMEM((tm, dff), BF16)],
        compiler_params=_params("parallel"),
        name="mlp_residual",
    )(x, mod, nw.reshape(1, d), w_up, w_down, final_w.reshape(1, d))


def _hgrn_lower_bound(logits_ref, layer_slot):
    lg = logits_ref[...]
    ex = jnp.exp(lg - jnp.max(lg, axis=0, keepdims=True))
    pr = ex / jnp.sum(ex, axis=0, keepdims=True)
    return jnp.sum(pr[:layer_slot + 1], axis=0, keepdims=True)


def _rotate_pairs(x, cos, sin_signed):
    n = x.shape[-1]
    lane = lax.broadcasted_iota(jnp.int32, x.shape, 1)
    swapped = jnp.where((lane & 1) == 0, pltpu.roll(x, n - 1, 1), pltpu.roll(x, 1, 1))
    return x * cos + swapped * sin_signed


def _cumsum_rows(x):
    n = x.shape[0]
    row = lax.broadcasted_iota(jnp.int32, x.shape, 0)
    d = 1
    while d < n:
        x = x + jnp.where(row >= d, pltpu.roll(x, d, 0), 0.0)
        d *= 2
    return x


def _split_row(g, b):
    n, lanes = g.shape
    if 2 * b >= SUBLANES:
        g3 = g.reshape(n // (2 * b), 2 * b, lanes)
        return jnp.broadcast_to(g3[:, b - 1:b, :], g3.shape).reshape(n, lanes)
    rmod = lax.broadcasted_iota(jnp.int32, g.shape, 0) & (2 * b - 1)
    out = g
    for r in range(2 * b):
        if r != b - 1:
            out = jnp.where(rmod == r, pltpu.roll(g, (r - (b - 1)) % n, 0), out)
    return out


def _level_matrix(n):
    t = np.arange(n)[:, None]
    s = np.arange(n)[None, :]
    hi = np.floor(np.log2(np.maximum(t ^ s, 1))).astype(np.int32)
    return jnp.asarray(np.where(s < t, hi, -1).astype(np.int32))


def _hgrn_chunk(q, k, v, g, lv, st):
    n = q.shape[0]
    row = lax.broadcasted_iota(jnp.int32, q.shape, 0)
    ri = lax.broadcasted_iota(jnp.int32, (n, n), 0)
    ci = lax.broadcasted_iota(jnp.int32, (n, n), 1)
    a = jnp.where(ri == ci, jnp.sum(q * k, axis=-1, keepdims=True), 0.0)
    b, level = 1, 0
    while b < n:
        gb = _split_row(g, b)
        is_query = (row & b) != 0
        e = jnp.exp(jnp.where(is_query, g - gb, gb - g))
        z = (jnp.where(is_query, q, k) * e).astype(BF16)
        a = jnp.where(lv == level, lax.dot_general(z, z, _NT, preferred_element_type=F32), a)
        b, level = 2 * b, level + 1
    carried = lax.dot_general((q * jnp.exp(g)).astype(BF16), st.astype(BF16), _NT,
                              preferred_element_type=F32)
    o = _mm(a.astype(BF16), v.astype(BF16)) + carried
    g_end = g[n - 1:n, :]
    k_dec = (k * jnp.exp(g_end - g)).astype(BF16)
    st_new = st * jnp.exp(g_end) + lax.dot_general(v.astype(BF16), k_dec, _TN,
                                                   preferred_element_type=F32)
    return o, st_new


def _retention_chunk(q, k, v, log_gamma, st):
    n = q.shape[0]
    rowf = lax.broadcasted_iota(jnp.int32, q.shape, 0).astype(F32)
    ri = lax.broadcasted_iota(jnp.int32, (n, n), 0)
    ci = lax.broadcasted_iota(jnp.int32, (n, n), 1)
    decay = jnp.where(ci <= ri, jnp.exp((ri - ci).astype(F32) * log_gamma), 0.0)
    a = lax.dot_general(q.astype(BF16), k.astype(BF16), _NT, preferred_element_type=F32) * decay
    q_dec = (q * jnp.exp((rowf + 1.0) * log_gamma)).astype(BF16)
    o = _mm(a.astype(BF16), v.astype(BF16)) + _mm(q_dec, st.astype(BF16))
    k_dec = (k * jnp.exp((n - 1.0 - rowf) * log_gamma)).astype(BF16)
    st_new = st * math.exp(n * log_gamma) + lax.dot_general(k_dec, v.astype(BF16), _TN,
                                                            preferred_element_type=F32)
    return o, st_new


def _retention_log_gamma(h):
    return math.log(1.0 - 2.0 ** (-5.0 - h))


def _mixer_prompt_kernel(proj_ref, cos_ref, sin_ref, lv_ref, lbl_ref, na_ref,
                         o_ref, sa_out, sb_out, sat_ref, sb_ref, *, layer_slot, n_heads):
    t = pl.program_id(1)
    hd = HEAD_DIM
    w = n_heads * hd

    @pl.when(t == 0)
    def _():
        sat_ref[...] = jnp.zeros_like(sat_ref)
        sb_ref[...] = jnp.zeros_like(sb_ref)

    def cols(group, h):
        return proj_ref[0, :, group * w + h * hd:group * w + (h + 1) * hd]

    lb = _hgrn_lower_bound(lbl_ref, layer_slot)
    fz = lb + (1.0 - lb) * _sigmoid(proj_ref[0, :, w:2 * w])
    g_all = _cumsum_rows(jnp.log(fz))
    k_all = 1.0 - fz
    lv = lv_ref[...]
    for h in range(n_heads):
        sl = slice(h * hd, (h + 1) * hd)
        q = _silu(cols(0, h)) * hd ** -0.5
        o, st = _hgrn_chunk(q, k_all[:, sl], cols(2, h), g_all[:, sl], lv, sat_ref[h])
        sat_ref[h] = st
        o_ref[0, :, sl] = (_rms(o) * na_ref[...] * _silu(cols(3, h))).astype(o_ref.dtype)

    cos = cos_ref[...]
    sin = sin_ref[...]
    for h in range(n_heads):
        q = _rotate_pairs(cols(4, h), cos, sin)
        k = _rotate_pairs(cols(5, h), cos, sin) * hd ** -0.5
        o, st = _retention_chunk(q, k, cols(6, h), _retention_log_gamma(h), sb_ref[h])
        sb_ref[h] = st
        o_ref[0, :, w + h * hd:w + (h + 1) * hd] = (_rms(o) * _silu(cols(7, h))).astype(o_ref.dtype)

    @pl.when(t == pl.num_programs(1) - 1)
    def _():
        for h in range(n_heads):
            sa_out[0, h] = sat_ref[h].T
            sb_out[0, h] = sb_ref[h]


def _mixer_prompt(proj, cos, sin, lb_logits, norm_a, *, layer_slot, chunk):
    bsz, tlen, n = proj.shape
    hd = HEAD_DIM
    n_heads = n // (8 * hd)
    kern = functools.partial(_mixer_prompt_kernel, layer_slot=layer_slot, n_heads=n_heads)
    state = jax.ShapeDtypeStruct((bsz, n_heads, hd, hd), F32)
    state_spec = pl.BlockSpec((1, n_heads, hd, hd), lambda b, t: (b, 0, 0, 0))
    return pl.pallas_call(
        kern,
        grid=(bsz, tlen // chunk),
        in_specs=[pl.BlockSpec((1, chunk, n), lambda b, t: (b, t, 0)),
                  pl.BlockSpec((chunk, hd), lambda b, t: (t, 0)),
                  pl.BlockSpec((chunk, hd), lambda b, t: (t, 0)),
                  pl.BlockSpec((chunk, chunk), lambda b, t: (0, 0)),
                  pl.BlockSpec(lb_logits.shape, lambda b, t: (0, 0)),
                  pl.BlockSpec((1, hd), lambda b, t: (0, 0))],
        out_specs=[pl.BlockSpec((1, chunk, 2 * n_heads * hd), lambda b, t: (b, t, 0)),
                   state_spec, state_spec],
        out_shape=[jax.ShapeDtypeStruct((bsz, tlen, 2 * n_heads * hd), BF16), state, state],
        scratch_shapes=[pltpu.VMEM((n_heads, hd, hd), F32), pltpu.VMEM((n_heads, hd, hd), F32)],
        compiler_params=_params("arbitrary", "arbitrary"),
        name="mixer_prompt",
    )(proj, cos, sin, _level_matrix(chunk), lb_logits, norm_a.reshape(1, hd))


def _pair_chunk(q, k, v, g, s0, dec, kpad_ref, vpad_ref):
    tl = q.shape[0]
    row = lax.broadcasted_iota(jnp.int32, q.shape, 0)
    o = _mm((q * jnp.exp(g)).astype(BF16), s0.astype(BF16))
    for s in range(tl):
        live = row >= s
        w = jnp.exp(jnp.where(live, g - g[s:s + 1, :], 0.0))
        a = jnp.sum(jnp.where(live, q * w * k[s:s + 1, :], 0.0), axis=-1, keepdims=True)
        o = o + a * v[s:s + 1, :]
    kpad_ref[0:tl, :] = k * jnp.exp(g[tl - 1:tl, :] - g)
    vpad_ref[0:tl, :] = v
    upd = lax.dot_general(kpad_ref[...].astype(BF16), vpad_ref[...].astype(BF16), _TN,
                          preferred_element_type=F32)
    return o, s0 * dec + upd


def _mixer_sample_kernel(proj_ref, cos_ref, sin_ref, lbl_ref, na_ref, sa_ref, sb_ref,
                         o_ref, sa_out, sb_out, kpad_ref, vpad_ref, *, layer_slot, n_heads):
    hd = HEAD_DIM
    w = n_heads * hd
    tl = proj_ref.shape[1]
    kpad_ref[...] = jnp.zeros_like(kpad_ref)
    vpad_ref[...] = jnp.zeros_like(vpad_ref)

    def cols(group, h):
        return proj_ref[0, :, group * w + h * hd:group * w + (h + 1) * hd]

    lb = _hgrn_lower_bound(lbl_ref, layer_slot)
    fz = lb + (1.0 - lb) * _sigmoid(proj_ref[0, :, w:2 * w])
    lf = jnp.log(fz)
    row = lax.broadcasted_iota(jnp.int32, lf.shape, 0)
    g_all = jnp.zeros_like(lf)
    for j in range(tl):
        g_all = g_all + jnp.where(row >= j, lf[j:j + 1, :], 0.0)
    k_all = 1.0 - fz
    for h in range(n_heads):
        sl = slice(h * hd, (h + 1) * hd)
        g = g_all[:, sl]
        q = _silu(cols(0, h)) * hd ** -0.5
        dec = jnp.broadcast_to(jnp.exp(g[tl - 1:tl, :]), (hd, hd)).T
        o, st = _pair_chunk(q, k_all[:, sl], cols(2, h), g, sa_ref[0, h], dec, kpad_ref, vpad_ref)
        sa_out[0, h] = st
        o_ref[0, :, sl] = _rms(o) * na_ref[...] * _silu(cols(3, h))

    cos = cos_ref[...]
    sin = sin_ref[...]
    rowf = lax.broadcasted_iota(jnp.int32, (tl, hd), 0).astype(F32)
    for h in range(n_heads):
        lg = _retention_log_gamma(h)
        q = _rotate_pairs(cols(4, h), cos, sin)
        k = _rotate_pairs(cols(5, h), cos, sin) * hd ** -0.5
        o, st = _pair_chunk(q, k, cols(6, h), (rowf + 1.0) * lg, sb_ref[0, h], math.exp(tl * lg),
                            kpad_ref, vpad_ref)
        sb_out[0, h] = st
        o_ref[0, :, w + h * hd:w + (h + 1) * hd] = _rms(o) * _silu(cols(7, h))


def _mixer_sample(proj, cos, sin, lb_logits, norm_a, state_a, state_b, *, layer_slot):
    bsz, tl, n = proj.shape
    hd = HEAD_DIM
    n_heads = n // (8 * hd)
    kern = functools.partial(_mixer_sample_kernel, layer_slot=layer_slot, n_heads=n_heads)
    state = jax.ShapeDtypeStruct((bsz, n_heads, hd, hd), F32)
    state_spec = pl.BlockSpec((1, n_heads, hd, hd), lambda b: (b, 0, 0, 0))
    return pl.pallas_call(
        kern,
        grid=(bsz,),
        in_specs=[pl.BlockSpec((1, tl, n), lambda b: (b, 0, 0)),
                  pl.BlockSpec((tl, hd), lambda b: (0, 0)),
                  pl.BlockSpec((tl, hd), lambda b: (0, 0)),
                  pl.BlockSpec(lb_logits.shape, lambda b: (0, 0)),
                  pl.BlockSpec((1, hd), lambda b: (0, 0)),
                  state_spec, state_spec],
        out_specs=[pl.BlockSpec((1, tl, 2 * n_heads * hd), lambda b: (b, 0, 0)), state_spec, state_spec],
        out_shape=[jax.ShapeDtypeStruct((bsz, tl, 2 * n_heads * hd), F32), state, state],
        scratch_shapes=[pltpu.VMEM((hd, hd), F32), pltpu.VMEM((hd, hd), F32)],
        compiler_params=_params("arbitrary"),
        name="mixer_sample",
    )(proj, cos, sin, lb_logits, norm_a.reshape(1, hd), state_a, state_b)


def _diff_lambda(lam_ref, lam_init):
    lp = lam_ref[...]
    return (jnp.exp(jnp.sum(lp[0:1] * lp[1:2], axis=-1, keepdims=True))
            - jnp.exp(jnp.sum(lp[2:3] * lp[3:4], axis=-1, keepdims=True)) + lam_init)


LOG2_E = math.log2(math.e)


def _attn_prompt_kernel(lam_ref, sw_ref, q_ref, k_ref, v_ref, o_ref, m_ref, l_ref, acc_ref, *,
                        tq, lam_init):
    qi = pl.program_id(2)
    dh = HEAD_DIM
    lanes = m_ref.shape[-1]
    m_ref[...] = jnp.full_like(m_ref, -jnp.inf)
    l_ref[...] = jnp.zeros_like(l_ref)
    acc_ref[...] = jnp.zeros_like(acc_ref)

    def kv_tile(kj, diagonal):
        start = pl.multiple_of(kj * tq, tq)
        v = v_ref[0, pl.ds(start, tq), :].astype(BF16)
        scores = []
        for j in range(2):
            k = k_ref[0, pl.ds(start, tq), j * dh:(j + 1) * dh].astype(BF16)
            s = lax.dot_general(q_ref[0, :, j * dh:(j + 1) * dh], k, _NT, preferred_element_type=F32)
            if diagonal:
                ri = lax.broadcasted_iota(jnp.int32, s.shape, 0)
                ci = lax.broadcasted_iota(jnp.int32, s.shape, 1)
                s = jnp.where(ci <= ri, s, -jnp.inf)
            scores.append(s)
        probs, alphas = [], []
        for j in range(2):
            s = scores[j]
            m_old = m_ref[j]
            m_new = jnp.maximum(m_old, jnp.max(s, axis=-1, keepdims=True))
            alpha = jnp.exp2(m_old - m_new)
            l_new = alpha * l_ref[j]
            tiles = []
            for c in range(0, s.shape[1], lanes):
                p = jnp.exp2(s[:, c:c + lanes] - m_new)
                l_new = l_new + p
                tiles.append(p.astype(BF16))
            l_ref[j] = l_new
            m_ref[j] = m_new
            probs.append(jnp.concatenate(tiles, axis=1))
            alphas.append(alpha)
        for j in range(2):
            pv = _mm(probs[j], v)
            for c in range(0, pv.shape[1], lanes):
                acc_ref[j, :, c:c + lanes] = alphas[j] * acc_ref[j, :, c:c + lanes] + pv[:, c:c + lanes]

    def body(kj, carry):
        kv_tile(kj, False)
        return carry

    lax.fori_loop(0, qi, body, 0)
    kv_tile(qi, True)

    l0 = jnp.sum(l_ref[0], axis=-1, keepdims=True)
    l1 = jnp.sum(l_ref[1], axis=-1, keepdims=True)
    o = acc_ref[0] / l0 - _diff_lambda(lam_ref, lam_init) * (acc_ref[1] / l1)
    o_ref[0] = (_rms(o) * sw_ref[...] * (1.0 - lam_init)).astype(o_ref.dtype)


def _attn_prompt(q, k, v, lam_p, subln_w, *, lam_init, tq):
    bsz, tlen, n = q.shape
    e = 2 * HEAD_DIM
    n_heads = n // e
    kern = functools.partial(_attn_prompt_kernel, tq=tq, lam_init=lam_init)
    return pl.pallas_call(
        kern,
        grid=(bsz, n_heads, tlen // tq),
        in_specs=[pl.BlockSpec(lam_p.shape, lambda b, h, i: (0, 0)),
                  pl.BlockSpec((1, e), lambda b, h, i: (0, 0)),
                  pl.BlockSpec((1, tq, e), lambda b, h, i: (b, i, h)),
                  pl.BlockSpec((1, tlen, e), lambda b, h, i: (b, 0, h)),
                  pl.BlockSpec((1, tlen, e), lambda b, h, i: (b, 0, h))],
        out_specs=pl.BlockSpec((1, tq, e), lambda b, h, i: (b, i, h)),
        out_shape=jax.ShapeDtypeStruct((bsz, tlen, n), BF16),
        scratch_shapes=[pltpu.VMEM((2, tq, HEAD_DIM), F32), pltpu.VMEM((2, tq, HEAD_DIM), F32),
                        pltpu.VMEM((2, tq, e), F32)],
        compiler_params=_params("parallel", "parallel", "arbitrary"),
        name="diff_attn_prompt",
    )(lam_p, subln_w.reshape(1, e), q, k, v)


def _sample_row_fields(shape, tl, n_heads):
    r = lax.broadcasted_iota(jnp.int32, shape, 0)
    head = n_heads - 1 - r % n_heads
    half = (r // n_heads) % 2
    token = (r // (2 * n_heads)) % tl
    qmap = r // (2 * n_heads * tl)
    return head, qmap, token, half


def _attn_sample_kernel(pt_ref, lam_ref, sw_ref, q_ref, kn_ref, vn_ref, *rest,
                        pages_per_step, tl, n_heads, lam_init):
    del pt_ref
    k_pages = rest[:pages_per_step]
    v_pages = rest[pages_per_step:2 * pages_per_step]
    o_ref, m_ref, l_ref, acc_ref, kpad_ref, vpad_ref = rest[2 * pages_per_step:]
    p = pl.program_id(1)
    group = 2 * n_heads
    n_rows = q_ref.shape[1]
    rows_per_map = n_rows // 2

    @pl.when(p == 0)
    def _():
        m_ref[...] = jnp.full_like(m_ref, -jnp.inf)
        l_ref[...] = jnp.zeros_like(l_ref)
        acc_ref[...] = jnp.zeros_like(acc_ref)

    q = q_ref[0]

    def own_lanes(n_keys, causal):
        head, qmap, token, _ = _sample_row_fields((n_rows, n_keys), tl, n_heads)
        key = lax.broadcasted_iota(jnp.int32, (n_rows, n_keys), 1)
        own = (key % group) == 2 * head + qmap
        return jnp.logical_and(own, key // group <= token) if causal else own

    def partial_softmax(pages, causal):
        own = own_lanes(pages[0][0].shape[0], causal)
        scores = [lax.dot_general(q, k_rows.astype(BF16), _NT, preferred_element_type=F32)
                  for k_rows, _ in pages]
        stats, probs = [], []
        for s in scores:
            s = jnp.where(own, s, -jnp.inf)
            m = jnp.max(s, axis=-1, keepdims=True)
            pexp = jnp.exp2(s - m)
            stats.append((m, jnp.sum(pexp, axis=-1, keepdims=True)))
            aligned = []
            for blk in range(n_rows // SUBLANES):
                shift = group - (n_heads - 1) - (blk * SUBLANES) // rows_per_map
                aligned.append(pltpu.roll(pexp[blk * SUBLANES:(blk + 1) * SUBLANES, :], shift, 1,
                                          stride=1, stride_axis=0))
            probs.append(jnp.concatenate(aligned, axis=0).astype(BF16))
        return [(m, l, _mm(p_rows, pltpu.roll(v_rows, group, 0).astype(BF16)))
                for (m, l), p_rows, (_, v_rows) in zip(stats, probs, pages)]

    def merge(parts):
        m_run = m_ref[...]
        m_new = m_run
        for m, _, _ in parts:
            m_new = jnp.maximum(m_new, m)
        alpha = jnp.exp2(m_run - m_new)
        l_new = alpha * l_ref[...]
        acc_new = alpha * acc_ref[...]
        for m, l, o in parts:
            wgt = jnp.exp2(m - m_new)
            l_new = l_new + wgt * l
            acc_new = acc_new + wgt * o
        m_ref[...] = m_new
        l_ref[...] = l_new
        acc_ref[...] = acc_new

    merge(partial_softmax([(k_pages[i][0], v_pages[i][0]) for i in range(pages_per_step)], False))

    @pl.when(p == pl.num_programs(1) - 1)
    def _():
        n_new = kn_ref.shape[1]
        kpad_ref[...] = jnp.zeros_like(kpad_ref)
        vpad_ref[...] = jnp.zeros_like(vpad_ref)
        kpad_ref[0:n_new, :] = kn_ref[0]
        vpad_ref[0:n_new, :] = vn_ref[0]
        merge(partial_softmax([(kpad_ref[...], vpad_ref[...])], True))

        o = (acc_ref[0:rows_per_map, :] / l_ref[0:rows_per_map, :]
             - _diff_lambda(lam_ref, lam_init) * (acc_ref[rows_per_map:, :] / l_ref[rows_per_map:, :]))
        _, _, _, half = _sample_row_fields(o.shape, tl, n_heads)
        ss = jnp.broadcast_to(jnp.sum(o * o, axis=-1, keepdims=True), o.shape)
        other = jnp.where(half == 0, pltpu.roll(ss, rows_per_map - n_heads, 0), pltpu.roll(ss, n_heads, 0))
        inv = lax.rsqrt((ss + other) / (2 * HEAD_DIM) + NORM_EPS)
        sw = jnp.where(half == 0, sw_ref[:, 0:HEAD_DIM], sw_ref[:, HEAD_DIM:])
        o_ref[0] = o * inv * sw * (1.0 - lam_init)


def _attn_sample(q_rows, k_new, v_new, cache_k, cache_v, page_table, lam_p, subln_w, *,
                 tl, n_heads, layer_slot, slot_pages, lam_init, pages_per_step):
    bsz, n_rows, dh = q_rows.shape
    assert 2 * n_heads == SUBLANES, "one (map, token) block of query rows must fill one vreg"
    n_new = k_new.shape[1]
    n_pages = page_table.shape[1]
    page_rows = cache_k.shape[1]
    base = layer_slot * slot_pages
    steps = n_pages // pages_per_step

    def page_spec(i):
        return pl.BlockSpec(
            (1, page_rows, dh),
            lambda b, p, pt: (base + pt[b * n_pages + p * pages_per_step + i], 0, 0))

    kern = functools.partial(_attn_sample_kernel, pages_per_step=pages_per_step, tl=tl,
                             n_heads=n_heads, lam_init=lam_init)
    grid_spec = pltpu.PrefetchScalarGridSpec(
        num_scalar_prefetch=1,
        grid=(bsz, steps),
        in_specs=[pl.BlockSpec(lam_p.shape, lambda b, p, pt: (0, 0)),
                  pl.BlockSpec((1, 2 * dh), lambda b, p, pt: (0, 0)),
                  pl.BlockSpec((1, n_rows, dh), lambda b, p, pt: (b, 0, 0)),
                  pl.BlockSpec((1, n_new, dh), lambda b, p, pt: (b, 0, 0)),
                  pl.BlockSpec((1, n_new, dh), lambda b, p, pt: (b, 0, 0))]
                 + [page_spec(i) for i in range(pages_per_step)] * 2,
        out_specs=pl.BlockSpec((1, n_rows // 2, dh), lambda b, p, pt: (b, 0, 0)),
        scratch_shapes=[pltpu.VMEM((n_rows, dh), F32), pltpu.VMEM((n_rows, dh), F32),
                        pltpu.VMEM((n_rows, dh), F32),
                        pltpu.VMEM((dh, dh), F32), pltpu.VMEM((dh, dh), F32)],
    )
    return pl.pallas_call(
        kern,
        grid_spec=grid_spec,
        out_shape=jax.ShapeDtypeStruct((bsz, n_rows // 2, dh), F32),
        compiler_params=_params("arbitrary", "arbitrary"),
        name="diff_attn_sample",
    )(page_table.reshape(-1), lam_p, subln_w.reshape(1, 2 * dh), q_rows, k_new, v_new,
      *([cache_k] * pages_per_step), *([cache_v] * pages_per_step))


def _sample_query_rows(q, n_heads):
    bsz, tl, _ = q.shape
    r = np.arange(4 * n_heads * tl)
    head = n_heads - 1 - r % n_heads
    token = (r // (2 * n_heads)) % tl
    qmap = r // (2 * n_heads * tl)
    return q.reshape(bsz, tl, n_heads, 2, HEAD_DIM)[:, token, head, qmap, :]


def _values_in_stored_order(v, n_heads):
    lead = v.shape[:-2]
    tokens = v.shape[-2]
    v = v.reshape(*lead, tokens, n_heads, 2, HEAD_DIM)
    return jnp.swapaxes(v, -2, -3).reshape(*lead, tokens * 2 * n_heads, HEAD_DIM)


def _rope_tables(pos0, tlen):
    inv = 1.0 / (ROPE_BASE ** jnp.linspace(0.0, 1.0, HEAD_DIM // 2, dtype=F32))
    ang = (pos0 + jnp.arange(tlen)).astype(F32)[:, None] * inv[None, :]
    sign = jnp.tile(jnp.asarray([-1.0, 1.0], F32), HEAD_DIM // 2)
    return jnp.repeat(jnp.cos(ang), 2, axis=1), jnp.repeat(jnp.sin(ang), 2, axis=1) * sign


def _trunk(x, mod, pos0, state_a, state_b, cache_k, cache_v, page_table, w, *, tm, prompt):
    bsz, tlen, d = x.shape
    depth = mod.shape[0]
    xf = x.reshape(bsz * tlen, d)
    hg_out, rt_out, k_out, v_out = [], [], [], []
    for l in range(depth):
        ml = mod[l]
        if l % 2 == 0:
            e = l // 2
            n_in = w["in_ab"].shape[2]
            (proj,) = _inproj(xf, ml, w["norm"][l, 0], w["in_ab"][e], [(n_in, F32, 1.0)],
                              tm=tm, shift_col=0, scale_col=1)
            proj = proj.reshape(bsz, tlen, n_in)
            cos, sin = _rope_tables(pos0, tlen)
            if prompt:
                o, s_a, s_b = _mixer_prompt(proj, cos, sin, w["lb_logits"], w["hgrn_norm"][e],
                                            layer_slot=e, chunk=128)
            else:
                o, s_a, s_b = _mixer_sample(proj, cos, sin, w["lb_logits"], w["hgrn_norm"][e],
                                            state_a[e], state_b[e], layer_slot=e)
            hg_out.append(s_a)
            rt_out.append(s_b)
            xf = _outproj(o.reshape(bsz * tlen, -1), xf, ml, w["out_ab"][e], tm=tm, gate_col=2)
        else:
            a = l // 2
            lam_init = 0.8 - 0.6 * math.exp(-0.3 * l)
            n_c = w["in_c"].shape[2] // 3
            n_heads = n_c // (2 * HEAD_DIM)
            q, k, v = _inproj(xf, ml, w["norm"][l, 0], w["in_c"][a],
                              [(n_c, BF16, HEAD_DIM ** -0.5 * LOG2_E), (n_c, F32, 1.0), (n_c, F32, 1.0)],
                              tm=tm, shift_col=0, scale_col=1)
            q, k, v = (t.reshape(bsz, tlen, n_c) for t in (q, k, v))
            if prompt:
                o = _attn_prompt(q, k, v, w["diff_lambda"][a], w["subln"][a], lam_init=lam_init,
                                 tq=min(512, tlen))
            else:
                n_slots, slot_pages, page_rows, _ = cache_k.shape
                o = _attn_sample(
                    _sample_query_rows(q, n_heads),
                    k.reshape(bsz, tlen * 2 * n_heads, HEAD_DIM), _values_in_stored_order(v, n_heads),
                    cache_k.reshape(n_slots * slot_pages, page_rows, HEAD_DIM),
                    cache_v.reshape(n_slots * slot_pages, page_rows, HEAD_DIM),
                    page_table, w["diff_lambda"][a], w["subln"][a], tl=tlen, n_heads=n_heads,
                    layer_slot=a, slot_pages=slot_pages, lam_init=lam_init, pages_per_step=8)
                o = o.reshape(bsz, tlen, 2, n_heads, HEAD_DIM)[:, :, :, ::-1, :]
                o = jnp.swapaxes(o, 2, 3)
            k_out.append(k)
            v_out.append(v)
            xf = _outproj(o.reshape(bsz * tlen, -1), xf, ml, w["out_c"][a], tm=tm, gate_col=2)
        xf = _mlp(xf, ml, w["norm"][l, 1], w["mlp_up"][l], w["mlp_down"][l], w["final_norm"],
                  tm=tm, final_norm=(l == depth - 1))
    return (xf.reshape(bsz, tlen, d), jnp.stack(hg_out), jnp.stack(rt_out),
            jnp.stack(k_out), jnp.stack(v_out))


def kernel(x_prompt, x_sample, state_hgrn, state_ret, cache_k, cache_v, page_table, c_prompt, c_sample,
           w_ada, b_ada, norm_w, w_in_ab, w_out_ab, hgrn_lb_logits, hgrn_norm_w, w_in_c, w_out_c,
           diff_lambda, diff_subln_w, w_mlp_up, w_mlp_down, final_norm_w):
    bp, tp, d = x_prompt.shape
    bs, ts, _ = x_sample.shape
    n_slots, n_phys, page, n_heads_c, _, dh = cache_k.shape

    mod = _ada(jnp.concatenate([c_prompt, c_sample], axis=0), w_ada, b_ada)
    depth = mod.shape[0]
    mod_p = mod[:, :bp].reshape(depth, bp, 1, 6 * d)
    mod_s = jnp.repeat(mod[:, bp:], ts, axis=1).reshape(depth, 1, bs * ts, 6 * d)

    w = dict(norm=norm_w, in_ab=w_in_ab.astype(BF16), out_ab=w_out_ab.astype(BF16),
             lb_logits=hgrn_lb_logits, hgrn_norm=hgrn_norm_w, in_c=w_in_c.astype(BF16),
             out_c=w_out_c.astype(BF16), diff_lambda=diff_lambda, subln=diff_subln_w,
             mlp_up=w_mlp_up.astype(BF16), mlp_down=w_mlp_down.astype(BF16), final_norm=final_norm_w)

    y_p, hg_p, rt_p, k_p, v_p = _trunk(x_prompt, mod_p, 0, None, None, None, None, None, w,
                                       tm=512, prompt=True)
    past_len = page_table.shape[1] * page
    page_rows = page * 2 * n_heads_c
    cache_k_rows = cache_k.reshape(n_slots, n_phys, page_rows, dh)
    cache_v_rows = _values_in_stored_order(cache_v.reshape(n_slots, n_phys, page, n_heads_c * 2 * dh),
                                           n_heads_c)
    y_s, hg_s, rt_s, k_s, v_s = _trunk(x_sample, mod_s, past_len, state_hgrn, state_ret,
                                       cache_k_rows, cache_v_rows, page_table, w,
                                       tm=bs * ts, prompt=False)

    n_odd = k_p.shape[0]
    k_p = k_p.reshape(n_odd, bp, tp // page, page, n_heads_c, 2, dh)
    v_p = v_p.reshape(n_odd, bp, tp // page, page, n_heads_c, 2 * dh)
    k_s = k_s.reshape(n_odd, bs, ts, n_heads_c, 2, dh)
    v_s = v_s.reshape(n_odd, bs, ts, n_heads_c, 2 * dh)
    return (y_p, y_s, hg_p, rt_p, k_p, v_p, hg_s, rt_s, k_s, v_s)
```

```python
import functools
import math

import jax
import jax.numpy as jnp
import numpy as np
from jax import lax
from jax.experimental import pallas as pl
from jax.experimental.pallas import tpu as pltpu

F32 = jnp.float32
BF16 = jnp.bfloat16

HEAD_DIM = 128
SUBLANES = 8
NORM_EPS = 1e-6
ROPE_BASE = 10000.0
V7X_VMEM_LIMIT = 52 * 1024 * 1024

_NT = (((1,), (1,)), ((), ()))
_TN = (((0,), (0,)), ((), ()))


def _params(*semantics):
    return pltpu.CompilerParams(dimension_semantics=semantics, vmem_limit_bytes=V7X_VMEM_LIMIT)


def _resident(shape):
    zeros = (0,) * len(shape)
    return pl.BlockSpec(shape, lambda *_: zeros, pipeline_mode=pl.Buffered(1))


def _sigmoid(x):
    return 0.5 + 0.5 * jnp.tanh(0.5 * x)


def _silu(x):
    half = 0.5 * x
    return half + half * jnp.tanh(half)


def _rms(x):
    return x * lax.rsqrt(jnp.mean(x * x, axis=-1, keepdims=True) + NORM_EPS)


def _mm(a, b):
    return jnp.dot(a, b, preferred_element_type=F32)


def _ada_kernel(c_ref, w_ref, b_ref, o_ref):
    s = _silu(c_ref[...]).astype(BF16)
    o_ref[0] = _mm(s, w_ref[0].astype(BF16)) + b_ref[0]


def _ada(c, w_ada, b_ada):
    depth, d, n = w_ada.shape
    r = c.shape[0]
    tn = n // 4
    return pl.pallas_call(
        _ada_kernel,
        grid=(depth, n // tn),
        in_specs=[pl.BlockSpec((r, d), lambda l, j: (0, 0)),
                  pl.BlockSpec((1, d, tn), lambda l, j: (l, 0, j)),
                  pl.BlockSpec((1, 1, tn), lambda l, j: (l, 0, j))],
        out_specs=pl.BlockSpec((1, r, tn), lambda l, j: (l, 0, j)),
        out_shape=jax.ShapeDtypeStruct((depth, r, n), F32),
        compiler_params=_params("arbitrary", "arbitrary"),
        name="ada_mod",
    )(c, w_ada, b_ada.reshape(depth, 1, n))


def _modulated_norm(x, nw_ref, mod_ref, shift_col, scale_col):
    d = x.shape[-1]
    shift = mod_ref[0, :, shift_col * d:(shift_col + 1) * d]
    scale = mod_ref[0, :, scale_col * d:(scale_col + 1) * d]
    return _rms(x) * nw_ref[...] * (1.0 + scale) + shift


def _inproj_kernel(x_ref, mod_ref, nw_ref, w_ref, *out_refs, shift_col, scale_col, out_scales, tc):
    h = _modulated_norm(x_ref[...], nw_ref, mod_ref, shift_col, scale_col).astype(BF16)
    off = 0
    for o_ref, out_scale in zip(out_refs, out_scales):
        width = o_ref.shape[-1]
        for c in range(0, width, tc):
            y = _mm(h, w_ref[:, off + c:off + c + tc])
            if out_scale != 1.0:
                y = y * out_scale
            o_ref[:, c:c + tc] = y.astype(o_ref.dtype)
        off += width


def _inproj(x, mod, nw, w, outs, *, tm, shift_col, scale_col):
    m, d = x.shape
    n = w.shape[1]
    rows_per_mod = m // mod.shape[0]
    tiles_per_mod = rows_per_mod // tm
    kern = functools.partial(_inproj_kernel, shift_col=shift_col, scale_col=scale_col,
                             out_scales=tuple(o[2] for o in outs), tc=512)
    return pl.pallas_call(
        kern,
        grid=(m // tm,),
        in_specs=[pl.BlockSpec((tm, d), lambda i: (i, 0)),
                  pl.BlockSpec((1, mod.shape[1], mod.shape[2]), lambda i: (i // tiles_per_mod, 0, 0)),
                  _resident((1, d)),
                  _resident((d, n))],
        out_specs=[pl.BlockSpec((tm, o[0]), lambda i: (i, 0)) for o in outs],
        out_shape=[jax.ShapeDtypeStruct((m, o[0]), o[1]) for o in outs],
        compiler_params=_params("parallel"),
        name="norm_mod_inproj",
    )(x, mod, nw.reshape(1, d), w)


def _qkv_kernel(x_ref, mod_ref, nw_ref, w_ref, q_ref, kb_ref, vb_ref, kr_ref, vr_ref, *,
                q_scale, n_heads, tc):
    h = _modulated_norm(x_ref[...], nw_ref, mod_ref, 0, 1).astype(BF16)
    tm, n_c = q_ref.shape
    lanes = HEAD_DIM
    group = 2 * n_heads
    for c in range(0, n_c, tc):
        q_ref[:, c:c + tc] = (_mm(h, w_ref[:, c:c + tc]) * q_scale).astype(q_ref.dtype)
    for c in range(0, n_c, tc):
        y = _mm(h, w_ref[:, n_c + c:n_c + c + tc])
        kb_ref[:, c:c + tc] = y.astype(kb_ref.dtype)
        for i in range(tc // lanes):
            kr_ref[pl.ds(c // lanes + i, tm, stride=group), :] = y[:, i * lanes:(i + 1) * lanes]
    for c in range(0, n_c, tc):
        y = _mm(h, w_ref[:, 2 * n_c + c:2 * n_c + c + tc])
        vb_ref[:, c:c + tc] = y.astype(vb_ref.dtype)
        for i in range(tc // lanes):
            head, half = divmod(c // lanes + i, 2)
            vr_ref[pl.ds(half * n_heads + head, tm, stride=group), :] = y[:, i * lanes:(i + 1) * lanes]


def _qkv(x, mod, nw, w, *, tm, q_scale, n_heads):
    m, d = x.shape
    n_c = w.shape[1] // 3
    group = 2 * n_heads
    tiles_per_mod = (m // mod.shape[0]) // tm
    kern = functools.partial(_qkv_kernel, q_scale=q_scale, n_heads=n_heads, tc=512)
    cols = pl.BlockSpec((tm, n_c), lambda i: (i, 0))
    rows = pl.BlockSpec((tm * group, HEAD_DIM), lambda i: (i, 0))
    return pl.pallas_call(
        kern,
        grid=(m // tm,),
        in_specs=[pl.BlockSpec((tm, d), lambda i: (i, 0)),
                  pl.BlockSpec((1, mod.shape[1], mod.shape[2]), lambda i: (i // tiles_per_mod, 0, 0)),
                  _resident((1, d)),
                  _resident((d, 3 * n_c))],
        out_specs=[cols, cols, cols, rows, rows],
        out_shape=[jax.ShapeDtypeStruct((m, n_c), BF16)] * 3
                  + [jax.ShapeDtypeStruct((m * group, HEAD_DIM), F32)] * 2,
        compiler_params=_params("parallel"),
        name="norm_mod_qkv",
    )(x, mod, nw.reshape(1, d), w)


def _mlp_kernel(o_ref, x_ref, mod_ref, nw_ref, wo_ref, wu_ref, wd_ref, fw_ref, out_ref, u_ref, *,
                tf, final_norm):
    d = x_ref.shape[-1]

    def mod_col(col):
        return mod_ref[0, :, col * d:(col + 1) * d]

    x = x_ref[...] + mod_col(2) * _mm(o_ref[...].astype(BF16), wo_ref[...])
    h = _modulated_norm(x, nw_ref, mod_ref, 3, 4).astype(BF16)
    for c in range(0, wu_ref.shape[1], tf):
        u = jnp.maximum(_mm(h, wu_ref[:, c:c + tf]), 0.0)
        u_ref[:, c:c + tf] = (u * u).astype(BF16)
    y = x + mod_col(5) * _mm(u_ref[...], wd_ref[...])
    if final_norm:
        y = _rms(y) * fw_ref[...]
    out_ref[...] = y


def _mlp(o, x, mod, nw, w_out, w_up, w_down, final_w, *, tm, final_norm):
    m, d = x.shape
    k = o.shape[1]
    dff = w_up.shape[1]
    tiles_per_mod = (m // mod.shape[0]) // tm
    kern = functools.partial(_mlp_kernel, tf=512, final_norm=final_norm)
    return pl.pallas_call(
        kern,
        grid=(m // tm,),
        in_specs=[pl.BlockSpec((tm, k), lambda i: (i, 0)),
                  pl.BlockSpec((tm, d), lambda i: (i, 0)),
                  pl.BlockSpec((1, mod.shape[1], mod.shape[2]), lambda i: (i // tiles_per_mod, 0, 0)),
                  _resident((1, d)),
                  _resident((k, d)),
                  _resident((d, dff)),
                  _resident((dff, d)),
                  _resident((1, d))],
        out_specs=pl.BlockSpec((tm, d), lambda i: (i, 0)),
        out_shape=jax.ShapeDtypeStruct((m, d), F32),
        scratch_shapes=[pltpu.VMEM((tm, dff), BF16)],
        compiler_params=_params("parallel"),
        name="outproj_mlp_residual",
    )(o, x, mod, nw.reshape(1, d), w_out, w_up, w_down, final_w.reshape(1, d))


def _hgrn_lower_bound(logits_ref, layer_slot):
    lg = logits_ref[...]
    ex = jnp.exp(lg - jnp.max(lg, axis=0, keepdims=True))
    pr = ex / jnp.sum(ex, axis=0, keepdims=True)
    return jnp.sum(pr[:layer_slot + 1], axis=0, keepdims=True)


def _rotate_pairs(x, cos, sin_signed):
    n = x.shape[-1]
    lane = lax.broadcasted_iota(jnp.int32, x.shape, 1)
    swapped = jnp.where((lane & 1) == 0, pltpu.roll(x, n - 1, 1), pltpu.roll(x, 1, 1))
    return x * cos + swapped * sin_signed


def _cumsum_rows(x):
    n = x.shape[0]
    row = lax.broadcasted_iota(jnp.int32, x.shape, 0)
    d = 1
    while d < n:
        x = x + jnp.where(row >= d, pltpu.roll(x, d, 0), 0.0)
        d *= 2
    return x


def _split_row(g, b):
    n, lanes = g.shape
    if 2 * b >= SUBLANES:
        g3 = g.reshape(n // (2 * b), 2 * b, lanes)
        return jnp.broadcast_to(g3[:, b - 1:b, :], g3.shape).reshape(n, lanes)
    rmod = lax.broadcasted_iota(jnp.int32, g.shape, 0) & (2 * b - 1)
    out = g
    for r in range(2 * b):
        if r != b - 1:
            out = jnp.where(rmod == r, pltpu.roll(g, (r - (b - 1)) % n, 0), out)
    return out


def _level_matrix(n):
    t = np.arange(n)[:, None]
    s = np.arange(n)[None, :]
    hi = np.floor(np.log2(np.maximum(t ^ s, 1))).astype(np.int32)
    return jnp.asarray(np.where(s < t, hi, -1).astype(np.int32))


def _hgrn_chunk(q, k, v, g, lv, st):
    n = q.shape[0]
    row = lax.broadcasted_iota(jnp.int32, q.shape, 0)
    ri = lax.broadcasted_iota(jnp.int32, (n, n), 0)
    ci = lax.broadcasted_iota(jnp.int32, (n, n), 1)
    a = jnp.where(ri == ci, jnp.sum(q * k, axis=-1, keepdims=True), 0.0)
    b, level = 1, 0
    while b < n:
        gb = _split_row(g, b)
        is_query = (row & b) != 0
        e = jnp.exp(jnp.where(is_query, g - gb, gb - g))
        z = (jnp.where(is_query, q, k) * e).astype(BF16)
        a = jnp.where(lv == level, lax.dot_general(z, z, _NT, preferred_element_type=F32), a)
        b, level = 2 * b, level + 1
    carried = lax.dot_general((q * jnp.exp(g)).astype(BF16), st.astype(BF16), _NT,
                              preferred_element_type=F32)
    o = _mm(a.astype(BF16), v.astype(BF16)) + carried
    g_end = g[n - 1:n, :]
    k_dec = (k * jnp.exp(g_end - g)).astype(BF16)
    st_new = st * jnp.exp(g_end) + lax.dot_general(v.astype(BF16), k_dec, _TN,
                                                   preferred_element_type=F32)
    return o, st_new


def _retention_chunk(q, k, v, log_gamma, st):
    n = q.shape[0]
    rowf = lax.broadcasted_iota(jnp.int32, q.shape, 0).astype(F32)
    ri = lax.broadcasted_iota(jnp.int32, (n, n), 0)
    ci = lax.broadcasted_iota(jnp.int32, (n, n), 1)
    decay = jnp.where(ci <= ri, jnp.exp((ri - ci).astype(F32) * log_gamma), 0.0)
    a = lax.dot_general(q.astype(BF16), k.astype(BF16), _NT, preferred_element_type=F32) * decay
    q_dec = (q * jnp.exp((rowf + 1.0) * log_gamma)).astype(BF16)
    o = _mm(a.astype(BF16), v.astype(BF16)) + _mm(q_dec, st.astype(BF16))
    k_dec = (k * jnp.exp((n - 1.0 - rowf) * log_gamma)).astype(BF16)
    st_new = st * math.exp(n * log_gamma) + lax.dot_general(k_dec, v.astype(BF16), _TN,
                                                            preferred_element_type=F32)
    return o, st_new


def _retention_log_gamma(h):
    return math.log(1.0 - 2.0 ** (-5.0 - h))


def _mixer_prompt_kernel(proj_ref, cos_ref, sin_ref, lv_ref, lbl_ref, na_ref,
                         o_ref, sa_out, sb_out, sat_ref, sb_ref, *, layer_slot, n_heads):
    t = pl.program_id(1)
    hd = HEAD_DIM
    w = n_heads * hd

    @pl.when(t == 0)
    def _():
        sat_ref[...] = jnp.zeros_like(sat_ref)
        sb_ref[...] = jnp.zeros_like(sb_ref)

    def cols(group, h):
        return proj_ref[0, :, group * w + h * hd:group * w + (h + 1) * hd]

    lb = _hgrn_lower_bound(lbl_ref, layer_slot)
    fz = lb + (1.0 - lb) * _sigmoid(proj_ref[0, :, w:2 * w])
    g_all = _cumsum_rows(jnp.log(fz))
    k_all = 1.0 - fz
    lv = lv_ref[...]
    for h in range(n_heads):
        sl = slice(h * hd, (h + 1) * hd)
        q = _silu(cols(0, h)) * hd ** -0.5
        o, st = _hgrn_chunk(q, k_all[:, sl], cols(2, h), g_all[:, sl], lv, sat_ref[h])
        sat_ref[h] = st
        o_ref[0, :, sl] = (_rms(o) * na_ref[...] * _silu(cols(3, h))).astype(o_ref.dtype)

    cos = cos_ref[...]
    sin = sin_ref[...]
    for h in range(n_heads):
        q = _rotate_pairs(cols(4, h), cos, sin)
        k = _rotate_pairs(cols(5, h), cos, sin) * hd ** -0.5
        o, st = _retention_chunk(q, k, cols(6, h), _retention_log_gamma(h), sb_ref[h])
        sb_ref[h] = st
        o_ref[0, :, w + h * hd:w + (h + 1) * hd] = (_rms(o) * _silu(cols(7, h))).astype(o_ref.dtype)

    @pl.when(t == pl.num_programs(1) - 1)
    def _():
        for h in range(n_heads):
            sa_out[0, h] = sat_ref[h].T
            sb_out[0, h] = sb_ref[h]


def _mixer_prompt(proj, cos, sin, lb_logits, norm_a, *, layer_slot, chunk):
    bsz, tlen, n = proj.shape
    hd = HEAD_DIM
    n_heads = n // (8 * hd)
    kern = functools.partial(_mixer_prompt_kernel, layer_slot=layer_slot, n_heads=n_heads)
    state = jax.ShapeDtypeStruct((bsz, n_heads, hd, hd), F32)
    state_spec = pl.BlockSpec((1, n_heads, hd, hd), lambda b, t: (b, 0, 0, 0))
    return pl.pallas_call(
        kern,
        grid=(bsz, tlen // chunk),
        in_specs=[pl.BlockSpec((1, chunk, n), lambda b, t: (b, t, 0)),
                  pl.BlockSpec((chunk, hd), lambda b, t: (t, 0)),
                  pl.BlockSpec((chunk, hd), lambda b, t: (t, 0)),
                  pl.BlockSpec((chunk, chunk), lambda b, t: (0, 0)),
                  pl.BlockSpec(lb_logits.shape, lambda b, t: (0, 0)),
                  pl.BlockSpec((1, hd), lambda b, t: (0, 0))],
        out_specs=[pl.BlockSpec((1, chunk, 2 * n_heads * hd), lambda b, t: (b, t, 0)),
                   state_spec, state_spec],
        out_shape=[jax.ShapeDtypeStruct((bsz, tlen, 2 * n_heads * hd), BF16), state, state],
        scratch_shapes=[pltpu.VMEM((n_heads, hd, hd), F32), pltpu.VMEM((n_heads, hd, hd), F32)],
        compiler_params=_params("arbitrary", "arbitrary"),
        name="mixer_prompt",
    )(proj, cos, sin, _level_matrix(chunk), lb_logits, norm_a.reshape(1, hd))


def _pair_chunk(q, k, v, g, s0, dec, kpad_ref, vpad_ref):
    tl = q.shape[0]
    row = lax.broadcasted_iota(jnp.int32, q.shape, 0)
    o = _mm((q * jnp.exp(g)).astype(BF16), s0.astype(BF16))
    for s in range(tl):
        live = row >= s
        w = jnp.exp(jnp.where(live, g - g[s:s + 1, :], 0.0))
        a = jnp.sum(jnp.where(live, q * w * k[s:s + 1, :], 0.0), axis=-1, keepdims=True)
        o = o + a * v[s:s + 1, :]
    kpad_ref[0:tl, :] = k * jnp.exp(g[tl - 1:tl, :] - g)
    vpad_ref[0:tl, :] = v
    upd = lax.dot_general(kpad_ref[...].astype(BF16), vpad_ref[...].astype(BF16), _TN,
                          preferred_element_type=F32)
    return o, s0 * dec + upd


def _mixer_sample_kernel(proj_ref, cos_ref, sin_ref, lbl_ref, na_ref, sa_ref, sb_ref,
                         o_ref, sa_out, sb_out, kpad_ref, vpad_ref, *, layer_slot, n_heads):
    hd = HEAD_DIM
    w = n_heads * hd
    tl = proj_ref.shape[1]
    kpad_ref[...] = jnp.zeros_like(kpad_ref)
    vpad_ref[...] = jnp.zeros_like(vpad_ref)

    def cols(group, h):
        return proj_ref[0, :, group * w + h * hd:group * w + (h + 1) * hd]

    lb = _hgrn_lower_bound(lbl_ref, layer_slot)
    fz = lb + (1.0 - lb) * _sigmoid(proj_ref[0, :, w:2 * w])
    lf = jnp.log(fz)
    row = lax.broadcasted_iota(jnp.int32, lf.shape, 0)
    g_all = jnp.zeros_like(lf)
    for j in range(tl):
        g_all = g_all + jnp.where(row >= j, lf[j:j + 1, :], 0.0)
    k_all = 1.0 - fz
    for h in range(n_heads):
        sl = slice(h * hd, (h + 1) * hd)
        g = g_all[:, sl]
        q = _silu(cols(0, h)) * hd ** -0.5
        dec = jnp.broadcast_to(jnp.exp(g[tl - 1:tl, :]), (hd, hd)).T
        o, st = _pair_chunk(q, k_all[:, sl], cols(2, h), g, sa_ref[0, h], dec, kpad_ref, vpad_ref)
        sa_out[0, h] = st
        o_ref[0, :, sl] = _rms(o) * na_ref[...] * _silu(cols(3, h))

    cos = cos_ref[...]
    sin = sin_ref[...]
    rowf = lax.broadcasted_iota(jnp.int32, (tl, hd), 0).astype(F32)
    for h in range(n_heads):
        lg = _retention_log_gamma(h)
        q = _rotate_pairs(cols(4, h), cos, sin)
        k = _rotate_pairs(cols(5, h), cos, sin) * hd ** -0.5
        o, st = _pair_chunk(q, k, cols(6, h), (rowf + 1.0) * lg, sb_ref[0, h], math.exp(tl * lg),
                            kpad_ref, vpad_ref)
        sb_out[0, h] = st
        o_ref[0, :, w + h * hd:w + (h + 1) * hd] = _rms(o) * _silu(cols(7, h))


def _mixer_sample(proj, cos, sin, lb_logits, norm_a, state_a, state_b, *, layer_slot):
    bsz, tl, n = proj.shape
    hd = HEAD_DIM
    n_heads = n // (8 * hd)
    kern = functools.partial(_mixer_sample_kernel, layer_slot=layer_slot, n_heads=n_heads)
    state = jax.ShapeDtypeStruct((bsz, n_heads, hd, hd), F32)
    state_spec = pl.BlockSpec((1, n_heads, hd, hd), lambda b: (b, 0, 0, 0))
    return pl.pallas_call(
        kern,
        grid=(bsz,),
        in_specs=[pl.BlockSpec((1, tl, n), lambda b: (b, 0, 0)),
                  pl.BlockSpec((tl, hd), lambda b: (0, 0)),
                  pl.BlockSpec((tl, hd), lambda b: (0, 0)),
                  pl.BlockSpec(lb_logits.shape, lambda b: (0, 0)),
                  pl.BlockSpec((1, hd), lambda b: (0, 0)),
                  state_spec, state_spec],
        out_specs=[pl.BlockSpec((1, tl, 2 * n_heads * hd), lambda b: (b, 0, 0)), state_spec, state_spec],
        out_shape=[jax.ShapeDtypeStruct((bsz, tl, 2 * n_heads * hd), F32), state, state],
        scratch_shapes=[pltpu.VMEM((hd, hd), F32), pltpu.VMEM((hd, hd), F32)],
        compiler_params=_params("arbitrary"),
        name="mixer_sample",
    )(proj, cos, sin, lb_logits, norm_a.reshape(1, hd), state_a, state_b)


def _diff_lambda(lam_ref, lam_init):
    lp = lam_ref[...]
    return (jnp.exp(jnp.sum(lp[0:1] * lp[1:2], axis=-1, keepdims=True))
            - jnp.exp(jnp.sum(lp[2:3] * lp[3:4], axis=-1, keepdims=True)) + lam_init)


LOG2_E = math.log2(math.e)


def _attn_prompt_kernel(lam_ref, sw_ref, q_ref, k_ref, v_ref, o_ref, m_ref, l_ref, acc_ref, *,
                        tq, lam_init):
    qi = pl.program_id(2)
    dh = HEAD_DIM
    lanes = m_ref.shape[-1]
    m_ref[...] = jnp.full_like(m_ref, -jnp.inf)
    l_ref[...] = jnp.zeros_like(l_ref)
    acc_ref[...] = jnp.zeros_like(acc_ref)

    def kv_tile(kj, diagonal):
        start = pl.multiple_of(kj * tq, tq)
        v = v_ref[0, pl.ds(start, tq), :].astype(BF16)
        scores = []
        for j in range(2):
            k = k_ref[0, pl.ds(start, tq), j * dh:(j + 1) * dh].astype(BF16)
            s = lax.dot_general(q_ref[0, :, j * dh:(j + 1) * dh], k, _NT, preferred_element_type=F32)
            if diagonal:
                ri = lax.broadcasted_iota(jnp.int32, s.shape, 0)
                ci = lax.broadcasted_iota(jnp.int32, s.shape, 1)
                s = jnp.where(ci <= ri, s, -jnp.inf)
            scores.append(s)
        probs, alphas = [], []
        for j in range(2):
            s = scores[j]
            m_old = m_ref[j]
            m_new = jnp.maximum(m_old, jnp.max(s, axis=-1, keepdims=True))
            alpha = jnp.exp2(m_old - m_new)
            l_new = alpha * l_ref[j]
            tiles = []
            for c in range(0, s.shape[1], lanes):
                p = jnp.exp2(s[:, c:c + lanes] - m_new)
                l_new = l_new + p
                tiles.append(p.astype(BF16))
            l_ref[j] = l_new
            m_ref[j] = m_new
            probs.append(jnp.concatenate(tiles, axis=1))
            alphas.append(alpha)
        for j in range(2):
            pv = _mm(probs[j], v)
            for c in range(0, pv.shape[1], lanes):
                acc_ref[j, :, c:c + lanes] = alphas[j] * acc_ref[j, :, c:c + lanes] + pv[:, c:c + lanes]

    def body(kj, carry):
        kv_tile(kj, False)
        return carry

    lax.fori_loop(0, qi, body, 0)
    kv_tile(qi, True)

    l0 = jnp.sum(l_ref[0], axis=-1, keepdims=True)
    l1 = jnp.sum(l_ref[1], axis=-1, keepdims=True)
    o = acc_ref[0] / l0 - _diff_lambda(lam_ref, lam_init) * (acc_ref[1] / l1)
    o_ref[0] = (_rms(o) * sw_ref[...] * (1.0 - lam_init)).astype(o_ref.dtype)


def _attn_prompt(q, k, v, lam_p, subln_w, *, lam_init, tq):
    bsz, tlen, n = q.shape
    e = 2 * HEAD_DIM
    n_heads = n // e
    kern = functools.partial(_attn_prompt_kernel, tq=tq, lam_init=lam_init)
    return pl.pallas_call(
        kern,
        grid=(bsz, n_heads, tlen // tq),
        in_specs=[pl.BlockSpec(lam_p.shape, lambda b, h, i: (0, 0)),
                  pl.BlockSpec((1, e), lambda b, h, i: (0, 0)),
                  pl.BlockSpec((1, tq, e), lambda b, h, i: (b, i, h)),
                  pl.BlockSpec((1, tlen, e), lambda b, h, i: (b, 0, h)),
                  pl.BlockSpec((1, tlen, e), lambda b, h, i: (b, 0, h))],
        out_specs=pl.BlockSpec((1, tq, e), lambda b, h, i: (b, i, h)),
        out_shape=jax.ShapeDtypeStruct((bsz, tlen, n), BF16),
        scratch_shapes=[pltpu.VMEM((2, tq, HEAD_DIM), F32), pltpu.VMEM((2, tq, HEAD_DIM), F32),
                        pltpu.VMEM((2, tq, e), F32)],
        compiler_params=_params("parallel", "parallel", "arbitrary"),
        name="diff_attn_prompt",
    )(lam_p, subln_w.reshape(1, e), q, k, v)


def _sample_row_fields(shape, tl, n_heads):
    r = lax.broadcasted_iota(jnp.int32, shape, 0)
    head = n_heads - 1 - r % n_heads
    half = (r // n_heads) % 2
    token = (r // (2 * n_heads)) % tl
    qmap = r // (2 * n_heads * tl)
    return head, qmap, token, half


def _attn_sample_kernel(pt_ref, lam_ref, sw_ref, q_ref, kn_ref, vn_ref, *rest,
                        pages_per_step, tl, n_heads, lam_init):
    del pt_ref
    k_pages = rest[:pages_per_step]
    v_pages = rest[pages_per_step:2 * pages_per_step]
    o_ref, m_ref, l_ref, acc_ref, kpad_ref, vpad_ref = rest[2 * pages_per_step:]
    p = pl.program_id(1)
    group = 2 * n_heads
    n_rows = q_ref.shape[1]
    rows_per_map = n_rows // 2

    @pl.when(p == 0)
    def _():
        m_ref[...] = jnp.full_like(m_ref, -jnp.inf)
        l_ref[...] = jnp.zeros_like(l_ref)
        acc_ref[...] = jnp.zeros_like(acc_ref)

    q = q_ref[0]

    def own_lanes(n_keys, causal):
        head, qmap, token, _ = _sample_row_fields((n_rows, n_keys), tl, n_heads)
        key = lax.broadcasted_iota(jnp.int32, (n_rows, n_keys), 1)
        own = (key % group) == 2 * head + qmap
        return jnp.logical_and(own, key // group <= token) if causal else own

    def partial_softmax(pages, causal):
        own = own_lanes(pages[0][0].shape[0], causal)
        scores = [lax.dot_general(q, k_rows.astype(BF16), _NT, preferred_element_type=F32)
                  for k_rows, _ in pages]
        stats, probs = [], []
        for s in scores:
            s = jnp.where(own, s, -jnp.inf)
            m = jnp.max(s, axis=-1, keepdims=True)
            pexp = jnp.exp2(s - m)
            stats.append((m, jnp.sum(pexp, axis=-1, keepdims=True)))
            aligned = []
            for blk in range(n_rows // SUBLANES):
                shift = group - (n_heads - 1) - (blk * SUBLANES) // rows_per_map
                aligned.append(pltpu.roll(pexp[blk * SUBLANES:(blk + 1) * SUBLANES, :], shift, 1,
                                          stride=1, stride_axis=0))
            probs.append(jnp.concatenate(aligned, axis=0).astype(BF16))
        return [(m, l, _mm(p_rows, pltpu.roll(v_rows, group, 0).astype(BF16)))
                for (m, l), p_rows, (_, v_rows) in zip(stats, probs, pages)]

    def merge(parts):
        m_run = m_ref[...]
        m_new = m_run
        for m, _, _ in parts:
            m_new = jnp.maximum(m_new, m)
        alpha = jnp.exp2(m_run - m_new)
        l_new = alpha * l_ref[...]
        acc_new = alpha * acc_ref[...]
        for m, l, o in parts:
            wgt = jnp.exp2(m - m_new)
            l_new = l_new + wgt * l
            acc_new = acc_new + wgt * o
        m_ref[...] = m_new
        l_ref[...] = l_new
        acc_ref[...] = acc_new

    merge(partial_softmax([(k_pages[i][0], v_pages[i][0]) for i in range(pages_per_step)], False))

    @pl.when(p == pl.num_programs(1) - 1)
    def _():
        n_new = kn_ref.shape[1]
        kpad_ref[...] = jnp.zeros_like(kpad_ref)
        vpad_ref[...] = jnp.zeros_like(vpad_ref)
        kpad_ref[0:n_new, :] = kn_ref[0]
        vpad_ref[0:n_new, :] = vn_ref[0]
        merge(partial_softmax([(kpad_ref[...], vpad_ref[...])], True))

        o = (acc_ref[0:rows_per_map, :] / l_ref[0:rows_per_map, :]
             - _diff_lambda(lam_ref, lam_init) * (acc_ref[rows_per_map:, :] / l_ref[rows_per_map:, :]))
        _, _, _, half = _sample_row_fields(o.shape, tl, n_heads)
        ss = jnp.broadcast_to(jnp.sum(o * o, axis=-1, keepdims=True), o.shape)
        other = jnp.where(half == 0, pltpu.roll(ss, rows_per_map - n_heads, 0), pltpu.roll(ss, n_heads, 0))
        inv = lax.rsqrt((ss + other) / (2 * HEAD_DIM) + NORM_EPS)
        sw = jnp.where(half == 0, sw_ref[:, 0:HEAD_DIM], sw_ref[:, HEAD_DIM:])
        o_ref[0] = o * inv * sw * (1.0 - lam_init)


def _attn_sample(q_rows, k_new, v_new, cache_k, cache_v, page_table, lam_p, subln_w, *,
                 tl, n_heads, layer_slot, slot_pages, lam_init, pages_per_step):
    bsz, n_rows, dh = q_rows.shape
    assert 2 * n_heads == SUBLANES, "one (map, token) block of query rows must fill one vreg"
    n_new = k_new.shape[1]
    n_pages = page_table.shape[1]
    page_rows = cache_k.shape[1]
    base = layer_slot * slot_pages
    steps = n_pages // pages_per_step

    def page_spec(i):
        return pl.BlockSpec(
            (1, page_rows, dh),
            lambda b, p, pt: (base + pt[b * n_pages + p * pages_per_step + i], 0, 0))

    kern = functools.partial(_attn_sample_kernel, pages_per_step=pages_per_step, tl=tl,
                             n_heads=n_heads, lam_init=lam_init)
    grid_spec = pltpu.PrefetchScalarGridSpec(
        num_scalar_prefetch=1,
        grid=(bsz, steps),
        in_specs=[pl.BlockSpec(lam_p.shape, lambda b, p, pt: (0, 0)),
                  pl.BlockSpec((1, 2 * dh), lambda b, p, pt: (0, 0)),
                  pl.BlockSpec((1, n_rows, dh), lambda b, p, pt: (b, 0, 0)),
                  pl.BlockSpec((1, n_new, dh), lambda b, p, pt: (b, 0, 0)),
                  pl.BlockSpec((1, n_new, dh), lambda b, p, pt: (b, 0, 0))]
                 + [page_spec(i) for i in range(pages_per_step)] * 2,
        out_specs=pl.BlockSpec((1, n_rows // 2, dh), lambda b, p, pt: (b, 0, 0)),
        scratch_shapes=[pltpu.VMEM((n_rows, dh), F32), pltpu.VMEM((n_rows, dh), F32),
                        pltpu.VMEM((n_rows, dh), F32),
                        pltpu.VMEM((dh, dh), F32), pltpu.VMEM((dh, dh), F32)],
    )
    return pl.pallas_call(
        kern,
        grid_spec=grid_spec,
        out_shape=jax.ShapeDtypeStruct((bsz, n_rows // 2, dh), F32),
        compiler_params=_params("arbitrary", "arbitrary"),
        name="diff_attn_sample",
    )(page_table.reshape(-1), lam_p, subln_w.reshape(1, 2 * dh), q_rows, k_new, v_new,
      *([cache_k] * pages_per_step), *([cache_v] * pages_per_step))


def _sample_query_rows(q, n_heads):
    bsz, tl, _ = q.shape
    r = np.arange(4 * n_heads * tl)
    head = n_heads - 1 - r % n_heads
    token = (r // (2 * n_heads)) % tl
    qmap = r // (2 * n_heads * tl)
    return q.reshape(bsz, tl, n_heads, 2, HEAD_DIM)[:, token, head, qmap, :]


def _values_in_stored_order(v, n_heads):
    lead = v.shape[:-2]
    tokens = v.shape[-2]
    v = v.reshape(*lead, tokens, n_heads, 2, HEAD_DIM)
    return jnp.swapaxes(v, -2, -3).reshape(*lead, tokens * 2 * n_heads, HEAD_DIM)


def _rope_tables(pos0, tlen):
    inv = 1.0 / (ROPE_BASE ** jnp.linspace(0.0, 1.0, HEAD_DIM // 2, dtype=F32))
    ang = (pos0 + jnp.arange(tlen)).astype(F32)[:, None] * inv[None, :]
    sign = jnp.tile(jnp.asarray([-1.0, 1.0], F32), HEAD_DIM // 2)
    return jnp.repeat(jnp.cos(ang), 2, axis=1), jnp.repeat(jnp.sin(ang), 2, axis=1) * sign


def _trunk(x, mod, pos0, state_a, state_b, cache_k, cache_v, page_table, w, *, tm, prompt):
    bsz, tlen, d = x.shape
    depth = mod.shape[0]
    xf = x.reshape(bsz * tlen, d)
    hg_out, rt_out, k_out, v_out = [], [], [], []
    for l in range(depth):
        ml = mod[l]
        if l % 2 == 0:
            e = l // 2
            n_in = w["in_ab"].shape[2]
            (proj,) = _inproj(xf, ml, w["norm"][l, 0], w["in_ab"][e], [(n_in, F32, 1.0)],
                              tm=tm, shift_col=0, scale_col=1)
            proj = proj.reshape(bsz, tlen, n_in)
            cos, sin = _rope_tables(pos0, tlen)
            if prompt:
                o, s_a, s_b = _mixer_prompt(proj, cos, sin, w["lb_logits"], w["hgrn_norm"][e],
                                            layer_slot=e, chunk=128)
            else:
                o, s_a, s_b = _mixer_sample(proj, cos, sin, w["lb_logits"], w["hgrn_norm"][e],
                                            state_a[e], state_b[e], layer_slot=e)
            hg_out.append(s_a)
            rt_out.append(s_b)
            w_out = w["out_ab"][e]
        else:
            a = l // 2
            lam_init = 0.8 - 0.6 * math.exp(-0.3 * l)
            n_c = w["in_c"].shape[2] // 3
            n_heads = n_c // (2 * HEAD_DIM)
            group = 2 * n_heads
            q, k, v, k_rows, v_rows = _qkv(xf, ml, w["norm"][l, 0], w["in_c"][a], tm=tm,
                                           q_scale=HEAD_DIM ** -0.5 * LOG2_E, n_heads=n_heads)
            if prompt:
                q, k, v = (t.reshape(bsz, tlen, n_c) for t in (q, k, v))
                o = _attn_prompt(q, k, v, w["diff_lambda"][a], w["subln"][a], lam_init=lam_init,
                                 tq=min(512, tlen))
            else:
                n_slots, slot_pages, page_rows, _ = cache_k.shape
                o = _attn_sample(
                    _sample_query_rows(q.reshape(bsz, tlen, n_c), n_heads),
                    k_rows.reshape(bsz, tlen * group, HEAD_DIM), v_rows.reshape(bsz, tlen * group, HEAD_DIM),
                    cache_k.reshape(n_slots * slot_pages, page_rows, HEAD_DIM),
                    cache_v.reshape(n_slots * slot_pages, page_rows, HEAD_DIM),
                    page_table, w["diff_lambda"][a], w["subln"][a], tl=tlen, n_heads=n_heads,
                    layer_slot=a, slot_pages=slot_pages, lam_init=lam_init, pages_per_step=8)
                o = o.reshape(bsz, tlen, 2, n_heads, HEAD_DIM)[:, :, :, ::-1, :]
                o = jnp.swapaxes(o, 2, 3)
            k_out.append(k_rows)
            v_out.append(v_rows)
            w_out = w["out_c"][a]
        xf = _mlp(o.reshape(bsz * tlen, -1), xf, ml, w["norm"][l, 1], w_out, w["mlp_up"][l],
                  w["mlp_down"][l], w["final_norm"], tm=tm, final_norm=(l == depth - 1))
    return (xf.reshape(bsz, tlen, d), jnp.stack(hg_out), jnp.stack(rt_out),
            jnp.stack(k_out), jnp.stack(v_out))


def kernel(x_prompt, x_sample, state_hgrn, state_ret, cache_k, cache_v, page_table, c_prompt, c_sample,
           w_ada, b_ada, norm_w, w_in_ab, w_out_ab, hgrn_lb_logits, hgrn_norm_w, w_in_c, w_out_c,
           diff_lambda, diff_subln_w, w_mlp_up, w_mlp_down, final_norm_w):
    bp, tp, d = x_prompt.shape
    bs, ts, _ = x_sample.shape
    n_slots, n_phys, page, n_heads_c, _, dh = cache_k.shape

    mod = _ada(jnp.concatenate([c_prompt, c_sample], axis=0), w_ada, b_ada)
    depth = mod.shape[0]
    mod_p = mod[:, :bp].reshape(depth, bp, 1, 6 * d)
    mod_s = jnp.repeat(mod[:, bp:], ts, axis=1).reshape(depth, 1, bs * ts, 6 * d)

    w = dict(norm=norm_w, in_ab=w_in_ab.astype(BF16), out_ab=w_out_ab.astype(BF16),
             lb_logits=hgrn_lb_logits, hgrn_norm=hgrn_norm_w, in_c=w_in_c.astype(BF16),
             out_c=w_out_c.astype(BF16), diff_lambda=diff_lambda, subln=diff_subln_w,
             mlp_up=w_mlp_up.astype(BF16), mlp_down=w_mlp_down.astype(BF16), final_norm=final_norm_w)

    y_p, hg_p, rt_p, k_p, v_p = _trunk(x_prompt, mod_p, 0, None, None, None, None, None, w,
                                       tm=512, prompt=True)
    past_len = page_table.shape[1] * page
    page_rows = page * 2 * n_heads_c
    cache_k_rows = cache_k.reshape(n_slots, n_phys, page_rows, dh)
    cache_v_rows = _values_in_stored_order(cache_v.reshape(n_slots, n_phys, page, n_heads_c * 2 * dh),
                                           n_heads_c)
    y_s, hg_s, rt_s, k_s, v_s = _trunk(x_sample, mod_s, past_len, state_hgrn, state_ret,
                                       cache_k_rows, cache_v_rows, page_table, w,
                                       tm=bs * ts, prompt=False)

    n_odd = k_p.shape[0]

    def values_from_stored_order(v, lead):
        v = jnp.swapaxes(v.reshape(n_odd, *lead, 2, n_heads_c, dh), -2, -3)
        return v.reshape(n_odd, *lead, n_heads_c, 2 * dh)

    k_p = k_p.reshape(n_odd, bp, tp // page, page, n_heads_c, 2, dh)
    v_p = values_from_stored_order(v_p, (bp, tp // page, page))
    k_s = k_s.reshape(n_odd, bs, ts, n_heads_c, 2, dh)
    v_s = values_from_stored_order(v_s, (bs, ts))
    return (y_p, y_s, hg_p, rt_p, k_p, v_p, hg_s, rt_s, k_s, v_s)
```

```python
import functools
import math

import jax
import jax.numpy as jnp
import numpy as np
from jax import lax
from jax.experimental import pallas as pl
from jax.experimental.pallas import tpu as pltpu

F32 = jnp.float32
BF16 = jnp.bfloat16

HEAD_DIM = 128
SUBLANES = 8
NORM_EPS = 1e-6
ROPE_BASE = 10000.0
V7X_VMEM_LIMIT = 52 * 1024 * 1024

_NT = (((1,), (1,)), ((), ()))
_TN = (((0,), (0,)), ((), ()))


def _params(*semantics):
    return pltpu.CompilerParams(dimension_semantics=semantics, vmem_limit_bytes=V7X_VMEM_LIMIT)


def _resident(shape):
    zeros = (0,) * len(shape)
    return pl.BlockSpec(shape, lambda *_: zeros, pipeline_mode=pl.Buffered(1))


def _sigmoid(x):
    return 0.5 + 0.5 * jnp.tanh(0.5 * x)


def _silu(x):
    half = 0.5 * x
    return half + half * jnp.tanh(half)


def _rms(x):
    return x * lax.rsqrt(jnp.mean(x * x, axis=-1, keepdims=True) + NORM_EPS)


def _mm(a, b):
    return jnp.dot(a, b, preferred_element_type=F32)


def _ada_kernel(c_ref, w_ref, b_ref, o_ref):
    s = _silu(c_ref[...]).astype(BF16)
    o_ref[0] = _mm(s, w_ref[0].astype(BF16)) + b_ref[0]


def _ada(c, w_ada, b_ada):
    depth, d, n = w_ada.shape
    r = c.shape[0]
    tn = n // 4
    return pl.pallas_call(
        _ada_kernel,
        grid=(depth, n // tn),
        in_specs=[pl.BlockSpec((r, d), lambda l, j: (0, 0)),
                  pl.BlockSpec((1, d, tn), lambda l, j: (l, 0, j)),
                  pl.BlockSpec((1, 1, tn), lambda l, j: (l, 0, j))],
        out_specs=pl.BlockSpec((1, r, tn), lambda l, j: (l, 0, j)),
        out_shape=jax.ShapeDtypeStruct((depth, r, n), F32),
        compiler_params=_params("arbitrary", "arbitrary"),
        name="ada_mod",
    )(c, w_ada, b_ada.reshape(depth, 1, n))


def _modulated_norm(x, nw_ref, mod_ref, shift_col, scale_col):
    d = x.shape[-1]
    shift = mod_ref[0, :, shift_col * d:(shift_col + 1) * d]
    scale = mod_ref[0, :, scale_col * d:(scale_col + 1) * d]
    return _rms(x) * nw_ref[...] * (1.0 + scale) + shift


def _inproj_kernel(x_ref, mod_ref, nw_ref, w_ref, *out_refs, shift_col, scale_col, out_scales, tc):
    h = _modulated_norm(x_ref[...], nw_ref, mod_ref, shift_col, scale_col).astype(BF16)
    off = 0
    for o_ref, out_scale in zip(out_refs, out_scales):
        width = o_ref.shape[-1]
        for c in range(0, width, tc):
            y = _mm(h, w_ref[:, off + c:off + c + tc])
            if out_scale != 1.0:
                y = y * out_scale
            o_ref[:, c:c + tc] = y.astype(o_ref.dtype)
        off += width


def _inproj(x, mod, nw, w, outs, *, tm, shift_col, scale_col):
    m, d = x.shape
    n = w.shape[1]
    rows_per_mod = m // mod.shape[0]
    tiles_per_mod = rows_per_mod // tm
    kern = functools.partial(_inproj_kernel, shift_col=shift_col, scale_col=scale_col,
                             out_scales=tuple(o[2] for o in outs), tc=512)
    return pl.pallas_call(
        kern,
        grid=(m // tm,),
        in_specs=[pl.BlockSpec((tm, d), lambda i: (i, 0)),
                  pl.BlockSpec((1, mod.shape[1], mod.shape[2]), lambda i: (i // tiles_per_mod, 0, 0)),
                  _resident((1, d)),
                  _resident((d, n))],
        out_specs=[pl.BlockSpec((tm, o[0]), lambda i: (i, 0)) for o in outs],
        out_shape=[jax.ShapeDtypeStruct((m, o[0]), o[1]) for o in outs],
        compiler_params=_params("parallel"),
        name="norm_mod_inproj",
    )(x, mod, nw.reshape(1, d), w)


def _qkv_kernel(x_ref, mod_ref, nw_ref, w_ref, q_ref, kb_ref, vb_ref, kr_ref, vr_ref, *,
                q_scale, n_heads, tc):
    h = _modulated_norm(x_ref[...], nw_ref, mod_ref, 0, 1).astype(BF16)
    tm, n_c = q_ref.shape
    lanes = HEAD_DIM
    group = 2 * n_heads
    for c in range(0, n_c, tc):
        q_ref[:, c:c + tc] = (_mm(h, w_ref[:, c:c + tc]) * q_scale).astype(q_ref.dtype)
    for c in range(0, n_c, tc):
        y = _mm(h, w_ref[:, n_c + c:n_c + c + tc])
        kb_ref[:, c:c + tc] = y.astype(kb_ref.dtype)
        for i in range(tc // lanes):
            kr_ref[pl.ds(c // lanes + i, tm, stride=group), :] = y[:, i * lanes:(i + 1) * lanes]
    for c in range(0, n_c, tc):
        y = _mm(h, w_ref[:, 2 * n_c + c:2 * n_c + c + tc])
        vb_ref[:, c:c + tc] = y.astype(vb_ref.dtype)
        for i in range(tc // lanes):
            head, half = divmod(c // lanes + i, 2)
            vr_ref[pl.ds(half * n_heads + head, tm, stride=group), :] = y[:, i * lanes:(i + 1) * lanes]


def _qkv(x, mod, nw, w, *, tm, q_scale, n_heads):
    m, d = x.shape
    n_c = w.shape[1] // 3
    group = 2 * n_heads
    tiles_per_mod = (m // mod.shape[0]) // tm
    kern = functools.partial(_qkv_kernel, q_scale=q_scale, n_heads=n_heads, tc=512)
    cols = pl.BlockSpec((tm, n_c), lambda i: (i, 0))
    rows = pl.BlockSpec((tm * group, HEAD_DIM), lambda i: (i, 0))
    return pl.pallas_call(
        kern,
        grid=(m // tm,),
        in_specs=[pl.BlockSpec((tm, d), lambda i: (i, 0)),
                  pl.BlockSpec((1, mod.shape[1], mod.shape[2]), lambda i: (i // tiles_per_mod, 0, 0)),
                  _resident((1, d)),
                  _resident((d, 3 * n_c))],
        out_specs=[cols, cols, cols, rows, rows],
        out_shape=[jax.ShapeDtypeStruct((m, n_c), BF16)] * 3
                  + [jax.ShapeDtypeStruct((m * group, HEAD_DIM), F32)] * 2,
        compiler_params=_params("parallel"),
        name="norm_mod_qkv",
    )(x, mod, nw.reshape(1, d), w)


def _mlp_kernel(o_ref, x_ref, mod_ref, nw_ref, wo_ref, wu_ref, wd_ref, fw_ref, out_ref, u_ref, *,
                tf, final_norm):
    d = x_ref.shape[-1]

    def mod_col(col):
        return mod_ref[0, :, col * d:(col + 1) * d]

    x = x_ref[...] + mod_col(2) * _mm(o_ref[...].astype(BF16), wo_ref[...])
    h = _modulated_norm(x, nw_ref, mod_ref, 3, 4).astype(BF16)
    for c in range(0, wu_ref.shape[1], tf):
        u = jnp.maximum(_mm(h, wu_ref[:, c:c + tf]), 0.0)
        u_ref[:, c:c + tf] = (u * u).astype(BF16)
    y = x + mod_col(5) * _mm(u_ref[...], wd_ref[...])
    if final_norm:
        y = _rms(y) * fw_ref[...]
    out_ref[...] = y


def _mlp(o, x, mod, nw, w_out, w_up, w_down, final_w, *, tm, final_norm):
    m, d = x.shape
    k = o.shape[1]
    dff = w_up.shape[1]
    tiles_per_mod = (m // mod.shape[0]) // tm
    kern = functools.partial(_mlp_kernel, tf=512, final_norm=final_norm)
    return pl.pallas_call(
        kern,
        grid=(m // tm,),
        in_specs=[pl.BlockSpec((tm, k), lambda i: (i, 0)),
                  pl.BlockSpec((tm, d), lambda i: (i, 0)),
                  pl.BlockSpec((1, mod.shape[1], mod.shape[2]), lambda i: (i // tiles_per_mod, 0, 0)),
                  _resident((1, d)),
                  _resident((k, d)),
                  _resident((d, dff)),
                  _resident((dff, d)),
                  _resident((1, d))],
        out_specs=pl.BlockSpec((tm, d), lambda i: (i, 0)),
        out_shape=jax.ShapeDtypeStruct((m, d), F32),
        scratch_shapes=[pltpu.VMEM((tm, dff), BF16)],
        compiler_params=_params("parallel"),
        name="outproj_mlp_residual",
    )(o, x, mod, nw.reshape(1, d), w_out, w_up, w_down, final_w.reshape(1, d))


def _hgrn_lower_bound(logits_ref, layer_slot):
    lg = logits_ref[...]
    ex = jnp.exp(lg - jnp.max(lg, axis=0, keepdims=True))
    pr = ex / jnp.sum(ex, axis=0, keepdims=True)
    return jnp.sum(pr[:layer_slot + 1], axis=0, keepdims=True)


def _rotate_pairs(x, cos, sin_signed):
    n = x.shape[-1]
    lane = lax.broadcasted_iota(jnp.int32, x.shape, 1)
    swapped = jnp.where((lane & 1) == 0, pltpu.roll(x, n - 1, 1), pltpu.roll(x, 1, 1))
    return x * cos + swapped * sin_signed


def _cumsum_rows(x):
    n = x.shape[0]
    row = lax.broadcasted_iota(jnp.int32, x.shape, 0)
    d = 1
    while d < n:
        x = x + jnp.where(row >= d, pltpu.roll(x, d, 0), 0.0)
        d *= 2
    return x


def _split_row(g, b):
    n, lanes = g.shape
    if 2 * b >= SUBLANES:
        g3 = g.reshape(n // (2 * b), 2 * b, lanes)
        return jnp.broadcast_to(g3[:, b - 1:b, :], g3.shape).reshape(n, lanes)
    rmod = lax.broadcasted_iota(jnp.int32, g.shape, 0) & (2 * b - 1)
    out = g
    for r in range(2 * b):
        if r != b - 1:
            out = jnp.where(rmod == r, pltpu.roll(g, (r - (b - 1)) % n, 0), out)
    return out


def _level_matrix(n):
    t = np.arange(n)[:, None]
    s = np.arange(n)[None, :]
    hi = np.floor(np.log2(np.maximum(t ^ s, 1))).astype(np.int32)
    return jnp.asarray(np.where(s < t, hi, -1).astype(np.int32))


def _hgrn_chunk(q, k, v, g2, lv, st):
    n = q.shape[0]
    row = lax.broadcasted_iota(jnp.int32, q.shape, 0)
    ri = lax.broadcasted_iota(jnp.int32, (n, n), 0)
    ci = lax.broadcasted_iota(jnp.int32, (n, n), 1)
    a = jnp.where(ri == ci, jnp.sum(q * k, axis=-1, keepdims=True), 0.0)
    b, level = 1, 0
    while b < n:
        if b >= SUBLANES:
            parts = []
            for lo in range(0, n, 2 * b):
                edge = g2[lo + b - 1:lo + b, :]
                parts.append(k[lo:lo + b] * jnp.exp2(edge - g2[lo:lo + b]))
                parts.append(q[lo + b:lo + 2 * b] * jnp.exp2(g2[lo + b:lo + 2 * b] - edge))
            z = jnp.concatenate(parts, axis=0).astype(BF16)
        else:
            is_query = (row & b) != 0
            sign = jnp.where(is_query, 1.0, -1.0)
            z = (jnp.where(is_query, q, k) * jnp.exp2((g2 - _split_row(g2, b)) * sign)).astype(BF16)
        a = jnp.where(lv == level, lax.dot_general(z, z, _NT, preferred_element_type=F32), a)
        b, level = 2 * b, level + 1
    carried = lax.dot_general((q * jnp.exp2(g2)).astype(BF16), st.astype(BF16), _NT,
                              preferred_element_type=F32)
    o = _mm(a.astype(BF16), v.astype(BF16)) + carried
    g_end = g2[n - 1:n, :]
    k_dec = (k * jnp.exp2(g_end - g2)).astype(BF16)
    st_new = st * jnp.exp2(g_end) + lax.dot_general(v.astype(BF16), k_dec, _TN,
                                                    preferred_element_type=F32)
    return o, st_new


def _retention_chunk(q, k, v, log_gamma, st):
    n = q.shape[0]
    rowf = lax.broadcasted_iota(jnp.int32, q.shape, 0).astype(F32)
    ri = lax.broadcasted_iota(jnp.int32, (n, n), 0)
    ci = lax.broadcasted_iota(jnp.int32, (n, n), 1)
    decay = jnp.where(ci <= ri, jnp.exp((ri - ci).astype(F32) * log_gamma), 0.0)
    a = lax.dot_general(q.astype(BF16), k.astype(BF16), _NT, preferred_element_type=F32) * decay
    q_dec = (q * jnp.exp((rowf + 1.0) * log_gamma)).astype(BF16)
    o = _mm(a.astype(BF16), v.astype(BF16)) + _mm(q_dec, st.astype(BF16))
    k_dec = (k * jnp.exp((n - 1.0 - rowf) * log_gamma)).astype(BF16)
    st_new = st * math.exp(n * log_gamma) + lax.dot_general(k_dec, v.astype(BF16), _TN,
                                                            preferred_element_type=F32)
    return o, st_new


def _retention_log_gamma(h):
    return math.log(1.0 - 2.0 ** (-5.0 - h))


def _mixer_prompt_kernel(proj_ref, cos_ref, sin_ref, lv_ref, lbl_ref, na_ref,
                         o_ref, sa_out, sb_out, sat_ref, sb_ref, *, layer_slot, n_heads):
    t = pl.program_id(1)
    hd = HEAD_DIM
    w = n_heads * hd

    @pl.when(t == 0)
    def _():
        sat_ref[...] = jnp.zeros_like(sat_ref)
        sb_ref[...] = jnp.zeros_like(sb_ref)

    lb = _hgrn_lower_bound(lbl_ref, layer_slot)
    lv = lv_ref[...]
    cos = cos_ref[...]
    sin = sin_ref[...]
    for r in range(proj_ref.shape[0]):
        def cols(group, h):
            return proj_ref[r, :, group * w + h * hd:group * w + (h + 1) * hd]

        fz = lb + (1.0 - lb) * _sigmoid(proj_ref[r, :, w:2 * w])
        g_all = _cumsum_rows(jnp.log2(fz))
        k_all = 1.0 - fz
        for h in range(n_heads):
            sl = slice(h * hd, (h + 1) * hd)
            q = _silu(cols(0, h)) * hd ** -0.5
            o, st = _hgrn_chunk(q, k_all[:, sl], cols(2, h), g_all[:, sl], lv, sat_ref[r, h])
            sat_ref[r, h] = st
            o_ref[r, :, sl] = (_rms(o) * na_ref[...] * _silu(cols(3, h))).astype(o_ref.dtype)
        for h in range(n_heads):
            q = _rotate_pairs(cols(4, h), cos, sin)
            k = _rotate_pairs(cols(5, h), cos, sin) * hd ** -0.5
            o, st = _retention_chunk(q, k, cols(6, h), _retention_log_gamma(h), sb_ref[r, h])
            sb_ref[r, h] = st
            o_ref[r, :, w + h * hd:w + (h + 1) * hd] = (_rms(o) * _silu(cols(7, h))).astype(o_ref.dtype)

    @pl.when(t == pl.num_programs(1) - 1)
    def _():
        for r in range(proj_ref.shape[0]):
            for h in range(n_heads):
                sa_out[r, h] = sat_ref[r, h].T
                sb_out[r, h] = sb_ref[r, h]


def _mixer_prompt(proj, cos, sin, lb_logits, norm_a, *, layer_slot, chunk, rows_per_step):
    bsz, tlen, n = proj.shape
    hd = HEAD_DIM
    n_heads = n // (8 * hd)
    rb = rows_per_step
    kern = functools.partial(_mixer_prompt_kernel, layer_slot=layer_slot, n_heads=n_heads)
    state = jax.ShapeDtypeStruct((bsz, n_heads, hd, hd), F32)
    state_spec = pl.BlockSpec((rb, n_heads, hd, hd), lambda b, t: (b, 0, 0, 0))
    return pl.pallas_call(
        kern,
        grid=(bsz // rb, tlen // chunk),
        in_specs=[pl.BlockSpec((rb, chunk, n), lambda b, t: (b, t, 0)),
                  pl.BlockSpec((chunk, hd), lambda b, t: (t, 0)),
                  pl.BlockSpec((chunk, hd), lambda b, t: (t, 0)),
                  pl.BlockSpec((chunk, chunk), lambda b, t: (0, 0)),
                  pl.BlockSpec(lb_logits.shape, lambda b, t: (0, 0)),
                  pl.BlockSpec((1, hd), lambda b, t: (0, 0))],
        out_specs=[pl.BlockSpec((rb, chunk, 2 * n_heads * hd), lambda b, t: (b, t, 0)),
                   state_spec, state_spec],
        out_shape=[jax.ShapeDtypeStruct((bsz, tlen, 2 * n_heads * hd), BF16), state, state],
        scratch_shapes=[pltpu.VMEM((rb, n_heads, hd, hd), F32), pltpu.VMEM((rb, n_heads, hd, hd), F32)],
        compiler_params=_params("arbitrary", "arbitrary"),
        name="mixer_prompt",
    )(proj, cos, sin, _level_matrix(chunk), lb_logits, norm_a.reshape(1, hd))


def _pair_chunk(q, k, v, g, s0, dec, kpad_ref, vpad_ref):
    tl = q.shape[0]
    row = lax.broadcasted_iota(jnp.int32, q.shape, 0)
    o = _mm((q * jnp.exp(g)).astype(BF16), s0.astype(BF16))
    for s in range(tl):
        live = row >= s
        w = jnp.exp(jnp.where(live, g - g[s:s + 1, :], 0.0))
        a = jnp.sum(jnp.where(live, q * w * k[s:s + 1, :], 0.0), axis=-1, keepdims=True)
        o = o + a * v[s:s + 1, :]
    kpad_ref[0:tl, :] = k * jnp.exp(g[tl - 1:tl, :] - g)
    vpad_ref[0:tl, :] = v
    upd = lax.dot_general(kpad_ref[...].astype(BF16), vpad_ref[...].astype(BF16), _TN,
                          preferred_element_type=F32)
    return o, s0 * dec + upd


def _mixer_sample_kernel(proj_ref, cos_ref, sin_ref, lbl_ref, na_ref, sa_ref, sb_ref,
                         o_ref, sa_out, sb_out, kpad_ref, vpad_ref, *, layer_slot, n_heads):
    hd = HEAD_DIM
    w = n_heads * hd
    tl = proj_ref.shape[1]
    kpad_ref[...] = jnp.zeros_like(kpad_ref)
    vpad_ref[...] = jnp.zeros_like(vpad_ref)

    def cols(group, h):
        return proj_ref[0, :, group * w + h * hd:group * w + (h + 1) * hd]

    lb = _hgrn_lower_bound(lbl_ref, layer_slot)
    fz = lb + (1.0 - lb) * _sigmoid(proj_ref[0, :, w:2 * w])
    lf = jnp.log(fz)
    row = lax.broadcasted_iota(jnp.int32, lf.shape, 0)
    g_all = jnp.zeros_like(lf)
    for j in range(tl):
        g_all = g_all + jnp.where(row >= j, lf[j:j + 1, :], 0.0)
    k_all = 1.0 - fz
    for h in range(n_heads):
        sl = slice(h * hd, (h + 1) * hd)
        g = g_all[:, sl]
        q = _silu(cols(0, h)) * hd ** -0.5
        dec = jnp.broadcast_to(jnp.exp(g[tl - 1:tl, :]), (hd, hd)).T
        o, st = _pair_chunk(q, k_all[:, sl], cols(2, h), g, sa_ref[0, h], dec, kpad_ref, vpad_ref)
        sa_out[0, h] = st
        o_ref[0, :, sl] = _rms(o) * na_ref[...] * _silu(cols(3, h))

    cos = cos_ref[...]
    sin = sin_ref[...]
    rowf = lax.broadcasted_iota(jnp.int32, (tl, hd), 0).astype(F32)
    for h in range(n_heads):
        lg = _retention_log_gamma(h)
        q = _rotate_pairs(cols(4, h), cos, sin)
        k = _rotate_pairs(cols(5, h), cos, sin) * hd ** -0.5
        o, st = _pair_chunk(q, k, cols(6, h), (rowf + 1.0) * lg, sb_ref[0, h], math.exp(tl * lg),
                            kpad_ref, vpad_ref)
        sb_out[0, h] = st
        o_ref[0, :, w + h * hd:w + (h + 1) * hd] = _rms(o) * _silu(cols(7, h))


def _mixer_sample(proj, cos, sin, lb_logits, norm_a, state_a, state_b, *, layer_slot):
    bsz, tl, n = proj.shape
    hd = HEAD_DIM
    n_heads = n // (8 * hd)
    kern = functools.partial(_mixer_sample_kernel, layer_slot=layer_slot, n_heads=n_heads)
    state = jax.ShapeDtypeStruct((bsz, n_heads, hd, hd), F32)
    state_spec = pl.BlockSpec((1, n_heads, hd, hd), lambda b: (b, 0, 0, 0))
    return pl.pallas_call(
        kern,
        grid=(bsz,),
        in_specs=[pl.BlockSpec((1, tl, n), lambda b: (b, 0, 0)),
                  pl.BlockSpec((tl, hd), lambda b: (0, 0)),
                  pl.BlockSpec((tl, hd), lambda b: (0, 0)),
                  pl.BlockSpec(lb_logits.shape, lambda b: (0, 0)),
                  pl.BlockSpec((1, hd), lambda b: (0, 0)),
                  state_spec, state_spec],
        out_specs=[pl.BlockSpec((1, tl, 2 * n_heads * hd), lambda b: (b, 0, 0)), state_spec, state_spec],
        out_shape=[jax.ShapeDtypeStruct((bsz, tl, 2 * n_heads * hd), F32), state, state],
        scratch_shapes=[pltpu.VMEM((hd, hd), F32), pltpu.VMEM((hd, hd), F32)],
        compiler_params=_params("arbitrary"),
        name="mixer_sample",
    )(proj, cos, sin, lb_logits, norm_a.reshape(1, hd), state_a, state_b)


def _diff_lambda(lam_ref, lam_init):
    lp = lam_ref[...]
    return (jnp.exp(jnp.sum(lp[0:1] * lp[1:2], axis=-1, keepdims=True))
            - jnp.exp(jnp.sum(lp[2:3] * lp[3:4], axis=-1, keepdims=True)) + lam_init)


LOG2_E = math.log2(math.e)


def _attn_prompt_kernel(lam_ref, sw_ref, q_ref, k_ref, v_ref, o_ref, m_ref, l_ref, acc_ref, *,
                        tq, lam_init):
    dh = HEAD_DIM
    lanes = m_ref.shape[-1]
    n_tiles = q_ref.shape[1] // tq
    m_ref[...] = jnp.full_like(m_ref, -jnp.inf)
    l_ref[...] = jnp.zeros_like(l_ref)
    acc_ref[...] = jnp.zeros_like(acc_ref)

    def tile_pair(q0, nq, k0, nk):
        rq = slice(q0, q0 + nq)
        rk = slice(k0, k0 + nk)
        v = v_ref[0, rk, :]
        scores = []
        for j in range(2):
            s = lax.dot_general(q_ref[0, rq, j * dh:(j + 1) * dh], k_ref[0, rk, j * dh:(j + 1) * dh],
                                _NT, preferred_element_type=F32)
            if k0 + nk - 1 > q0:
                ri = lax.broadcasted_iota(jnp.int32, s.shape, 0)
                ci = lax.broadcasted_iota(jnp.int32, s.shape, 1)
                s = jnp.where(ci - ri <= q0 - k0, s, -jnp.inf)
            scores.append(s)
        probs, alphas = [], []
        for j in range(2):
            s = scores[j]
            m_old = m_ref[j, rq]
            m_new = jnp.maximum(m_old, jnp.max(s, axis=-1, keepdims=True))
            alpha = jnp.exp2(m_old - m_new)
            l_new = alpha * l_ref[j, rq]
            tiles = []
            for c in range(0, s.shape[1], lanes):
                p = jnp.exp2(s[:, c:c + lanes] - m_new)
                l_new = l_new + p
                tiles.append(p.astype(BF16))
            l_ref[j, rq] = l_new
            m_ref[j, rq] = m_new
            probs.append(jnp.concatenate(tiles, axis=1))
            alphas.append(alpha)
        for j in range(2):
            pv = _mm(probs[j], v)
            for c in range(0, pv.shape[1], lanes):
                acc_ref[j, rq, c:c + lanes] = (alphas[j] * acc_ref[j, rq, c:c + lanes]
                                               + pv[:, c:c + lanes])

    for kj in range(n_tiles):
        for qi in range(kj, n_tiles):
            tile_pair(qi * tq, tq, kj * tq, tq)

    lam = _diff_lambda(lam_ref, lam_init)
    for qi in range(n_tiles):
        rq = slice(qi * tq, (qi + 1) * tq)
        l0 = jnp.sum(l_ref[0, rq], axis=-1, keepdims=True)
        l1 = jnp.sum(l_ref[1, rq], axis=-1, keepdims=True)
        o = acc_ref[0, rq] / l0 - lam * (acc_ref[1, rq] / l1)
        o_ref[0, rq] = (_rms(o) * sw_ref[...] * (1.0 - lam_init)).astype(o_ref.dtype)


def _attn_prompt(q, k, v, lam_p, subln_w, *, lam_init, tq):
    bsz, tlen, n = q.shape
    e = 2 * HEAD_DIM
    n_heads = n // e
    kern = functools.partial(_attn_prompt_kernel, tq=tq, lam_init=lam_init)
    head_cols = pl.BlockSpec((1, tlen, e), lambda b, h: (b, 0, h))
    return pl.pallas_call(
        kern,
        grid=(bsz, n_heads),
        in_specs=[pl.BlockSpec(lam_p.shape, lambda b, h: (0, 0)),
                  pl.BlockSpec((1, e), lambda b, h: (0, 0)),
                  head_cols, head_cols, head_cols],
        out_specs=head_cols,
        out_shape=jax.ShapeDtypeStruct((bsz, tlen, n), BF16),
        scratch_shapes=[pltpu.VMEM((2, tlen, HEAD_DIM), F32), pltpu.VMEM((2, tlen, HEAD_DIM), F32),
                        pltpu.VMEM((2, tlen, e), F32)],
        compiler_params=_params("parallel", "parallel"),
        name="diff_attn_prompt",
    )(lam_p, subln_w.reshape(1, e), q, k, v)


def _sample_row_fields(shape, tl, n_heads):
    r = lax.broadcasted_iota(jnp.int32, shape, 0)
    head = n_heads - 1 - r % n_heads
    half = (r // n_heads) % 2
    token = (r // (2 * n_heads)) % tl
    qmap = r // (2 * n_heads * tl)
    return head, qmap, token, half


PAGE_RING_DEPTH = 3


def _attn_sample_kernel(pt_ref, lam_ref, sw_ref, q_ref, kn_ref, vn_ref, ck_hbm, cv_hbm,
                        o_ref, m_ref, l_ref, acc_ref, kpad_ref, vpad_ref, kbuf, vbuf, sems, *,
                        pages_per_step, page_base, tl, n_heads, lam_init):
    b = pl.program_id(0)
    p = pl.program_id(1)
    steps = pl.num_programs(1)
    step = b * steps + p
    n_steps = pl.num_programs(0) * steps
    group = 2 * n_heads
    n_rows = q_ref.shape[1]
    rows_per_map = n_rows // 2

    def page_copies(of_step, slot):
        out = []
        for i in range(pages_per_step):
            page = page_base + pt_ref[of_step * pages_per_step + i]
            out.append(pltpu.make_async_copy(ck_hbm.at[page], kbuf.at[slot, i], sems.at[0, slot]))
            out.append(pltpu.make_async_copy(cv_hbm.at[page], vbuf.at[slot, i], sems.at[1, slot]))
        return out

    def start_step(of_step):
        for cp in page_copies(of_step, of_step % PAGE_RING_DEPTH):
            cp.start()

    @pl.when(step == 0)
    def _():
        for s in range(PAGE_RING_DEPTH - 1):
            @pl.when(s < n_steps)
            def _():
                start_step(s)

    @pl.when(step + (PAGE_RING_DEPTH - 1) < n_steps)
    def _():
        start_step(step + (PAGE_RING_DEPTH - 1))

    slot = step % PAGE_RING_DEPTH
    for cp in page_copies(step, slot):
        cp.wait()

    @pl.when(p == 0)
    def _():
        m_ref[...] = jnp.full_like(m_ref, -jnp.inf)
        l_ref[...] = jnp.zeros_like(l_ref)
        acc_ref[...] = jnp.zeros_like(acc_ref)

    q = q_ref[0]

    def own_lanes(n_keys, causal):
        head, qmap, token, _ = _sample_row_fields((n_rows, n_keys), tl, n_heads)
        key = lax.broadcasted_iota(jnp.int32, (n_rows, n_keys), 1)
        own = (key % group) == 2 * head + qmap
        return jnp.logical_and(own, key // group <= token) if causal else own

    def partial_softmax(pages, causal):
        own = own_lanes(pages[0][0].shape[0], causal)
        scores = [lax.dot_general(q, k_rows.astype(BF16), _NT, preferred_element_type=F32)
                  for k_rows, _ in pages]
        stats, probs = [], []
        for s in scores:
            s = jnp.where(own, s, -jnp.inf)
            m = jnp.max(s, axis=-1, keepdims=True)
            pexp = jnp.exp2(s - m)
            stats.append((m, jnp.sum(pexp, axis=-1, keepdims=True)))
            aligned = []
            for blk in range(n_rows // SUBLANES):
                shift = group - (n_heads - 1) - (blk * SUBLANES) // rows_per_map
                aligned.append(pltpu.roll(pexp[blk * SUBLANES:(blk + 1) * SUBLANES, :], shift, 1,
                                          stride=1, stride_axis=0))
            probs.append(jnp.concatenate(aligned, axis=0).astype(BF16))
        return [(m, l, _mm(p_rows, pltpu.roll(v_rows, group, 0).astype(BF16)))
                for (m, l), p_rows, (_, v_rows) in zip(stats, probs, pages)]

    def merge(parts):
        m_run = m_ref[...]
        m_new = m_run
        for m, _, _ in parts:
            m_new = jnp.maximum(m_new, m)
        alpha = jnp.exp2(m_run - m_new)
        l_new = alpha * l_ref[...]
        acc_new = alpha * acc_ref[...]
        for m, l, o in parts:
            wgt = jnp.exp2(m - m_new)
            l_new = l_new + wgt * l
            acc_new = acc_new + wgt * o
        m_ref[...] = m_new
        l_ref[...] = l_new
        acc_ref[...] = acc_new

    merge(partial_softmax([(kbuf[slot, i], vbuf[slot, i]) for i in range(pages_per_step)], False))

    @pl.when(p == pl.num_programs(1) - 1)
    def _():
        n_new = kn_ref.shape[1]
        kpad_ref[...] = jnp.zeros_like(kpad_ref)
        vpad_ref[...] = jnp.zeros_like(vpad_ref)
        kpad_ref[0:n_new, :] = kn_ref[0]
        vpad_ref[0:n_new, :] = vn_ref[0]
        merge(partial_softmax([(kpad_ref[...], vpad_ref[...])], True))

        o = (acc_ref[0:rows_per_map, :] / l_ref[0:rows_per_map, :]
             - _diff_lambda(lam_ref, lam_init) * (acc_ref[rows_per_map:, :] / l_ref[rows_per_map:, :]))
        _, _, _, half = _sample_row_fields(o.shape, tl, n_heads)
        ss = jnp.broadcast_to(jnp.sum(o * o, axis=-1, keepdims=True), o.shape)
        other = jnp.where(half == 0, pltpu.roll(ss, rows_per_map - n_heads, 0), pltpu.roll(ss, n_heads, 0))
        inv = lax.rsqrt((ss + other) / (2 * HEAD_DIM) + NORM_EPS)
        sw = jnp.where(half == 0, sw_ref[:, 0:HEAD_DIM], sw_ref[:, HEAD_DIM:])
        o_ref[0] = o * inv * sw * (1.0 - lam_init)


def _attn_sample(q_rows, k_new, v_new, cache_k, cache_v, page_table, lam_p, subln_w, *,
                 tl, n_heads, layer_slot, slot_pages, lam_init, pages_per_step):
    bsz, n_rows, dh = q_rows.shape
    assert 2 * n_heads == SUBLANES, "one (map, token) block of query rows must fill one vreg"
    n_new = k_new.shape[1]
    n_pages = page_table.shape[1]
    page_rows = cache_k.shape[1]
    steps = n_pages // pages_per_step
    assert steps * pages_per_step == n_pages

    kern = functools.partial(_attn_sample_kernel, pages_per_step=pages_per_step,
                             page_base=layer_slot * slot_pages, tl=tl, n_heads=n_heads,
                             lam_init=lam_init)
    ring = (PAGE_RING_DEPTH, pages_per_step, page_rows, dh)
    grid_spec = pltpu.PrefetchScalarGridSpec(
        num_scalar_prefetch=1,
        grid=(bsz, steps),
        in_specs=[pl.BlockSpec(lam_p.shape, lambda b, p, pt: (0, 0)),
                  pl.BlockSpec((1, 2 * dh), lambda b, p, pt: (0, 0)),
                  pl.BlockSpec((1, n_rows, dh), lambda b, p, pt: (b, 0, 0)),
                  pl.BlockSpec((1, n_new, dh), lambda b, p, pt: (b, 0, 0)),
                  pl.BlockSpec((1, n_new, dh), lambda b, p, pt: (b, 0, 0)),
                  pl.BlockSpec(memory_space=pl.ANY),
                  pl.BlockSpec(memory_space=pl.ANY)],
        out_specs=pl.BlockSpec((1, n_rows // 2, dh), lambda b, p, pt: (b, 0, 0)),
        scratch_shapes=[pltpu.VMEM((n_rows, dh), F32), pltpu.VMEM((n_rows, dh), F32),
                        pltpu.VMEM((n_rows, dh), F32),
                        pltpu.VMEM((dh, dh), F32), pltpu.VMEM((dh, dh), F32),
                        pltpu.VMEM(ring, F32), pltpu.VMEM(ring, F32),
                        pltpu.SemaphoreType.DMA((2, PAGE_RING_DEPTH))],
    )
    return pl.pallas_call(
        kern,
        grid_spec=grid_spec,
        out_shape=jax.ShapeDtypeStruct((bsz, n_rows // 2, dh), F32),
        compiler_params=_params("arbitrary", "arbitrary"),
        name="diff_attn_sample",
    )(page_table.reshape(-1), lam_p, subln_w.reshape(1, 2 * dh), q_rows, k_new, v_new,
      cache_k, cache_v)


def _sample_query_rows(q, n_heads):
    bsz, tl, _ = q.shape
    r = np.arange(4 * n_heads * tl)
    head = n_heads - 1 - r % n_heads
    token = (r // (2 * n_heads)) % tl
    qmap = r // (2 * n_heads * tl)
    return q.reshape(bsz, tl, n_heads, 2, HEAD_DIM)[:, token, head, qmap, :]


def _values_in_stored_order(v, n_heads):
    lead = v.shape[:-2]
    tokens = v.shape[-2]
    v = v.reshape(*lead, tokens, n_heads, 2, HEAD_DIM)
    return jnp.swapaxes(v, -2, -3).reshape(*lead, tokens * 2 * n_heads, HEAD_DIM)


def _rope_tables(pos0, tlen):
    inv = 1.0 / (ROPE_BASE ** jnp.linspace(0.0, 1.0, HEAD_DIM // 2, dtype=F32))
    ang = (pos0 + jnp.arange(tlen)).astype(F32)[:, None] * inv[None, :]
    sign = jnp.tile(jnp.asarray([-1.0, 1.0], F32), HEAD_DIM // 2)
    return jnp.repeat(jnp.cos(ang), 2, axis=1), jnp.repeat(jnp.sin(ang), 2, axis=1) * sign


def _trunk(x, mod, pos0, state_a, state_b, cache_k, cache_v, page_table, w, *, tm, prompt):
    bsz, tlen, d = x.shape
    depth = mod.shape[0]
    xf = x.reshape(bsz * tlen, d)
    hg_out, rt_out, k_out, v_out = [], [], [], []
    for l in range(depth):
        ml = mod[l]
        if l % 2 == 0:
            e = l // 2
            n_in = w["in_ab"].shape[2]
            (proj,) = _inproj(xf, ml, w["norm"][l, 0], w["in_ab"][e], [(n_in, F32, 1.0)],
                              tm=tm, shift_col=0, scale_col=1)
            proj = proj.reshape(bsz, tlen, n_in)
            cos, sin = _rope_tables(pos0, tlen)
            if prompt:
                o, s_a, s_b = _mixer_prompt(proj, cos, sin, w["lb_logits"], w["hgrn_norm"][e],
                                            layer_slot=e, chunk=128, rows_per_step=min(4, bsz))
            else:
                o, s_a, s_b = _mixer_sample(proj, cos, sin, w["lb_logits"], w["hgrn_norm"][e],
                                            state_a[e], state_b[e], layer_slot=e)
            hg_out.append(s_a)
            rt_out.append(s_b)
            w_out = w["out_ab"][e]
        else:
            a = l // 2
            lam_init = 0.8 - 0.6 * math.exp(-0.3 * l)
            n_c = w["in_c"].shape[2] // 3
            n_heads = n_c // (2 * HEAD_DIM)
            group = 2 * n_heads
            q, k, v, k_rows, v_rows = _qkv(xf, ml, w["norm"][l, 0], w["in_c"][a], tm=tm,
                                           q_scale=HEAD_DIM ** -0.5 * LOG2_E, n_heads=n_heads)
            if prompt:
                q, k, v = (t.reshape(bsz, tlen, n_c) for t in (q, k, v))
                o = _attn_prompt(q, k, v, w["diff_lambda"][a], w["subln"][a], lam_init=lam_init,
                                 tq=min(512, tlen))
            else:
                n_slots, slot_pages, page_rows, _ = cache_k.shape
                o = _attn_sample(
                    _sample_query_rows(q.reshape(bsz, tlen, n_c), n_heads),
                    k_rows.reshape(bsz, tlen * group, HEAD_DIM), v_rows.reshape(bsz, tlen * group, HEAD_DIM),
                    cache_k.reshape(n_slots * slot_pages, page_rows, HEAD_DIM),
                    cache_v.reshape(n_slots * slot_pages, page_rows, HEAD_DIM),
                    page_table, w["diff_lambda"][a], w["subln"][a], tl=tlen, n_heads=n_heads,
                    layer_slot=a, slot_pages=slot_pages, lam_init=lam_init, pages_per_step=8)
                o = o.reshape(bsz, tlen, 2, n_heads, HEAD_DIM)[:, :, :, ::-1, :]
                o = jnp.swapaxes(o, 2, 3)
            k_out.append(k_rows)
            v_out.append(v_rows)
            w_out = w["out_c"][a]
        xf = _mlp(o.reshape(bsz * tlen, -1), xf, ml, w["norm"][l, 1], w_out, w["mlp_up"][l],
                  w["mlp_down"][l], w["final_norm"], tm=tm, final_norm=(l == depth - 1))
    return (xf.reshape(bsz, tlen, d), jnp.stack(hg_out), jnp.stack(rt_out),
            jnp.stack(k_out), jnp.stack(v_out))


def kernel(x_prompt, x_sample, state_hgrn, state_ret, cache_k, cache_v, page_table, c_prompt, c_sample,
           w_ada, b_ada, norm_w, w_in_ab, w_out_ab, hgrn_lb_logits, hgrn_norm_w, w_in_c, w_out_c,
           diff_lambda, diff_subln_w, w_mlp_up, w_mlp_down, final_norm_w):
    bp, tp, d = x_prompt.shape
    bs, ts, _ = x_sample.shape
    n_slots, n_phys, page, n_heads_c, _, dh = cache_k.shape

    mod = _ada(jnp.concatenate([c_prompt, c_sample], axis=0), w_ada, b_ada)
    depth = mod.shape[0]
    mod_p = mod[:, :bp].reshape(depth, bp, 1, 6 * d)
    mod_s = jnp.repeat(mod[:, bp:], ts, axis=1).reshape(depth, 1, bs * ts, 6 * d)

    w = dict(norm=norm_w, in_ab=w_in_ab.astype(BF16), out_ab=w_out_ab.astype(BF16),
             lb_logits=hgrn_lb_logits, hgrn_norm=hgrn_norm_w, in_c=w_in_c.astype(BF16),
             out_c=w_out_c.astype(BF16), diff_lambda=diff_lambda, subln=diff_subln_w,
             mlp_up=w_mlp_up.astype(BF16), mlp_down=w_mlp_down.astype(BF16), final_norm=final_norm_w)

    y_p, hg_p, rt_p, k_p, v_p = _trunk(x_prompt, mod_p, 0, None, None, None, None, None, w,
                                       tm=512, prompt=True)
    past_len = page_table.shape[1] * page
    page_rows = page * 2 * n_heads_c
    cache_k_rows = cache_k.reshape(n_slots, n_phys, page_rows, dh)
    cache_v_rows = _values_in_stored_order(cache_v.reshape(n_slots, n_phys, page, n_heads_c * 2 * dh),
                                           n_heads_c)
    y_s, hg_s, rt_s, k_s, v_s = _trunk(x_sample, mod_s, past_len, state_hgrn, state_ret,
                                       cache_k_rows, cache_v_rows, page_table, w,
                                       tm=bs * ts, prompt=False)

    n_odd = k_p.shape[0]

    def values_from_stored_order(v, lead):
        v = jnp.swapaxes(v.reshape(n_odd, *lead, 2, n_heads_c, dh), -2, -3)
        return v.reshape(n_odd, *lead, n_heads_c, 2 * dh)

    k_p = k_p.reshape(n_odd, bp, tp // page, page, n_heads_c, 2, dh)
    v_p = values_from_stored_order(v_p, (bp, tp // page, page))
    k_s = k_s.reshape(n_odd, bs, ts, n_heads_c, 2, dh)
    v_s = values_from_stored_order(v_s, (bs, ts))
    return (y_p, y_s, hg_p, rt_p, k_p, v_p, hg_s, rt_s, k_s, v_s)
```

```python
import functools
import math

import jax
import jax.numpy as jnp
import numpy as np
from jax import lax
from jax.experimental import pallas as pl
from jax.experimental.pallas import tpu as pltpu

F32 = jnp.float32
BF16 = jnp.bfloat16

HEAD_DIM = 128
SUBLANES = 8
NORM_EPS = 1e-6
ROPE_BASE = 10000.0
V7X_VMEM_LIMIT = 52 * 1024 * 1024

_NT = (((1,), (1,)), ((), ()))
_TN = (((0,), (0,)), ((), ()))


def _params(*semantics):
    return pltpu.CompilerParams(dimension_semantics=semantics, vmem_limit_bytes=V7X_VMEM_LIMIT)


def _resident(shape):
    zeros = (0,) * len(shape)
    return pl.BlockSpec(shape, lambda *_: zeros, pipeline_mode=pl.Buffered(1))


def _sigmoid(x):
    return 0.5 + 0.5 * jnp.tanh(0.5 * x)


def _silu(x):
    half = 0.5 * x
    return half + half * jnp.tanh(half)


def _rms(x):
    return x * lax.rsqrt(jnp.mean(x * x, axis=-1, keepdims=True) + NORM_EPS)


def _mm(a, b):
    return jnp.dot(a, b, preferred_element_type=F32)


def _ada_kernel(c_ref, w_ref, b_ref, o_ref):
    s = _silu(c_ref[...]).astype(BF16)
    o_ref[0] = _mm(s, w_ref[0].astype(BF16)) + b_ref[0]


def _ada(c, w_ada, b_ada):
    depth, d, n = w_ada.shape
    r = c.shape[0]
    tn = n // 4
    return pl.pallas_call(
        _ada_kernel,
        grid=(depth, n // tn),
        in_specs=[pl.BlockSpec((r, d), lambda l, j: (0, 0)),
                  pl.BlockSpec((1, d, tn), lambda l, j: (l, 0, j)),
                  pl.BlockSpec((1, 1, tn), lambda l, j: (l, 0, j))],
        out_specs=pl.BlockSpec((1, r, tn), lambda l, j: (l, 0, j)),
        out_shape=jax.ShapeDtypeStruct((depth, r, n), F32),
        compiler_params=_params("arbitrary", "arbitrary"),
        name="ada_mod",
    )(c, w_ada, b_ada.reshape(depth, 1, n))


def _modulated_norm(x, nw_ref, mod_ref, shift_col, scale_col):
    d = x.shape[-1]
    shift = mod_ref[0, :, shift_col * d:(shift_col + 1) * d]
    scale = mod_ref[0, :, scale_col * d:(scale_col + 1) * d]
    return _rms(x) * nw_ref[...] * (1.0 + scale) + shift


def _inproj_kernel(x_ref, mod_ref, nw_ref, w_ref, *out_refs, shift_col, scale_col, out_scales, tc):
    h = _modulated_norm(x_ref[...], nw_ref, mod_ref, shift_col, scale_col).astype(BF16)
    off = 0
    for o_ref, out_scale in zip(out_refs, out_scales):
        width = o_ref.shape[-1]
        for c in range(0, width, tc):
            y = _mm(h, w_ref[:, off + c:off + c + tc])
            if out_scale != 1.0:
                y = y * out_scale
            o_ref[:, c:c + tc] = y.astype(o_ref.dtype)
        off += width


def _inproj(x, mod, nw, w, outs, *, tm, shift_col, scale_col):
    m, d = x.shape
    n = w.shape[1]
    rows_per_mod = m // mod.shape[0]
    tiles_per_mod = rows_per_mod // tm
    kern = functools.partial(_inproj_kernel, shift_col=shift_col, scale_col=scale_col,
                             out_scales=tuple(o[2] for o in outs), tc=512)
    return pl.pallas_call(
        kern,
        grid=(m // tm,),
        in_specs=[pl.BlockSpec((tm, d), lambda i: (i, 0)),
                  pl.BlockSpec((1, mod.shape[1], mod.shape[2]), lambda i: (i // tiles_per_mod, 0, 0)),
                  _resident((1, d)),
                  _resident((d, n))],
        out_specs=[pl.BlockSpec((tm, o[0]), lambda i: (i, 0)) for o in outs],
        out_shape=[jax.ShapeDtypeStruct((m, o[0]), o[1]) for o in outs],
        compiler_params=_params("parallel"),
        name="norm_mod_inproj",
    )(x, mod, nw.reshape(1, d), w)


def _qkv_kernel(x_ref, mod_ref, nw_ref, w_ref, q_ref, kb_ref, vb_ref, kr_ref, vr_ref, *,
                q_scale, n_heads, tc):
    h = _modulated_norm(x_ref[...], nw_ref, mod_ref, 0, 1).astype(BF16)
    tm, n_c = q_ref.shape
    lanes = HEAD_DIM
    group = 2 * n_heads
    for c in range(0, n_c, tc):
        q_ref[:, c:c + tc] = (_mm(h, w_ref[:, c:c + tc]) * q_scale).astype(q_ref.dtype)
    for c in range(0, n_c, tc):
        y = _mm(h, w_ref[:, n_c + c:n_c + c + tc])
        kb_ref[:, c:c + tc] = y.astype(kb_ref.dtype)
        for i in range(tc // lanes):
            kr_ref[pl.ds(c // lanes + i, tm, stride=group), :] = y[:, i * lanes:(i + 1) * lanes]
    for c in range(0, n_c, tc):
        y = _mm(h, w_ref[:, 2 * n_c + c:2 * n_c + c + tc])
        vb_ref[:, c:c + tc] = y.astype(vb_ref.dtype)
        for i in range(tc // lanes):
            head, half = divmod(c // lanes + i, 2)
            vr_ref[pl.ds(half * n_heads + head, tm, stride=group), :] = y[:, i * lanes:(i + 1) * lanes]


def _qkv(x, mod, nw, w, *, tm, q_scale, n_heads):
    m, d = x.shape
    n_c = w.shape[1] // 3
    group = 2 * n_heads
    tiles_per_mod = (m // mod.shape[0]) // tm
    kern = functools.partial(_qkv_kernel, q_scale=q_scale, n_heads=n_heads, tc=512)
    cols = pl.BlockSpec((tm, n_c), lambda i: (i, 0))
    rows = pl.BlockSpec((tm * group, HEAD_DIM), lambda i: (i, 0))
    return pl.pallas_call(
        kern,
        grid=(m // tm,),
        in_specs=[pl.BlockSpec((tm, d), lambda i: (i, 0)),
                  pl.BlockSpec((1, mod.shape[1], mod.shape[2]), lambda i: (i // tiles_per_mod, 0, 0)),
                  _resident((1, d)),
                  _resident((d, 3 * n_c))],
        out_specs=[cols, cols, cols, rows, rows],
        out_shape=[jax.ShapeDtypeStruct((m, n_c), BF16)] * 3
                  + [jax.ShapeDtypeStruct((m * group, HEAD_DIM), F32)] * 2,
        compiler_params=_params("parallel"),
        name="norm_mod_qkv",
    )(x, mod, nw.reshape(1, d), w)


N_MLP_INPUTS = 8
N_ATTENTION_INPUTS = 7


def _mlp_kernel(*refs, n_parts, final_norm, attention):
    if attention is None:
        o_ref, x_ref, mod_ref, nw_ref, wo_ref, wu_ref, wd_ref, fw_ref, out_ref, h_ref, acc_ref = refs
    else:
        pt_ref, refs = refs[0], refs[1:]
        o_ref, x_ref, mod_ref, nw_ref, wo_ref, wu_ref, wd_ref, fw_ref = refs[:N_MLP_INPUTS]
        attention_in = refs[N_MLP_INPUTS:N_MLP_INPUTS + N_ATTENTION_INPUTS]
        out_ref, attention_out, h_ref, acc_ref = refs[N_MLP_INPUTS + N_ATTENTION_INPUTS:][:4]
        attention_refs = (pt_ref, *attention_in, attention_out,
                          *refs[N_MLP_INPUTS + N_ATTENTION_INPUTS + 4:])
    d = x_ref.shape[-1]
    cw = wu_ref.shape[1] // n_parts

    def mod_col(col):
        return mod_ref[0, :, col * d:(col + 1) * d]

    hidden = [None] * n_parts

    def mlp_up(r):
        if r == 0:
            x1 = x_ref[...] + mod_col(2) * _mm(o_ref[...].astype(BF16), wo_ref[...])
            out_ref[...] = x1
            h_ref[...] = _modulated_norm(x1, nw_ref, mod_ref, 3, 4).astype(BF16)
        u = jnp.maximum(_mm(h_ref[...], wu_ref[:, r * cw:(r + 1) * cw]), 0.0)
        hidden[r] = (u * u).astype(BF16)

    def mlp_down(r):
        part = _mm(hidden[r], wd_ref[r * cw:(r + 1) * cw, :])
        acc_ref[...] = part if r == 0 else acc_ref[...] + part

    if attention is None:
        for r in range(n_parts):
            mlp_up(r)
            mlp_down(r)
    else:
        aligned = attention["steps_per_seq"] % n_parts == 0
        for r in range(n_parts):
            _paged_attention_step(pl.program_id(0) * n_parts + r, pl.num_programs(0) * n_parts,
                                  attention_refs,
                                  (functools.partial(mlp_up, r), functools.partial(mlp_down, r)),
                                  may_open=(r == 0 or not aligned),
                                  may_close=(r == n_parts - 1 or not aligned), **attention)

    y = out_ref[...] + mod_col(5) * acc_ref[...]
    if final_norm:
        y = _rms(y) * fw_ref[...]
    out_ref[...] = y


def _mlp(o, x, mod, nw, w_out, w_up, w_down, final_w, *, tm, final_norm, n_parts, paged=None):
    m, d = x.shape
    k = o.shape[1]
    dff = w_up.shape[1]
    n_tiles = m // tm
    tiles_per_mod = (m // mod.shape[0]) // tm
    in_specs = [pl.BlockSpec((tm, k), lambda i, *_: (i, 0)),
                pl.BlockSpec((tm, d), lambda i, *_: (i, 0)),
                pl.BlockSpec((1, mod.shape[1], mod.shape[2]), lambda i, *_: (i // tiles_per_mod, 0, 0)),
                _resident((1, d)),
                _resident((k, d)),
                _resident((d, dff)),
                _resident((dff, d)),
                _resident((1, d))]
    operands = [o, x, mod, nw.reshape(1, d), w_out, w_up, w_down, final_w.reshape(1, d)]
    out_specs = pl.BlockSpec((tm, d), lambda i, *_: (i, 0))
    out_shape = jax.ShapeDtypeStruct((m, d), F32)
    scratch = [pltpu.VMEM((tm, d), BF16), pltpu.VMEM((tm, d), F32)]
    if paged is None:
        kern = functools.partial(_mlp_kernel, n_parts=n_parts, final_norm=final_norm, attention=None)
        return pl.pallas_call(
            kern, grid=(n_tiles,), in_specs=in_specs, out_specs=out_specs, out_shape=out_shape,
            scratch_shapes=scratch, compiler_params=_params("arbitrary"), name="outproj_mlp_residual",
        )(*operands)

    q_rows, k_new, v_new = paged["q_rows"], paged["k_new"], paged["v_new"]
    cache_k, cache_v, page_table = paged["cache_k"], paged["cache_v"], paged["page_table"]
    n_seq = paged["n_seq"]
    n_rows, dh = q_rows.shape[1:]
    n_pages = page_table.shape[1]
    page_rows = cache_k.shape[1]
    assert 2 * paged["n_heads"] == SUBLANES, "one (map, token) block of query rows must fill one vreg"
    pages_per_step = (n_seq * n_pages) // (n_tiles * n_parts)
    assert pages_per_step * n_tiles * n_parts == n_seq * n_pages and n_pages % pages_per_step == 0
    attention = dict(pages_per_step=pages_per_step, page_base=paged["page_base"],
                     steps_per_seq=n_pages // pages_per_step, first_seq=paged["first_seq"],
                     tl=paged["tl"], n_heads=paged["n_heads"], lam_init=paged["lam_init"])
    kern = functools.partial(_mlp_kernel, n_parts=n_parts, final_norm=final_norm, attention=attention)
    ring = (PAGE_RING_DEPTH, pages_per_step, page_rows, dh)
    grid_spec = pltpu.PrefetchScalarGridSpec(
        num_scalar_prefetch=1,
        grid=(n_tiles,),
        in_specs=in_specs + [_resident(paged["lam_p"].shape), _resident((1, 2 * dh)),
                             _resident(q_rows.shape), _resident(k_new.shape), _resident(v_new.shape),
                             pl.BlockSpec(memory_space=pl.ANY), pl.BlockSpec(memory_space=pl.ANY)],
        out_specs=[out_specs, pl.BlockSpec((n_seq, n_rows // 2, dh), lambda i, *_: (0, 0, 0))],
        scratch_shapes=scratch + [pltpu.VMEM((n_rows, dh), F32), pltpu.VMEM((n_rows, dh), F32),
                                  pltpu.VMEM((n_rows, dh), F32),
                                  pltpu.VMEM((dh, dh), F32), pltpu.VMEM((dh, dh), F32),
                                  pltpu.VMEM(ring, F32), pltpu.VMEM(ring, F32),
                                  pltpu.SemaphoreType.DMA((2, PAGE_RING_DEPTH))],
    )
    return pl.pallas_call(
        kern,
        grid_spec=grid_spec,
        out_shape=[out_shape, jax.ShapeDtypeStruct((n_seq, n_rows // 2, dh), F32)],
        compiler_params=_params("arbitrary"),
        name="outproj_mlp_paged_attention",
    )(page_table.reshape(-1), *operands, paged["lam_p"], paged["subln_w"].reshape(1, 2 * dh),
      q_rows, k_new, v_new, cache_k, cache_v)


def _hgrn_lower_bound(logits_ref, layer_slot):
    lg = logits_ref[...]
    ex = jnp.exp(lg - jnp.max(lg, axis=0, keepdims=True))
    pr = ex / jnp.sum(ex, axis=0, keepdims=True)
    return jnp.sum(pr[:layer_slot + 1], axis=0, keepdims=True)


def _rotate_pairs(x, cos, sin_signed):
    n = x.shape[-1]
    lane = lax.broadcasted_iota(jnp.int32, x.shape, 1)
    swapped = jnp.where((lane & 1) == 0, pltpu.roll(x, n - 1, 1), pltpu.roll(x, 1, 1))
    return x * cos + swapped * sin_signed


def _cumsum_rows(x):
    n = x.shape[0]
    row = lax.broadcasted_iota(jnp.int32, x.shape, 0)
    d = 1
    while d < n:
        x = x + jnp.where(row >= d, pltpu.roll(x, d, 0), 0.0)
        d *= 2
    return x


def _split_row(g, b):
    n, lanes = g.shape
    if 2 * b >= SUBLANES:
        g3 = g.reshape(n // (2 * b), 2 * b, lanes)
        return jnp.broadcast_to(g3[:, b - 1:b, :], g3.shape).reshape(n, lanes)
    rmod = lax.broadcasted_iota(jnp.int32, g.shape, 0) & (2 * b - 1)
    out = g
    for r in range(2 * b):
        if r != b - 1:
            out = jnp.where(rmod == r, pltpu.roll(g, (r - (b - 1)) % n, 0), out)
    return out


def _level_matrix(n):
    t = np.arange(n)[:, None]
    s = np.arange(n)[None, :]
    hi = np.floor(np.log2(np.maximum(t ^ s, 1))).astype(np.int32)
    return jnp.asarray(np.where(s < t, hi, -1).astype(np.int32))


def _hgrn_chunk(q, k, v, g2, lv, st):
    n = q.shape[0]
    row = lax.broadcasted_iota(jnp.int32, q.shape, 0)
    ri = lax.broadcasted_iota(jnp.int32, (n, n), 0)
    ci = lax.broadcasted_iota(jnp.int32, (n, n), 1)
    a = jnp.where(ri == ci, jnp.sum(q * k, axis=-1, keepdims=True), 0.0)
    b, level = 1, 0
    while b < n:
        if b >= SUBLANES:
            parts = []
            for lo in range(0, n, 2 * b):
                edge = g2[lo + b - 1:lo + b, :]
                parts.append(k[lo:lo + b] * jnp.exp2(edge - g2[lo:lo + b]))
                parts.append(q[lo + b:lo + 2 * b] * jnp.exp2(g2[lo + b:lo + 2 * b] - edge))
            z = jnp.concatenate(parts, axis=0).astype(BF16)
        else:
            is_query = (row & b) != 0
            sign = jnp.where(is_query, 1.0, -1.0)
            z = (jnp.where(is_query, q, k) * jnp.exp2((g2 - _split_row(g2, b)) * sign)).astype(BF16)
        a = jnp.where(lv == level, lax.dot_general(z, z, _NT, preferred_element_type=F32), a)
        b, level = 2 * b, level + 1
    carried = lax.dot_general((q * jnp.exp2(g2)).astype(BF16), st.astype(BF16), _NT,
                              preferred_element_type=F32)
    o = _mm(a.astype(BF16), v.astype(BF16)) + carried
    g_end = g2[n - 1:n, :]
    k_dec = (k * jnp.exp2(g_end - g2)).astype(BF16)
    st_new = st * jnp.exp2(g_end) + lax.dot_general(v.astype(BF16), k_dec, _TN,
                                                    preferred_element_type=F32)
    return o, st_new


def _retention_chunk(q, k, v, log_gamma, st):
    n = q.shape[0]
    rowf = lax.broadcasted_iota(jnp.int32, q.shape, 0).astype(F32)
    ri = lax.broadcasted_iota(jnp.int32, (n, n), 0)
    ci = lax.broadcasted_iota(jnp.int32, (n, n), 1)
    decay = jnp.where(ci <= ri, jnp.exp((ri - ci).astype(F32) * log_gamma), 0.0)
    a = lax.dot_general(q.astype(BF16), k.astype(BF16), _NT, preferred_element_type=F32) * decay
    q_dec = (q * jnp.exp((rowf + 1.0) * log_gamma)).astype(BF16)
    o = _mm(a.astype(BF16), v.astype(BF16)) + _mm(q_dec, st.astype(BF16))
    k_dec = (k * jnp.exp((n - 1.0 - rowf) * log_gamma)).astype(BF16)
    st_new = st * math.exp(n * log_gamma) + lax.dot_general(k_dec, v.astype(BF16), _TN,
                                                            preferred_element_type=F32)
    return o, st_new


def _retention_log_gamma(h):
    return math.log(1.0 - 2.0 ** (-5.0 - h))


def _mixer_prompt_kernel(proj_ref, cos_ref, sin_ref, lv_ref, lbl_ref, na_ref,
                         o_ref, sa_out, sb_out, sat_ref, sb_ref, *, layer_slot, n_heads):
    t = pl.program_id(1)
    hd = HEAD_DIM
    w = n_heads * hd

    @pl.when(t == 0)
    def _():
        sat_ref[...] = jnp.zeros_like(sat_ref)
        sb_ref[...] = jnp.zeros_like(sb_ref)

    lb = _hgrn_lower_bound(lbl_ref, layer_slot)
    lv = lv_ref[...]
    cos = cos_ref[...]
    sin = sin_ref[...]
    for r in range(proj_ref.shape[0]):
        def cols(group, h):
            return proj_ref[r, :, group * w + h * hd:group * w + (h + 1) * hd]

        fz = lb + (1.0 - lb) * _sigmoid(proj_ref[r, :, w:2 * w])
        g_all = _cumsum_rows(jnp.log2(fz))
        k_all = 1.0 - fz
        for h in range(n_heads):
            sl = slice(h * hd, (h + 1) * hd)
            q = _silu(cols(0, h)) * hd ** -0.5
            o, st = _hgrn_chunk(q, k_all[:, sl], cols(2, h), g_all[:, sl], lv, sat_ref[r, h])
            sat_ref[r, h] = st
            o_ref[r, :, sl] = (_rms(o) * na_ref[...] * _silu(cols(3, h))).astype(o_ref.dtype)
        for h in range(n_heads):
            q = _rotate_pairs(cols(4, h), cos, sin)
            k = _rotate_pairs(cols(5, h), cos, sin) * hd ** -0.5
            o, st = _retention_chunk(q, k, cols(6, h), _retention_log_gamma(h), sb_ref[r, h])
            sb_ref[r, h] = st
            o_ref[r, :, w + h * hd:w + (h + 1) * hd] = (_rms(o) * _silu(cols(7, h))).astype(o_ref.dtype)

    @pl.when(t == pl.num_programs(1) - 1)
    def _():
        for r in range(proj_ref.shape[0]):
            for h in range(n_heads):
                sa_out[r, h] = sat_ref[r, h].T
                sb_out[r, h] = sb_ref[r, h]


def _mixer_prompt(proj, cos, sin, lb_logits, norm_a, *, layer_slot, chunk, rows_per_step):
    bsz, tlen, n = proj.shape
    hd = HEAD_DIM
    n_heads = n // (8 * hd)
    rb = rows_per_step
    kern = functools.partial(_mixer_prompt_kernel, layer_slot=layer_slot, n_heads=n_heads)
    state = jax.ShapeDtypeStruct((bsz, n_heads, hd, hd), F32)
    state_spec = pl.BlockSpec((rb, n_heads, hd, hd), lambda b, t: (b, 0, 0, 0))
    return pl.pallas_call(
        kern,
        grid=(bsz // rb, tlen // chunk),
        in_specs=[pl.BlockSpec((rb, chunk, n), lambda b, t: (b, t, 0)),
                  pl.BlockSpec((chunk, hd), lambda b, t: (t, 0)),
                  pl.BlockSpec((chunk, hd), lambda b, t: (t, 0)),
                  pl.BlockSpec((chunk, chunk), lambda b, t: (0, 0)),
                  pl.BlockSpec(lb_logits.shape, lambda b, t: (0, 0)),
                  pl.BlockSpec((1, hd), lambda b, t: (0, 0))],
        out_specs=[pl.BlockSpec((rb, chunk, 2 * n_heads * hd), lambda b, t: (b, t, 0)),
                   state_spec, state_spec],
        out_shape=[jax.ShapeDtypeStruct((bsz, tlen, 2 * n_heads * hd), BF16), state, state],
        scratch_shapes=[pltpu.VMEM((rb, n_heads, hd, hd), F32), pltpu.VMEM((rb, n_heads, hd, hd), F32)],
        compiler_params=_params("arbitrary", "arbitrary"),
        name="mixer_prompt",
    )(proj, cos, sin, _level_matrix(chunk), lb_logits, norm_a.reshape(1, hd))


def _pair_chunk(q, k, v, g, s0, dec, kpad_ref, vpad_ref):
    tl = q.shape[0]
    row = lax.broadcasted_iota(jnp.int32, q.shape, 0)
    o = _mm((q * jnp.exp(g)).astype(BF16), s0.astype(BF16))
    for s in range(tl):
        live = row >= s
        w = jnp.exp(jnp.where(live, g - g[s:s + 1, :], 0.0))
        a = jnp.sum(jnp.where(live, q * w * k[s:s + 1, :], 0.0), axis=-1, keepdims=True)
        o = o + a * v[s:s + 1, :]
    kpad_ref[0:tl, :] = k * jnp.exp(g[tl - 1:tl, :] - g)
    vpad_ref[0:tl, :] = v
    upd = lax.dot_general(kpad_ref[...].astype(BF16), vpad_ref[...].astype(BF16), _TN,
                          preferred_element_type=F32)
    return o, s0 * dec + upd


def _mixer_sample_kernel(proj_ref, cos_ref, sin_ref, lbl_ref, na_ref, sa_ref, sb_ref,
                         o_ref, sa_out, sb_out, kpad_ref, vpad_ref, *, layer_slot, n_heads):
    hd = HEAD_DIM
    w = n_heads * hd
    tl = proj_ref.shape[1]
    kpad_ref[...] = jnp.zeros_like(kpad_ref)
    vpad_ref[...] = jnp.zeros_like(vpad_ref)

    def cols(group, h):
        return proj_ref[0, :, group * w + h * hd:group * w + (h + 1) * hd]

    lb = _hgrn_lower_bound(lbl_ref, layer_slot)
    fz = lb + (1.0 - lb) * _sigmoid(proj_ref[0, :, w:2 * w])
    lf = jnp.log(fz)
    row = lax.broadcasted_iota(jnp.int32, lf.shape, 0)
    g_all = jnp.zeros_like(lf)
    for j in range(tl):
        g_all = g_all + jnp.where(row >= j, lf[j:j + 1, :], 0.0)
    k_all = 1.0 - fz
    for h in range(n_heads):
        sl = slice(h * hd, (h + 1) * hd)
        g = g_all[:, sl]
        q = _silu(cols(0, h)) * hd ** -0.5
        dec = jnp.broadcast_to(jnp.exp(g[tl - 1:tl, :]), (hd, hd)).T
        o, st = _pair_chunk(q, k_all[:, sl], cols(2, h), g, sa_ref[0, h], dec, kpad_ref, vpad_ref)
        sa_out[0, h] = st
        o_ref[0, :, sl] = _rms(o) * na_ref[...] * _silu(cols(3, h))

    cos = cos_ref[...]
    sin = sin_ref[...]
    rowf = lax.broadcasted_iota(jnp.int32, (tl, hd), 0).astype(F32)
    for h in range(n_heads):
        lg = _retention_log_gamma(h)
        q = _rotate_pairs(cols(4, h), cos, sin)
        k = _rotate_pairs(cols(5, h), cos, sin) * hd ** -0.5
        o, st = _pair_chunk(q, k, cols(6, h), (rowf + 1.0) * lg, sb_ref[0, h], math.exp(tl * lg),
                            kpad_ref, vpad_ref)
        sb_out[0, h] = st
        o_ref[0, :, w + h * hd:w + (h + 1) * hd] = _rms(o) * _silu(cols(7, h))


def _mixer_sample(proj, cos, sin, lb_logits, norm_a, state_a, state_b, *, layer_slot):
    bsz, tl, n = proj.shape
    hd = HEAD_DIM
    n_heads = n // (8 * hd)
    kern = functools.partial(_mixer_sample_kernel, layer_slot=layer_slot, n_heads=n_heads)
    state = jax.ShapeDtypeStruct((bsz, n_heads, hd, hd), F32)
    state_spec = pl.BlockSpec((1, n_heads, hd, hd), lambda b: (b, 0, 0, 0))
    return pl.pallas_call(
        kern,
        grid=(bsz,),
        in_specs=[pl.BlockSpec((1, tl, n), lambda b: (b, 0, 0)),
                  pl.BlockSpec((tl, hd), lambda b: (0, 0)),
                  pl.BlockSpec((tl, hd), lambda b: (0, 0)),
                  pl.BlockSpec(lb_logits.shape, lambda b: (0, 0)),
                  pl.BlockSpec((1, hd), lambda b: (0, 0)),
                  state_spec, state_spec],
        out_specs=[pl.BlockSpec((1, tl, 2 * n_heads * hd), lambda b: (b, 0, 0)), state_spec, state_spec],
        out_shape=[jax.ShapeDtypeStruct((bsz, tl, 2 * n_heads * hd), F32), state, state],
        scratch_shapes=[pltpu.VMEM((hd, hd), F32), pltpu.VMEM((hd, hd), F32)],
        compiler_params=_params("arbitrary"),
        name="mixer_sample",
    )(proj, cos, sin, lb_logits, norm_a.reshape(1, hd), state_a, state_b)


def _diff_lambda(lam_ref, lam_init):
    lp = lam_ref[...]
    return (jnp.exp(jnp.sum(lp[0:1] * lp[1:2], axis=-1, keepdims=True))
            - jnp.exp(jnp.sum(lp[2:3] * lp[3:4], axis=-1, keepdims=True)) + lam_init)


LOG2_E = math.log2(math.e)


def _attn_prompt_kernel(lam_ref, sw_ref, q_ref, k_ref, v_ref, o_ref, m_ref, l_ref, acc_ref, *,
                        tq, lam_init):
    dh = HEAD_DIM
    lanes = m_ref.shape[-1]
    n_tiles = q_ref.shape[1] // tq
    m_ref[...] = jnp.full_like(m_ref, -jnp.inf)
    l_ref[...] = jnp.zeros_like(l_ref)
    acc_ref[...] = jnp.zeros_like(acc_ref)

    def tile_pair(q0, nq, k0, nk):
        rq = slice(q0, q0 + nq)
        rk = slice(k0, k0 + nk)
        v = v_ref[0, rk, :]
        scores = []
        for j in range(2):
            s = lax.dot_general(q_ref[0, rq, j * dh:(j + 1) * dh], k_ref[0, rk, j * dh:(j + 1) * dh],
                                _NT, preferred_element_type=F32)
            if k0 + nk - 1 > q0:
                ri = lax.broadcasted_iota(jnp.int32, s.shape, 0)
                ci = lax.broadcasted_iota(jnp.int32, s.shape, 1)
                s = jnp.where(ci - ri <= q0 - k0, s, -jnp.inf)
            scores.append(s)
        probs, alphas = [], []
        for j in range(2):
            s = scores[j]
            m_old = m_ref[j, rq]
            m_new = jnp.maximum(m_old, jnp.max(s, axis=-1, keepdims=True))
            alpha = jnp.exp2(m_old - m_new)
            l_new = alpha * l_ref[j, rq]
            tiles = []
            for c in range(0, s.shape[1], lanes):
                p = jnp.exp2(s[:, c:c + lanes] - m_new)
                l_new = l_new + p
                tiles.append(p.astype(BF16))
            l_ref[j, rq] = l_new
            m_ref[j, rq] = m_new
            probs.append(jnp.concatenate(tiles, axis=1))
            alphas.append(alpha)
        for j in range(2):
            pv = _mm(probs[j], v)
            for c in range(0, pv.shape[1], lanes):
                acc_ref[j, rq, c:c + lanes] = (alphas[j] * acc_ref[j, rq, c:c + lanes]
                                               + pv[:, c:c + lanes])

    for kj in range(n_tiles):
        for qi in range(kj, n_tiles):
            tile_pair(qi * tq, tq, kj * tq, tq)

    lam = _diff_lambda(lam_ref, lam_init)
    for qi in range(n_tiles):
        rq = slice(qi * tq, (qi + 1) * tq)
        l0 = jnp.sum(l_ref[0, rq], axis=-1, keepdims=True)
        l1 = jnp.sum(l_ref[1, rq], axis=-1, keepdims=True)
        o = acc_ref[0, rq] / l0 - lam * (acc_ref[1, rq] / l1)
        o_ref[0, rq] = (_rms(o) * sw_ref[...] * (1.0 - lam_init)).astype(o_ref.dtype)


def _attn_prompt(q, k, v, lam_p, subln_w, *, lam_init, tq):
    bsz, tlen, n = q.shape
    e = 2 * HEAD_DIM
    n_heads = n // e
    kern = functools.partial(_attn_prompt_kernel, tq=tq, lam_init=lam_init)
    head_cols = pl.BlockSpec((1, tlen, e), lambda b, h: (b, 0, h))
    return pl.pallas_call(
        kern,
        grid=(bsz, n_heads),
        in_specs=[pl.BlockSpec(lam_p.shape, lambda b, h: (0, 0)),
                  pl.BlockSpec((1, e), lambda b, h: (0, 0)),
                  head_cols, head_cols, head_cols],
        out_specs=head_cols,
        out_shape=jax.ShapeDtypeStruct((bsz, tlen, n), BF16),
        scratch_shapes=[pltpu.VMEM((2, tlen, HEAD_DIM), F32), pltpu.VMEM((2, tlen, HEAD_DIM), F32),
                        pltpu.VMEM((2, tlen, e), F32)],
        compiler_params=_params("parallel", "parallel"),
        name="diff_attn_prompt",
    )(lam_p, subln_w.reshape(1, e), q, k, v)


def _sample_row_fields(shape, tl, n_heads):
    r = lax.broadcasted_iota(jnp.int32, shape, 0)
    head = n_heads - 1 - r % n_heads
    half = (r // n_heads) % 2
    token = (r // (2 * n_heads)) % tl
    qmap = r // (2 * n_heads * tl)
    return head, qmap, token, half


PAGE_RING_DEPTH = 2


def _paged_attention_step(step, n_steps, refs, overlapped_work, *, pages_per_step, page_base,
                          steps_per_seq, first_seq, tl, n_heads, lam_init, may_open, may_close):
    (pt_ref, lam_ref, sw_ref, q_ref, kn_ref, vn_ref, ck_hbm, cv_hbm, o_ref,
     m_ref, l_ref, acc_ref, kpad_ref, vpad_ref, kbuf, vbuf, sems) = refs
    seq = step // steps_per_seq
    p = step % steps_per_seq
    group = 2 * n_heads
    n_rows = q_ref.shape[1]
    rows_per_map = n_rows // 2

    def page_copies(of_step, slot):
        out = []
        for i in range(pages_per_step):
            page = page_base + pt_ref[first_seq * steps_per_seq * pages_per_step
                                      + of_step * pages_per_step + i]
            out.append(pltpu.make_async_copy(ck_hbm.at[page], kbuf.at[slot, i], sems.at[0, slot]))
            out.append(pltpu.make_async_copy(cv_hbm.at[page], vbuf.at[slot, i], sems.at[1, slot]))
        return out

    def start_step(of_step):
        for cp in page_copies(of_step, of_step % PAGE_RING_DEPTH):
            cp.start()

    @pl.when(step == 0)
    def _():
        start_step(0)

    @pl.when(step + 1 < n_steps)
    def _():
        start_step(step + 1)

    slot = step % PAGE_RING_DEPTH
    for cp in page_copies(step, slot):
        cp.wait()

    if may_open:
        @pl.when(p == 0)
        def _():
            m_ref[...] = jnp.full_like(m_ref, -jnp.inf)
            l_ref[...] = jnp.zeros_like(l_ref)
            acc_ref[...] = jnp.zeros_like(acc_ref)

    q = q_ref[first_seq + seq]

    def own_lanes(n_keys, causal):
        head, qmap, token, _ = _sample_row_fields((n_rows, n_keys), tl, n_heads)
        key = lax.broadcasted_iota(jnp.int32, (n_rows, n_keys), 1)
        own = (key % group) == 2 * head + qmap
        return jnp.logical_and(own, key // group <= token) if causal else own

    def partial_softmax(pages, causal, after_scores=None):
        own = own_lanes(pages[0][0].shape[0], causal)
        scores = [lax.dot_general(q, k_rows.astype(BF16), _NT, preferred_element_type=F32)
                  for k_rows, _ in pages]
        if after_scores is not None:
            after_scores()
        stats, probs = [], []
        for s in scores:
            s = jnp.where(own, s, -jnp.inf)
            m = jnp.max(s, axis=-1, keepdims=True)
            pexp = jnp.exp2(s - m)
            stats.append((m, jnp.sum(pexp, axis=-1, keepdims=True)))
            aligned = []
            for blk in range(n_rows // SUBLANES):
                shift = group - (n_heads - 1) - (blk * SUBLANES) // rows_per_map
                aligned.append(pltpu.roll(pexp[blk * SUBLANES:(blk + 1) * SUBLANES, :], shift, 1,
                                          stride=1, stride_axis=0))
            probs.append(jnp.concatenate(aligned, axis=0).astype(BF16))
        return [(m, l, _mm(p_rows, pltpu.roll(v_rows, group, 0).astype(BF16)))
                for (m, l), p_rows, (_, v_rows) in zip(stats, probs, pages)]

    def merge(parts):
        m_run = m_ref[...]
        m_new = m_run
        for m, _, _ in parts:
            m_new = jnp.maximum(m_new, m)
        alpha = jnp.exp2(m_run - m_new)
        l_new = alpha * l_ref[...]
        acc_new = alpha * acc_ref[...]
        for m, l, o in parts:
            wgt = jnp.exp2(m - m_new)
            l_new = l_new + wgt * l
            acc_new = acc_new + wgt * o
        m_ref[...] = m_new
        l_ref[...] = l_new
        acc_ref[...] = acc_new

    after_scores, after_values = overlapped_work
    parts = partial_softmax([(kbuf[slot, i], vbuf[slot, i]) for i in range(pages_per_step)], False,
                            after_scores)
    after_values()
    merge(parts)

    if may_close:
        @pl.when(p == steps_per_seq - 1)
        def _():
            n_new = kn_ref.shape[1]
            kpad_ref[...] = jnp.zeros_like(kpad_ref)
            vpad_ref[...] = jnp.zeros_like(vpad_ref)
            kpad_ref[0:n_new, :] = kn_ref[first_seq + seq]
            vpad_ref[0:n_new, :] = vn_ref[first_seq + seq]
            merge(partial_softmax([(kpad_ref[...], vpad_ref[...])], True))

            o = (acc_ref[0:rows_per_map, :] / l_ref[0:rows_per_map, :]
                 - _diff_lambda(lam_ref, lam_init) * (acc_ref[rows_per_map:, :] / l_ref[rows_per_map:, :]))
            _, _, _, half = _sample_row_fields(o.shape, tl, n_heads)
            ss = jnp.broadcast_to(jnp.sum(o * o, axis=-1, keepdims=True), o.shape)
            other = jnp.where(half == 0, pltpu.roll(ss, rows_per_map - n_heads, 0),
                              pltpu.roll(ss, n_heads, 0))
            inv = lax.rsqrt((ss + other) / (2 * HEAD_DIM) + NORM_EPS)
            sw = jnp.where(half == 0, sw_ref[:, 0:HEAD_DIM], sw_ref[:, HEAD_DIM:])
            o_ref[seq] = o * inv * sw * (1.0 - lam_init)


def _sample_query_rows(q, n_heads):
    bsz, tl, _ = q.shape
    r = np.arange(4 * n_heads * tl)
    head = n_heads - 1 - r % n_heads
    token = (r // (2 * n_heads)) % tl
    qmap = r // (2 * n_heads * tl)
    return q.reshape(bsz, tl, n_heads, 2, HEAD_DIM)[:, token, head, qmap, :]


def _values_in_stored_order(v, n_heads):
    lead = v.shape[:-2]
    tokens = v.shape[-2]
    v = v.reshape(*lead, tokens, n_heads, 2, HEAD_DIM)
    return jnp.swapaxes(v, -2, -3).reshape(*lead, tokens * 2 * n_heads, HEAD_DIM)


def _rope_tables(pos0, tlen):
    inv = 1.0 / (ROPE_BASE ** jnp.linspace(0.0, 1.0, HEAD_DIM // 2, dtype=F32))
    ang = (pos0 + jnp.arange(tlen)).astype(F32)[:, None] * inv[None, :]
    sign = jnp.tile(jnp.asarray([-1.0, 1.0], F32), HEAD_DIM // 2)
    return jnp.repeat(jnp.cos(ang), 2, axis=1), jnp.repeat(jnp.sin(ang), 2, axis=1) * sign


PROMPT_ROW_TILE = 512
PROMPT_MLP_PARTS = 8


def kernel(x_prompt, x_sample, state_hgrn, state_ret, cache_k, cache_v, page_table, c_prompt, c_sample,
           w_ada, b_ada, norm_w, w_in_ab, w_out_ab, hgrn_lb_logits, hgrn_norm_w, w_in_c, w_out_c,
           diff_lambda, diff_subln_w, w_mlp_up, w_mlp_down, final_norm_w):
    bp, tp, d = x_prompt.shape
    bs, ts, _ = x_sample.shape
    n_slots, n_phys, page, n_heads_c, _, dh = cache_k.shape

    mod = _ada(jnp.concatenate([c_prompt, c_sample], axis=0), w_ada, b_ada)
    depth = mod.shape[0]
    assert depth % 2 == 0, "layers come in (recurrent mixer, attention) pairs"
    mod_p = mod[:, :bp].reshape(depth, bp, 1, 6 * d)
    mod_s = jnp.repeat(mod[:, bp:], ts, axis=1).reshape(depth, 1, bs * ts, 6 * d)

    w_in_ab, w_out_ab, w_in_c, w_out_c, w_mlp_up, w_mlp_down = (
        t.astype(BF16) for t in (w_in_ab, w_out_ab, w_in_c, w_out_c, w_mlp_up, w_mlp_down))
    n_in = w_in_ab.shape[2]
    n_c = w_in_c.shape[2] // 3
    group = 2 * n_heads_c
    q_scale = dh ** -0.5 * LOG2_E

    past_len = page_table.shape[1] * page
    page_rows = page * group
    cache_k_rows = cache_k.reshape(n_slots * n_phys, page_rows, dh)
    cache_v_rows = _values_in_stored_order(
        cache_v.reshape(n_slots * n_phys, page, n_heads_c * 2 * dh), n_heads_c)
    cos_p, sin_p = _rope_tables(0, tp)
    cos_s, sin_s = _rope_tables(past_len, ts)

    xp = x_prompt.reshape(bp * tp, d)
    xs = x_sample.reshape(bs * ts, d)
    tm_p, tm_s = PROMPT_ROW_TILE, bs * ts
    hg_p, rt_p, k_p, v_p, hg_s, rt_s, k_s, v_s = ([] for _ in range(8))
    for l in range(0, depth, 2):
        e = a = l // 2
        lam_init = 0.8 - 0.6 * math.exp(-0.3 * (l + 1))

        (proj,) = _inproj(xs, mod_s[l], norm_w[l, 0], w_in_ab[e], [(n_in, F32, 1.0)],
                          tm=tm_s, shift_col=0, scale_col=1)
        o, s_a, s_b = _mixer_sample(proj.reshape(bs, ts, n_in), cos_s, sin_s, hgrn_lb_logits,
                                    hgrn_norm_w[e], state_hgrn[e], state_ret[e], layer_slot=e)
        hg_s.append(s_a)
        rt_s.append(s_b)
        xs = _mlp(o.reshape(bs * ts, -1), xs, mod_s[l], norm_w[l, 1], w_out_ab[e], w_mlp_up[l],
                  w_mlp_down[l], final_norm_w, tm=tm_s, final_norm=False, n_parts=1)
        q, _, _, k_rows, v_rows = _qkv(xs, mod_s[l + 1], norm_w[l + 1, 0], w_in_c[a], tm=tm_s,
                                       q_scale=q_scale, n_heads=n_heads_c)
        k_s.append(k_rows)
        v_s.append(v_rows)
        paged = dict(q_rows=_sample_query_rows(q.reshape(bs, ts, n_c), n_heads_c),
                     k_new=k_rows.reshape(bs, ts * group, dh), v_new=v_rows.reshape(bs, ts * group, dh),
                     cache_k=cache_k_rows, cache_v=cache_v_rows, page_table=page_table,
                     lam_p=diff_lambda[a], subln_w=diff_subln_w[a], page_base=a * n_phys, tl=ts,
                     n_heads=n_heads_c, lam_init=lam_init)
        n_lo = bs // 2

        (proj,) = _inproj(xp, mod_p[l], norm_w[l, 0], w_in_ab[e], [(n_in, F32, 1.0)],
                          tm=tm_p, shift_col=0, scale_col=1)
        o, s_a, s_b = _mixer_prompt(proj.reshape(bp, tp, n_in), cos_p, sin_p, hgrn_lb_logits,
                                    hgrn_norm_w[e], layer_slot=e, chunk=128, rows_per_step=min(4, bp))
        hg_p.append(s_a)
        rt_p.append(s_b)
        xp, attn_lo = _mlp(o.reshape(bp * tp, -1), xp, mod_p[l], norm_w[l, 1], w_out_ab[e], w_mlp_up[l],
                           w_mlp_down[l], final_norm_w, tm=tm_p, final_norm=False,
                           n_parts=PROMPT_MLP_PARTS, paged=dict(paged, first_seq=0, n_seq=n_lo))

        q, k, v, k_rows, v_rows = _qkv(xp, mod_p[l + 1], norm_w[l + 1, 0], w_in_c[a], tm=tm_p,
                                       q_scale=q_scale, n_heads=n_heads_c)
        k_p.append(k_rows)
        v_p.append(v_rows)
        q, k, v = (t.reshape(bp, tp, n_c) for t in (q, k, v))
        o = _attn_prompt(q, k, v, diff_lambda[a], diff_subln_w[a], lam_init=lam_init, tq=min(512, tp))
        xp, attn_hi = _mlp(o.reshape(bp * tp, -1), xp, mod_p[l + 1], norm_w[l + 1, 1], w_out_c[a],
                           w_mlp_up[l + 1], w_mlp_down[l + 1], final_norm_w, tm=tm_p,
                           final_norm=(l + 2 == depth), n_parts=PROMPT_MLP_PARTS,
                           paged=dict(paged, first_seq=n_lo, n_seq=bs - n_lo))

        o = jnp.concatenate([attn_lo, attn_hi], axis=0)
        o = jnp.swapaxes(o.reshape(bs, ts, 2, n_heads_c, dh)[:, :, :, ::-1, :], 2, 3)
        xs = _mlp(o.reshape(bs * ts, -1), xs, mod_s[l + 1], norm_w[l + 1, 1], w_out_c[a],
                  w_mlp_up[l + 1], w_mlp_down[l + 1], final_norm_w, tm=tm_s,
                  final_norm=(l + 2 == depth), n_parts=1)

    y_p = xp.reshape(bp, tp, d)
    y_s = xs.reshape(bs, ts, d)
    hg_p, rt_p, k_p, v_p, hg_s, rt_s, k_s, v_s = (
        jnp.stack(t) for t in (hg_p, rt_p, k_p, v_p, hg_s, rt_s, k_s, v_s))

    n_odd = k_p.shape[0]

    def values_from_stored_order(v, lead):
        v = jnp.swapaxes(v.reshape(n_odd, *lead, 2, n_heads_c, dh), -2, -3)
        return v.reshape(n_odd, *lead, n_heads_c, 2 * dh)

    k_p = k_p.reshape(n_odd, bp, tp // page, page, n_heads_c, 2, dh)
    v_p = values_from_stored_order(v_p, (bp, tp // page, page))
    k_s = k_s.reshape(n_odd, bs, ts, n_heads_c, 2, dh)
    v_s = values_from_stored_order(v_s, (bs, ts))
    return (y_p, y_s, hg_p, rt_p, k_p, v_p, hg_s, rt_s, k_s, v_s)
```

```python
import functools
import math

import jax
import jax.numpy as jnp
import numpy as np
from jax import lax
from jax.experimental import pallas as pl
from jax.experimental.pallas import tpu as pltpu

F32 = jnp.float32
BF16 = jnp.bfloat16

HEAD_DIM = 128
SUBLANES = 8
NORM_EPS = 1e-6
ROPE_BASE = 10000.0
V7X_VMEM_LIMIT = 52 * 1024 * 1024

_NT = (((1,), (1,)), ((), ()))
_TN = (((0,), (0,)), ((), ()))


def _params(*semantics):
    return pltpu.CompilerParams(dimension_semantics=semantics, vmem_limit_bytes=V7X_VMEM_LIMIT)


def _resident(shape):
    zeros = (0,) * len(shape)
    return pl.BlockSpec(shape, lambda *_: zeros, pipeline_mode=pl.Buffered(1))


def _sigmoid(x):
    return 0.5 + 0.5 * jnp.tanh(0.5 * x)


def _silu(x):
    half = 0.5 * x
    return half + half * jnp.tanh(half)


def _rms(x):
    return x * lax.rsqrt(jnp.mean(x * x, axis=-1, keepdims=True) + NORM_EPS)


def _mm(a, b):
    return jnp.dot(a, b, preferred_element_type=F32)


def _ada_kernel(c_ref, w_ref, b_ref, o_ref):
    s = _silu(c_ref[...]).astype(BF16)
    o_ref[0] = _mm(s, w_ref[0].astype(BF16)) + b_ref[0]


def _ada(c, w_ada, b_ada):
    depth, d, n = w_ada.shape
    r = c.shape[0]
    tn = n // 4
    return pl.pallas_call(
        _ada_kernel,
        grid=(depth, n // tn),
        in_specs=[pl.BlockSpec((r, d), lambda l, j: (0, 0)),
                  pl.BlockSpec((1, d, tn), lambda l, j: (l, 0, j)),
                  pl.BlockSpec((1, 1, tn), lambda l, j: (l, 0, j))],
        out_specs=pl.BlockSpec((1, r, tn), lambda l, j: (l, 0, j)),
        out_shape=jax.ShapeDtypeStruct((depth, r, n), F32),
        compiler_params=_params("arbitrary", "arbitrary"),
        name="ada_mod",
    )(c, w_ada, b_ada.reshape(depth, 1, n))


def _modulated_norm(x, nw_ref, mod_ref, shift_col, scale_col):
    d = x.shape[-1]
    shift = mod_ref[0, :, shift_col * d:(shift_col + 1) * d]
    scale = mod_ref[0, :, scale_col * d:(scale_col + 1) * d]
    return _rms(x) * nw_ref[...] * (1.0 + scale) + shift


def _inproj_kernel(x_ref, mod_ref, nw_ref, w_ref, *out_refs, shift_col, scale_col, out_scales, tc):
    h = _modulated_norm(x_ref[...], nw_ref, mod_ref, shift_col, scale_col).astype(BF16)
    off = 0
    for o_ref, out_scale in zip(out_refs, out_scales):
        width = o_ref.shape[-1]
        for c in range(0, width, tc):
            y = _mm(h, w_ref[:, off + c:off + c + tc])
            if out_scale != 1.0:
                y = y * out_scale
            o_ref[:, c:c + tc] = y.astype(o_ref.dtype)
        off += width


def _inproj(x, mod, nw, w, outs, *, tm, shift_col, scale_col):
    m, d = x.shape
    n = w.shape[1]
    rows_per_mod = m // mod.shape[0]
    tiles_per_mod = rows_per_mod // tm
    kern = functools.partial(_inproj_kernel, shift_col=shift_col, scale_col=scale_col,
                             out_scales=tuple(o[2] for o in outs), tc=512)
    return pl.pallas_call(
        kern,
        grid=(m // tm,),
        in_specs=[pl.BlockSpec((tm, d), lambda i: (i, 0)),
                  pl.BlockSpec((1, mod.shape[1], mod.shape[2]), lambda i: (i // tiles_per_mod, 0, 0)),
                  _resident((1, d)),
                  _resident((d, n))],
        out_specs=[pl.BlockSpec((tm, o[0]), lambda i: (i, 0)) for o in outs],
        out_shape=[jax.ShapeDtypeStruct((m, o[0]), o[1]) for o in outs],
        compiler_params=_params("parallel"),
        name="norm_mod_inproj",
    )(x, mod, nw.reshape(1, d), w)


def _qkv_kernel(x_ref, mod_ref, nw_ref, w_ref, q_ref, kb_ref, vb_ref, kr_ref, vr_ref, *,
                q_scale, n_heads, tc):
    h = _modulated_norm(x_ref[...], nw_ref, mod_ref, 0, 1).astype(BF16)
    tm, n_c = q_ref.shape
    lanes = HEAD_DIM
    group = 2 * n_heads
    for c in range(0, n_c, tc):
        q_ref[:, c:c + tc] = (_mm(h, w_ref[:, c:c + tc]) * q_scale).astype(q_ref.dtype)
    for c in range(0, n_c, tc):
        y = _mm(h, w_ref[:, n_c + c:n_c + c + tc])
        kb_ref[:, c:c + tc] = y.astype(kb_ref.dtype)
        for i in range(tc // lanes):
            kr_ref[pl.ds(c // lanes + i, tm, stride=group), :] = y[:, i * lanes:(i + 1) * lanes]
    for c in range(0, n_c, tc):
        y = _mm(h, w_ref[:, 2 * n_c + c:2 * n_c + c + tc])
        vb_ref[:, c:c + tc] = y.astype(vb_ref.dtype)
        for i in range(tc // lanes):
            head, half = divmod(c // lanes + i, 2)
            vr_ref[pl.ds(half * n_heads + head, tm, stride=group), :] = y[:, i * lanes:(i + 1) * lanes]


def _qkv(x, mod, nw, w, *, tm, q_scale, n_heads):
    m, d = x.shape
    n_c = w.shape[1] // 3
    group = 2 * n_heads
    tiles_per_mod = (m // mod.shape[0]) // tm
    kern = functools.partial(_qkv_kernel, q_scale=q_scale, n_heads=n_heads, tc=512)
    cols = pl.BlockSpec((tm, n_c), lambda i: (i, 0))
    rows = pl.BlockSpec((tm * group, HEAD_DIM), lambda i: (i, 0))
    return pl.pallas_call(
        kern,
        grid=(m // tm,),
        in_specs=[pl.BlockSpec((tm, d), lambda i: (i, 0)),
                  pl.BlockSpec((1, mod.shape[1], mod.shape[2]), lambda i: (i // tiles_per_mod, 0, 0)),
                  _resident((1, d)),
                  _resident((d, 3 * n_c))],
        out_specs=[cols, cols, cols, rows, rows],
        out_shape=[jax.ShapeDtypeStruct((m, n_c), BF16)] * 3
                  + [jax.ShapeDtypeStruct((m * group, HEAD_DIM), F32)] * 2,
        compiler_params=_params("parallel"),
        name="norm_mod_qkv",
    )(x, mod, nw.reshape(1, d), w)


N_MLP_INPUTS = 8
N_ATTENTION_INPUTS = 7


def _mlp_kernel(*refs, n_parts, final_norm, attention):
    if attention is None:
        o_ref, x_ref, mod_ref, nw_ref, wo_ref, wu_ref, wd_ref, fw_ref, out_ref, h_ref, acc_ref = refs
    else:
        pt_ref, refs = refs[0], refs[1:]
        o_ref, x_ref, mod_ref, nw_ref, wo_ref, wu_ref, wd_ref, fw_ref = refs[:N_MLP_INPUTS]
        attention_in = refs[N_MLP_INPUTS:N_MLP_INPUTS + N_ATTENTION_INPUTS]
        out_ref, attention_out, h_ref, acc_ref = refs[N_MLP_INPUTS + N_ATTENTION_INPUTS:][:4]
        attention_refs = (pt_ref, *attention_in, attention_out,
                          *refs[N_MLP_INPUTS + N_ATTENTION_INPUTS + 4:])
    d = x_ref.shape[-1]
    cw = wu_ref.shape[1] // n_parts

    def mod_col(col):
        return mod_ref[0, :, col * d:(col + 1) * d]

    hidden = [None] * n_parts

    def mlp_up(r):
        if r == 0:
            x1 = x_ref[...] + mod_col(2) * _mm(o_ref[...].astype(BF16), wo_ref[...])
            out_ref[...] = x1
            h_ref[...] = _modulated_norm(x1, nw_ref, mod_ref, 3, 4).astype(BF16)
        u = jnp.maximum(_mm(h_ref[...], wu_ref[:, r * cw:(r + 1) * cw]), 0.0)
        hidden[r] = (u * u).astype(BF16)

    def mlp_down(r):
        part = _mm(hidden[r], wd_ref[r * cw:(r + 1) * cw, :])
        acc_ref[...] = part if r == 0 else acc_ref[...] + part

    def finish():
        y = out_ref[...] + mod_col(5) * acc_ref[...]
        if final_norm:
            y = _rms(y) * fw_ref[...]
        out_ref[...] = y

    if attention is None:
        for r in range(n_parts):
            mlp_up(r)
            mlp_down(r)
        finish()
    else:
        aligned = attention["steps_per_seq"] % n_parts == 0
        mlp_up(0)
        for r in range(n_parts):
            last = r == n_parts - 1
            _paged_attention_step(pl.program_id(0) * n_parts + r, pl.num_programs(0) * n_parts,
                                  attention_refs,
                                  (functools.partial(mlp_down, r),
                                   finish if last else functools.partial(mlp_up, r + 1),
                                   lambda: None),
                                  may_open=(r == 0 or not aligned),
                                  may_close=(last or not aligned), **attention)


def _mlp(o, x, mod, nw, w_out, w_up, w_down, final_w, *, tm, final_norm, n_parts, paged=None):
    m, d = x.shape
    k = o.shape[1]
    dff = w_up.shape[1]
    n_tiles = m // tm
    tiles_per_mod = (m // mod.shape[0]) // tm
    in_specs = [pl.BlockSpec((tm, k), lambda i, *_: (i, 0)),
                pl.BlockSpec((tm, d), lambda i, *_: (i, 0)),
                pl.BlockSpec((1, mod.shape[1], mod.shape[2]), lambda i, *_: (i // tiles_per_mod, 0, 0)),
                _resident((1, d)),
                _resident((k, d)),
                _resident((d, dff)),
                _resident((dff, d)),
                _resident((1, d))]
    operands = [o, x, mod, nw.reshape(1, d), w_out, w_up, w_down, final_w.reshape(1, d)]
    out_specs = pl.BlockSpec((tm, d), lambda i, *_: (i, 0))
    out_shape = jax.ShapeDtypeStruct((m, d), F32)
    scratch = [pltpu.VMEM((tm, d), BF16), pltpu.VMEM((tm, d), F32)]
    if paged is None:
        kern = functools.partial(_mlp_kernel, n_parts=n_parts, final_norm=final_norm, attention=None)
        return pl.pallas_call(
            kern, grid=(n_tiles,), in_specs=in_specs, out_specs=out_specs, out_shape=out_shape,
            scratch_shapes=scratch, compiler_params=_params("arbitrary"), name="outproj_mlp_residual",
        )(*operands)

    q_rows, k_new, v_new = paged["q_rows"], paged["k_new"], paged["v_new"]
    cache_k, cache_v, page_table = paged["cache_k"], paged["cache_v"], paged["page_table"]
    n_seq = paged["n_seq"]
    n_rows, dh = q_rows.shape[1:]
    n_pages = page_table.shape[1]
    page_rows = cache_k.shape[1]
    assert 2 * paged["n_heads"] == SUBLANES, "one (map, token) block of query rows must fill one vreg"
    pages_per_step = (n_seq * n_pages) // (n_tiles * n_parts)
    assert pages_per_step * n_tiles * n_parts == n_seq * n_pages and n_pages % pages_per_step == 0
    attention = dict(pages_per_step=pages_per_step, page_base=paged["page_base"],
                     steps_per_seq=n_pages // pages_per_step, first_seq=paged["first_seq"],
                     tl=paged["tl"], n_heads=paged["n_heads"], lam_init=paged["lam_init"])
    kern = functools.partial(_mlp_kernel, n_parts=n_parts, final_norm=final_norm, attention=attention)
    ring = (PAGE_RING_DEPTH, pages_per_step, page_rows, dh)
    grid_spec = pltpu.PrefetchScalarGridSpec(
        num_scalar_prefetch=1,
        grid=(n_tiles,),
        in_specs=in_specs + [_resident(paged["lam_p"].shape), _resident((1, 2 * dh)),
                             _resident(q_rows.shape), _resident(k_new.shape), _resident(v_new.shape),
                             pl.BlockSpec(memory_space=pl.ANY), pl.BlockSpec(memory_space=pl.ANY)],
        out_specs=[out_specs, pl.BlockSpec((n_seq, n_rows // 2, dh), lambda i, *_: (0, 0, 0))],
        scratch_shapes=scratch + [pltpu.VMEM((n_rows, dh), F32), pltpu.VMEM((n_rows, dh), F32),
                                  pltpu.VMEM((n_rows, dh), F32),
                                  pltpu.VMEM((dh, dh), F32), pltpu.VMEM((dh, dh), F32),
                                  pltpu.VMEM(ring, F32), pltpu.VMEM(ring, F32),
                                  pltpu.SemaphoreType.DMA((2, PAGE_RING_DEPTH))],
    )
    return pl.pallas_call(
        kern,
        grid_spec=grid_spec,
        out_shape=[out_shape, jax.ShapeDtypeStruct((n_seq, n_rows // 2, dh), F32)],
        compiler_params=_params("arbitrary"),
        name="outproj_mlp_paged_attention",
    )(page_table.reshape(-1), *operands, paged["lam_p"], paged["subln_w"].reshape(1, 2 * dh),
      q_rows, k_new, v_new, cache_k, cache_v)


def _hgrn_lower_bound(logits_ref, layer_slot):
    lg = logits_ref[...]
    ex = jnp.exp(lg - jnp.max(lg, axis=0, keepdims=True))
    pr = ex / jnp.sum(ex, axis=0, keepdims=True)
    return jnp.sum(pr[:layer_slot + 1], axis=0, keepdims=True)


def _rotate_pairs(x, cos, sin_signed):
    n = x.shape[-1]
    lane = lax.broadcasted_iota(jnp.int32, x.shape, 1)
    swapped = jnp.where((lane & 1) == 0, pltpu.roll(x, n - 1, 1), pltpu.roll(x, 1, 1))
    return x * cos + swapped * sin_signed


def _cumsum_rows(x):
    n = x.shape[0]
    row = lax.broadcasted_iota(jnp.int32, x.shape, 0)
    d = 1
    while d < n:
        x = x + jnp.where(row >= d, pltpu.roll(x, d, 0), 0.0)
        d *= 2
    return x


def _split_row(g, b):
    n, lanes = g.shape
    if 2 * b >= SUBLANES:
        g3 = g.reshape(n // (2 * b), 2 * b, lanes)
        return jnp.broadcast_to(g3[:, b - 1:b, :], g3.shape).reshape(n, lanes)
    rmod = lax.broadcasted_iota(jnp.int32, g.shape, 0) & (2 * b - 1)
    out = g
    for r in range(2 * b):
        if r != b - 1:
            out = jnp.where(rmod == r, pltpu.roll(g, (r - (b - 1)) % n, 0), out)
    return out


def _level_matrix(n):
    t = np.arange(n)[:, None]
    s = np.arange(n)[None, :]
    hi = np.floor(np.log2(np.maximum(t ^ s, 1))).astype(np.int32)
    return jnp.asarray(np.where(s < t, hi, -1).astype(np.int32))


def _hgrn_chunk(q, k, v, g2, lv, st):
    n = q.shape[0]
    row = lax.broadcasted_iota(jnp.int32, q.shape, 0)
    ri = lax.broadcasted_iota(jnp.int32, (n, n), 0)
    ci = lax.broadcasted_iota(jnp.int32, (n, n), 1)
    a = jnp.where(ri == ci, jnp.sum(q * k, axis=-1, keepdims=True), 0.0)
    b, level = 1, 0
    while b < n:
        if b >= SUBLANES:
            parts = []
            for lo in range(0, n, 2 * b):
                edge = g2[lo + b - 1:lo + b, :]
                parts.append(k[lo:lo + b] * jnp.exp2(edge - g2[lo:lo + b]))
                parts.append(q[lo + b:lo + 2 * b] * jnp.exp2(g2[lo + b:lo + 2 * b] - edge))
            z = jnp.concatenate(parts, axis=0).astype(BF16)
        else:
            is_query = (row & b) != 0
            sign = jnp.where(is_query, 1.0, -1.0)
            z = (jnp.where(is_query, q, k) * jnp.exp2((g2 - _split_row(g2, b)) * sign)).astype(BF16)
        a = jnp.where(lv == level, lax.dot_general(z, z, _NT, preferred_element_type=F32), a)
        b, level = 2 * b, level + 1
    carried = lax.dot_general((q * jnp.exp2(g2)).astype(BF16), st.astype(BF16), _NT,
                              preferred_element_type=F32)
    o = _mm(a.astype(BF16), v.astype(BF16)) + carried
    g_end = g2[n - 1:n, :]
    k_dec = (k * jnp.exp2(g_end - g2)).astype(BF16)
    st_new = st * jnp.exp2(g_end) + lax.dot_general(v.astype(BF16), k_dec, _TN,
                                                    preferred_element_type=F32)
    return o, st_new


def _retention_chunk(q, k, v, log_gamma, st):
    n = q.shape[0]
    rowf = lax.broadcasted_iota(jnp.int32, q.shape, 0).astype(F32)
    ri = lax.broadcasted_iota(jnp.int32, (n, n), 0)
    ci = lax.broadcasted_iota(jnp.int32, (n, n), 1)
    decay = jnp.where(ci <= ri, jnp.exp((ri - ci).astype(F32) * log_gamma), 0.0)
    a = lax.dot_general(q.astype(BF16), k.astype(BF16), _NT, preferred_element_type=F32) * decay
    q_dec = (q * jnp.exp((rowf + 1.0) * log_gamma)).astype(BF16)
    o = _mm(a.astype(BF16), v.astype(BF16)) + _mm(q_dec, st.astype(BF16))
    k_dec = (k * jnp.exp((n - 1.0 - rowf) * log_gamma)).astype(BF16)
    st_new = st * math.exp(n * log_gamma) + lax.dot_general(k_dec, v.astype(BF16), _TN,
                                                            preferred_element_type=F32)
    return o, st_new


def _retention_log_gamma(h):
    return math.log(1.0 - 2.0 ** (-5.0 - h))


def _mixer_prompt_kernel(proj_ref, cos_ref, sin_ref, lv_ref, lbl_ref, na_ref,
                         o_ref, sa_out, sb_out, sat_ref, sb_ref, *, layer_slot, n_heads):
    t = pl.program_id(1)
    hd = HEAD_DIM
    w = n_heads * hd

    @pl.when(t == 0)
    def _():
        sat_ref[...] = jnp.zeros_like(sat_ref)
        sb_ref[...] = jnp.zeros_like(sb_ref)

    lb = _hgrn_lower_bound(lbl_ref, layer_slot)
    lv = lv_ref[...]
    cos = cos_ref[...]
    sin = sin_ref[...]
    for r in range(proj_ref.shape[0]):
        def cols(group, h):
            return proj_ref[r, :, group * w + h * hd:group * w + (h + 1) * hd]

        fz = lb + (1.0 - lb) * _sigmoid(proj_ref[r, :, w:2 * w])
        g_all = _cumsum_rows(jnp.log2(fz))
        k_all = 1.0 - fz
        for h in range(n_heads):
            sl = slice(h * hd, (h + 1) * hd)
            q = _silu(cols(0, h)) * hd ** -0.5
            o, st = _hgrn_chunk(q, k_all[:, sl], cols(2, h), g_all[:, sl], lv, sat_ref[r, h])
            sat_ref[r, h] = st
            o_ref[r, :, sl] = (_rms(o) * na_ref[...] * _silu(cols(3, h))).astype(o_ref.dtype)
        for h in range(n_heads):
            q = _rotate_pairs(cols(4, h), cos, sin)
            k = _rotate_pairs(cols(5, h), cos, sin) * hd ** -0.5
            o, st = _retention_chunk(q, k, cols(6, h), _retention_log_gamma(h), sb_ref[r, h])
            sb_ref[r, h] = st
            o_ref[r, :, w + h * hd:w + (h + 1) * hd] = (_rms(o) * _silu(cols(7, h))).astype(o_ref.dtype)

    @pl.when(t == pl.num_programs(1) - 1)
    def _():
        for r in range(proj_ref.shape[0]):
            for h in range(n_heads):
                sa_out[r, h] = sat_ref[r, h].T
                sb_out[r, h] = sb_ref[r, h]


def _mixer_prompt(proj, cos, sin, lb_logits, norm_a, *, layer_slot, chunk, rows_per_step):
    bsz, tlen, n = proj.shape
    hd = HEAD_DIM
    n_heads = n // (8 * hd)
    rb = rows_per_step
    kern = functools.partial(_mixer_prompt_kernel, layer_slot=layer_slot, n_heads=n_heads)
    state = jax.ShapeDtypeStruct((bsz, n_heads, hd, hd), F32)
    state_spec = pl.BlockSpec((rb, n_heads, hd, hd), lambda b, t: (b, 0, 0, 0))
    return pl.pallas_call(
        kern,
        grid=(bsz // rb, tlen // chunk),
        in_specs=[pl.BlockSpec((rb, chunk, n), lambda b, t: (b, t, 0)),
                  pl.BlockSpec((chunk, hd), lambda b, t: (t, 0)),
                  pl.BlockSpec((chunk, hd), lambda b, t: (t, 0)),
                  pl.BlockSpec((chunk, chunk), lambda b, t: (0, 0)),
                  pl.BlockSpec(lb_logits.shape, lambda b, t: (0, 0)),
                  pl.BlockSpec((1, hd), lambda b, t: (0, 0))],
        out_specs=[pl.BlockSpec((rb, chunk, 2 * n_heads * hd), lambda b, t: (b, t, 0)),
                   state_spec, state_spec],
        out_shape=[jax.ShapeDtypeStruct((bsz, tlen, 2 * n_heads * hd), BF16), state, state],
        scratch_shapes=[pltpu.VMEM((rb, n_heads, hd, hd), F32), pltpu.VMEM((rb, n_heads, hd, hd), F32)],
        compiler_params=_params("arbitrary", "arbitrary"),
        name="mixer_prompt",
    )(proj, cos, sin, _level_matrix(chunk), lb_logits, norm_a.reshape(1, hd))


def _pair_chunk(q, k, v, g, s0, dec, kpad_ref, vpad_ref):
    tl = q.shape[0]
    row = lax.broadcasted_iota(jnp.int32, q.shape, 0)
    o = _mm((q * jnp.exp(g)).astype(BF16), s0.astype(BF16))
    for s in range(tl):
        live = row >= s
        w = jnp.exp(jnp.where(live, g - g[s:s + 1, :], 0.0))
        a = jnp.sum(jnp.where(live, q * w * k[s:s + 1, :], 0.0), axis=-1, keepdims=True)
        o = o + a * v[s:s + 1, :]
    kpad_ref[0:tl, :] = k * jnp.exp(g[tl - 1:tl, :] - g)
    vpad_ref[0:tl, :] = v
    upd = lax.dot_general(kpad_ref[...].astype(BF16), vpad_ref[...].astype(BF16), _TN,
                          preferred_element_type=F32)
    return o, s0 * dec + upd


def _mixer_sample_kernel(proj_ref, cos_ref, sin_ref, lbl_ref, na_ref, sa_ref, sb_ref,
                         o_ref, sa_out, sb_out, kpad_ref, vpad_ref, *, layer_slot, n_heads):
    hd = HEAD_DIM
    w = n_heads * hd
    tl = proj_ref.shape[1]
    kpad_ref[...] = jnp.zeros_like(kpad_ref)
    vpad_ref[...] = jnp.zeros_like(vpad_ref)

    def cols(group, h):
        return proj_ref[0, :, group * w + h * hd:group * w + (h + 1) * hd]

    lb = _hgrn_lower_bound(lbl_ref, layer_slot)
    fz = lb + (1.0 - lb) * _sigmoid(proj_ref[0, :, w:2 * w])
    lf = jnp.log(fz)
    row = lax.broadcasted_iota(jnp.int32, lf.shape, 0)
    g_all = jnp.zeros_like(lf)
    for j in range(tl):
        g_all = g_all + jnp.where(row >= j, lf[j:j + 1, :], 0.0)
    k_all = 1.0 - fz
    for h in range(n_heads):
        sl = slice(h * hd, (h + 1) * hd)
        g = g_all[:, sl]
        q = _silu(cols(0, h)) * hd ** -0.5
        dec = jnp.broadcast_to(jnp.exp(g[tl - 1:tl, :]), (hd, hd)).T
        o, st = _pair_chunk(q, k_all[:, sl], cols(2, h), g, sa_ref[0, h], dec, kpad_ref, vpad_ref)
        sa_out[0, h] = st
        o_ref[0, :, sl] = _rms(o) * na_ref[...] * _silu(cols(3, h))

    cos = cos_ref[...]
    sin = sin_ref[...]
    rowf = lax.broadcasted_iota(jnp.int32, (tl, hd), 0).astype(F32)
    for h in range(n_heads):
        lg = _retention_log_gamma(h)
        q = _rotate_pairs(cols(4, h), cos, sin)
        k = _rotate_pairs(cols(5, h), cos, sin) * hd ** -0.5
        o, st = _pair_chunk(q, k, cols(6, h), (rowf + 1.0) * lg, sb_ref[0, h], math.exp(tl * lg),
                            kpad_ref, vpad_ref)
        sb_out[0, h] = st
        o_ref[0, :, w + h * hd:w + (h + 1) * hd] = _rms(o) * _silu(cols(7, h))


def _mixer_sample(proj, cos, sin, lb_logits, norm_a, state_a, state_b, *, layer_slot):
    bsz, tl, n = proj.shape
    hd = HEAD_DIM
    n_heads = n // (8 * hd)
    kern = functools.partial(_mixer_sample_kernel, layer_slot=layer_slot, n_heads=n_heads)
    state = jax.ShapeDtypeStruct((bsz, n_heads, hd, hd), F32)
    state_spec = pl.BlockSpec((1, n_heads, hd, hd), lambda b: (b, 0, 0, 0))
    return pl.pallas_call(
        kern,
        grid=(bsz,),
        in_specs=[pl.BlockSpec((1, tl, n), lambda b: (b, 0, 0)),
                  pl.BlockSpec((tl, hd), lambda b: (0, 0)),
                  pl.BlockSpec((tl, hd), lambda b: (0, 0)),
                  pl.BlockSpec(lb_logits.shape, lambda b: (0, 0)),
                  pl.BlockSpec((1, hd), lambda b: (0, 0)),
                  state_spec, state_spec],
        out_specs=[pl.BlockSpec((1, tl, 2 * n_heads * hd), lambda b: (b, 0, 0)), state_spec, state_spec],
        out_shape=[jax.ShapeDtypeStruct((bsz, tl, 2 * n_heads * hd), F32), state, state],
        scratch_shapes=[pltpu.VMEM((hd, hd), F32), pltpu.VMEM((hd, hd), F32)],
        compiler_params=_params("arbitrary"),
        name="mixer_sample",
    )(proj, cos, sin, lb_logits, norm_a.reshape(1, hd), state_a, state_b)


def _diff_lambda(lam_ref, lam_init):
    lp = lam_ref[...]
    return (jnp.exp(jnp.sum(lp[0:1] * lp[1:2], axis=-1, keepdims=True))
            - jnp.exp(jnp.sum(lp[2:3] * lp[3:4], axis=-1, keepdims=True)) + lam_init)


LOG2_E = math.log2(math.e)


def _attn_prompt_kernel(lam_ref, sw_ref, q_ref, k_ref, v_ref, o_ref, m_ref, l_ref, acc_ref, *,
                        tq, lam_init):
    dh = HEAD_DIM
    lanes = m_ref.shape[-1]
    n_tiles = q_ref.shape[1] // tq
    m_ref[...] = jnp.full_like(m_ref, -jnp.inf)
    l_ref[...] = jnp.zeros_like(l_ref)
    acc_ref[...] = jnp.zeros_like(acc_ref)

    def tile_pair(q0, nq, k0, nk):
        rq = slice(q0, q0 + nq)
        rk = slice(k0, k0 + nk)
        v = v_ref[0, rk, :]
        scores = []
        for j in range(2):
            s = lax.dot_general(q_ref[0, rq, j * dh:(j + 1) * dh], k_ref[0, rk, j * dh:(j + 1) * dh],
                                _NT, preferred_element_type=F32)
            if k0 + nk - 1 > q0:
                ri = lax.broadcasted_iota(jnp.int32, s.shape, 0)
                ci = lax.broadcasted_iota(jnp.int32, s.shape, 1)
                s = jnp.where(ci - ri <= q0 - k0, s, -jnp.inf)
            scores.append(s)
        probs, alphas = [], []
        for j in range(2):
            s = scores[j]
            m_old = m_ref[j, rq]
            m_new = jnp.maximum(m_old, jnp.max(s, axis=-1, keepdims=True))
            alpha = jnp.exp2(m_old - m_new)
            l_new = alpha * l_ref[j, rq]
            tiles = []
            for c in range(0, s.shape[1], lanes):
                p = jnp.exp2(s[:, c:c + lanes] - m_new)
                l_new = l_new + p
                tiles.append(p.astype(BF16))
            l_ref[j, rq] = l_new
            m_ref[j, rq] = m_new
            probs.append(jnp.concatenate(tiles, axis=1))
            alphas.append(alpha)
        for j in range(2):
            pv = _mm(probs[j], v)
            for c in range(0, pv.shape[1], lanes):
                acc_ref[j, rq, c:c + lanes] = (alphas[j] * acc_ref[j, rq, c:c + lanes]
                                               + pv[:, c:c + lanes])

    for kj in range(n_tiles):
        for qi in range(kj, n_tiles):
            tile_pair(qi * tq, tq, kj * tq, tq)

    lam = _diff_lambda(lam_ref, lam_init)
    for qi in range(n_tiles):
        rq = slice(qi * tq, (qi + 1) * tq)
        l0 = jnp.sum(l_ref[0, rq], axis=-1, keepdims=True)
        l1 = jnp.sum(l_ref[1, rq], axis=-1, keepdims=True)
        o = acc_ref[0, rq] / l0 - lam * (acc_ref[1, rq] / l1)
        o_ref[0, rq] = (_rms(o) * sw_ref[...] * (1.0 - lam_init)).astype(o_ref.dtype)


def _attn_prompt(q, k, v, lam_p, subln_w, *, lam_init, tq):
    bsz, tlen, n = q.shape
    e = 2 * HEAD_DIM
    n_heads = n // e
    kern = functools.partial(_attn_prompt_kernel, tq=tq, lam_init=lam_init)
    head_cols = pl.BlockSpec((1, tlen, e), lambda b, h: (b, 0, h))
    return pl.pallas_call(
        kern,
        grid=(bsz, n_heads),
        in_specs=[pl.BlockSpec(lam_p.shape, lambda b, h: (0, 0)),
                  pl.BlockSpec((1, e), lambda b, h: (0, 0)),
                  head_cols, head_cols, head_cols],
        out_specs=head_cols,
        out_shape=jax.ShapeDtypeStruct((bsz, tlen, n), BF16),
        scratch_shapes=[pltpu.VMEM((2, tlen, HEAD_DIM), F32), pltpu.VMEM((2, tlen, HEAD_DIM), F32),
                        pltpu.VMEM((2, tlen, e), F32)],
        compiler_params=_params("parallel", "parallel"),
        name="diff_attn_prompt",
    )(lam_p, subln_w.reshape(1, e), q, k, v)


def _sample_row_fields(shape, tl, n_heads):
    r = lax.broadcasted_iota(jnp.int32, shape, 0)
    head = n_heads - 1 - r % n_heads
    half = (r // n_heads) % 2
    token = (r // (2 * n_heads)) % tl
    qmap = r // (2 * n_heads * tl)
    return head, qmap, token, half


PAGE_RING_DEPTH = 3


def _paged_attention_step(step, n_steps, refs, overlapped_work, *, pages_per_step, page_base,
                          steps_per_seq, first_seq, tl, n_heads, lam_init, may_open, may_close):
    (pt_ref, lam_ref, sw_ref, q_ref, kn_ref, vn_ref, ck_hbm, cv_hbm, o_ref,
     m_ref, l_ref, acc_ref, kpad_ref, vpad_ref, kbuf, vbuf, sems) = refs
    seq = step // steps_per_seq
    p = step % steps_per_seq
    group = 2 * n_heads
    n_rows = q_ref.shape[1]
    rows_per_map = n_rows // 2

    def page_copies(of_step, slot):
        out = []
        for i in range(pages_per_step):
            page = page_base + pt_ref[first_seq * steps_per_seq * pages_per_step
                                      + of_step * pages_per_step + i]
            out.append(pltpu.make_async_copy(ck_hbm.at[page], kbuf.at[slot, i], sems.at[0, slot]))
            out.append(pltpu.make_async_copy(cv_hbm.at[page], vbuf.at[slot, i], sems.at[1, slot]))
        return out

    def start_step(of_step):
        for cp in page_copies(of_step, of_step % PAGE_RING_DEPTH):
            cp.start()

    ahead = PAGE_RING_DEPTH - 1

    @pl.when(step == 0)
    def _():
        for s in range(ahead):
            @pl.when(s < n_steps)
            def _():
                start_step(s)

    @pl.when(step + ahead < n_steps)
    def _():
        start_step(step + ahead)

    slot = step % PAGE_RING_DEPTH
    for cp in page_copies(step, slot):
        cp.wait()

    if may_open:
        @pl.when(p == 0)
        def _():
            m_ref[...] = jnp.full_like(m_ref, -jnp.inf)
            l_ref[...] = jnp.zeros_like(l_ref)
            acc_ref[...] = jnp.zeros_like(acc_ref)

    q = q_ref[first_seq + seq]

    def own_lanes(n_keys, causal):
        head, qmap, token, _ = _sample_row_fields((n_rows, n_keys), tl, n_heads)
        key = lax.broadcasted_iota(jnp.int32, (n_rows, n_keys), 1)
        own = (key % group) == 2 * head + qmap
        return jnp.logical_and(own, key // group <= token) if causal else own

    def partial_softmax(pages, causal, hooks=(None, None)):
        before_scores, after_scores = hooks
        own = own_lanes(pages[0][0].shape[0], causal)
        keys = [k_rows.astype(BF16) for k_rows, _ in pages]
        if before_scores is not None:
            before_scores()
        scores = [lax.dot_general(q, k, _NT, preferred_element_type=F32) for k in keys]
        if after_scores is not None:
            after_scores()
        stats, probs = [], []
        for s in scores:
            s = jnp.where(own, s, -jnp.inf)
            m = jnp.max(s, axis=-1, keepdims=True)
            pexp = jnp.exp2(s - m)
            stats.append((m, jnp.sum(pexp, axis=-1, keepdims=True)))
            aligned = []
            for blk in range(n_rows // SUBLANES):
                shift = group - (n_heads - 1) - (blk * SUBLANES) // rows_per_map
                aligned.append(pltpu.roll(pexp[blk * SUBLANES:(blk + 1) * SUBLANES, :], shift, 1,
                                          stride=1, stride_axis=0))
            probs.append(jnp.concatenate(aligned, axis=0).astype(BF16))
        return [(m, l, _mm(p_rows, pltpu.roll(v_rows, group, 0).astype(BF16)))
                for (m, l), p_rows, (_, v_rows) in zip(stats, probs, pages)]

    def merge(parts):
        m_run = m_ref[...]
        m_new = m_run
        for m, _, _ in parts:
            m_new = jnp.maximum(m_new, m)
        alpha = jnp.exp2(m_run - m_new)
        l_new = alpha * l_ref[...]
        acc_new = alpha * acc_ref[...]
        for m, l, o in parts:
            wgt = jnp.exp2(m - m_new)
            l_new = l_new + wgt * l
            acc_new = acc_new + wgt * o
        m_ref[...] = m_new
        l_ref[...] = l_new
        acc_ref[...] = acc_new

    before_scores, after_scores, after_values = overlapped_work
    parts = partial_softmax([(kbuf[slot, i], vbuf[slot, i]) for i in range(pages_per_step)], False,
                            (before_scores, after_scores))
    after_values()
    merge(parts)

    if may_close:
        @pl.when(p == steps_per_seq - 1)
        def _():
            n_new = kn_ref.shape[1]
            kpad_ref[...] = jnp.zeros_like(kpad_ref)
            vpad_ref[...] = jnp.zeros_like(vpad_ref)
            kpad_ref[0:n_new, :] = kn_ref[first_seq + seq]
            vpad_ref[0:n_new, :] = vn_ref[first_seq + seq]
            merge(partial_softmax([(kpad_ref[...], vpad_ref[...])], True))

            o = (acc_ref[0:rows_per_map, :] / l_ref[0:rows_per_map, :]
                 - _diff_lambda(lam_ref, lam_init) * (acc_ref[rows_per_map:, :] / l_ref[rows_per_map:, :]))
            _, _, _, half = _sample_row_fields(o.shape, tl, n_heads)
            ss = jnp.broadcast_to(jnp.sum(o * o, axis=-1, keepdims=True), o.shape)
            other = jnp.where(half == 0, pltpu.roll(ss, rows_per_map - n_heads, 0),
                              pltpu.roll(ss, n_heads, 0))
            inv = lax.rsqrt((ss + other) / (2 * HEAD_DIM) + NORM_EPS)
            sw = jnp.where(half == 0, sw_ref[:, 0:HEAD_DIM], sw_ref[:, HEAD_DIM:])
            o_ref[seq] = o * inv * sw * (1.0 - lam_init)


def _sample_query_rows(q, n_heads):
    bsz, tl, _ = q.shape
    r = np.arange(4 * n_heads * tl)
    head = n_heads - 1 - r % n_heads
    token = (r // (2 * n_heads)) % tl
    qmap = r // (2 * n_heads * tl)
    return q.reshape(bsz, tl, n_heads, 2, HEAD_DIM)[:, token, head, qmap, :]


def _values_in_stored_order(v, n_heads):
    lead = v.shape[:-2]
    tokens = v.shape[-2]
    v = v.reshape(*lead, tokens, n_heads, 2, HEAD_DIM)
    return jnp.swapaxes(v, -2, -3).reshape(*lead, tokens * 2 * n_heads, HEAD_DIM)


def _rope_tables(pos0, tlen):
    inv = 1.0 / (ROPE_BASE ** jnp.linspace(0.0, 1.0, HEAD_DIM // 2, dtype=F32))
    ang = (pos0 + jnp.arange(tlen)).astype(F32)[:, None] * inv[None, :]
    sign = jnp.tile(jnp.asarray([-1.0, 1.0], F32), HEAD_DIM // 2)
    return jnp.repeat(jnp.cos(ang), 2, axis=1), jnp.repeat(jnp.sin(ang), 2, axis=1) * sign


PROMPT_ROW_TILE = 512
PROMPT_MLP_PARTS = 8


def kernel(x_prompt, x_sample, state_hgrn, state_ret, cache_k, cache_v, page_table, c_prompt, c_sample,
           w_ada, b_ada, norm_w, w_in_ab, w_out_ab, hgrn_lb_logits, hgrn_norm_w, w_in_c, w_out_c,
           diff_lambda, diff_subln_w, w_mlp_up, w_mlp_down, final_norm_w):
    bp, tp, d = x_prompt.shape
    bs, ts, _ = x_sample.shape
    n_slots, n_phys, page, n_heads_c, _, dh = cache_k.shape

    mod = _ada(jnp.concatenate([c_prompt, c_sample], axis=0), w_ada, b_ada)
    depth = mod.shape[0]
    assert depth % 2 == 0, "layers come in (recurrent mixer, attention) pairs"
    mod_p = mod[:, :bp].reshape(depth, bp, 1, 6 * d)
    mod_s = jnp.repeat(mod[:, bp:], ts, axis=1).reshape(depth, 1, bs * ts, 6 * d)

    w_in_ab, w_out_ab, w_in_c, w_out_c, w_mlp_up, w_mlp_down = (
        t.astype(BF16) for t in (w_in_ab, w_out_ab, w_in_c, w_out_c, w_mlp_up, w_mlp_down))
    n_in = w_in_ab.shape[2]
    n_c = w_in_c.shape[2] // 3
    group = 2 * n_heads_c
    q_scale = dh ** -0.5 * LOG2_E

    past_len = page_table.shape[1] * page
    page_rows = page * group
    cache_k_rows = cache_k.reshape(n_slots * n_phys, page_rows, dh)
    cache_v_rows = _values_in_stored_order(
        cache_v.reshape(n_slots * n_phys, page, n_heads_c * 2 * dh), n_heads_c)
    cos_p, sin_p = _rope_tables(0, tp)
    cos_s, sin_s = _rope_tables(past_len, ts)

    xp = x_prompt.reshape(bp * tp, d)
    xs = x_sample.reshape(bs * ts, d)
    tm_p, tm_s = PROMPT_ROW_TILE, bs * ts
    hg_p, rt_p, k_p, v_p, hg_s, rt_s, k_s, v_s = ([] for _ in range(8))
    for l in range(0, depth, 2):
        e = a = l // 2
        lam_init = 0.8 - 0.6 * math.exp(-0.3 * (l + 1))

        (proj,) = _inproj(xs, mod_s[l], norm_w[l, 0], w_in_ab[e], [(n_in, F32, 1.0)],
                          tm=tm_s, shift_col=0, scale_col=1)
        o, s_a, s_b = _mixer_sample(proj.reshape(bs, ts, n_in), cos_s, sin_s, hgrn_lb_logits,
                                    hgrn_norm_w[e], state_hgrn[e], state_ret[e], layer_slot=e)
        hg_s.append(s_a)
        rt_s.append(s_b)
        xs = _mlp(o.reshape(bs * ts, -1), xs, mod_s[l], norm_w[l, 1], w_out_ab[e], w_mlp_up[l],
                  w_mlp_down[l], final_norm_w, tm=tm_s, final_norm=False, n_parts=1)
        q, _, _, k_rows, v_rows = _qkv(xs, mod_s[l + 1], norm_w[l + 1, 0], w_in_c[a], tm=tm_s,
                                       q_scale=q_scale, n_heads=n_heads_c)
        k_s.append(k_rows)
        v_s.append(v_rows)
        paged = dict(q_rows=_sample_query_rows(q.reshape(bs, ts, n_c), n_heads_c),
                     k_new=k_rows.reshape(bs, ts * group, dh), v_new=v_rows.reshape(bs, ts * group, dh),
                     cache_k=cache_k_rows, cache_v=cache_v_rows, page_table=page_table,
                     lam_p=diff_lambda[a], subln_w=diff_subln_w[a], page_base=a * n_phys, tl=ts,
                     n_heads=n_heads_c, lam_init=lam_init)
        n_lo = bs // 2

        (proj,) = _inproj(xp, mod_p[l], norm_w[l, 0], w_in_ab[e], [(n_in, F32, 1.0)],
                          tm=tm_p, shift_col=0, scale_col=1)
        o, s_a, s_b = _mixer_prompt(proj.reshape(bp, tp, n_in), cos_p, sin_p, hgrn_lb_logits,
                                    hgrn_norm_w[e], layer_slot=e, chunk=128, rows_per_step=min(4, bp))
        hg_p.append(s_a)
        rt_p.append(s_b)
        xp, attn_lo = _mlp(o.reshape(bp * tp, -1), xp, mod_p[l], norm_w[l, 1], w_out_ab[e], w_mlp_up[l],
                           w_mlp_down[l], final_norm_w, tm=tm_p, final_norm=False,
                           n_parts=PROMPT_MLP_PARTS, paged=dict(paged, first_seq=0, n_seq=n_lo))

        q, k, v, k_rows, v_rows = _qkv(xp, mod_p[l + 1], norm_w[l + 1, 0], w_in_c[a], tm=tm_p,
                                       q_scale=q_scale, n_heads=n_heads_c)
        k_p.append(k_rows)
        v_p.append(v_rows)
        q, k, v = (t.reshape(bp, tp, n_c) for t in (q, k, v))
        o = _attn_prompt(q, k, v, diff_lambda[a], diff_subln_w[a], lam_init=lam_init, tq=min(512, tp))
        xp, attn_hi = _mlp(o.reshape(bp * tp, -1), xp, mod_p[l + 1], norm_w[l + 1, 1], w_out_c[a],
                           w_mlp_up[l + 1], w_mlp_down[l + 1], final_norm_w, tm=tm_p,
                           final_norm=(l + 2 == depth), n_parts=PROMPT_MLP_PARTS,
                           paged=dict(paged, first_seq=n_lo, n_seq=bs - n_lo))

        o = jnp.concatenate([attn_lo, attn_hi], axis=0)
        o = jnp.swapaxes(o.reshape(bs, ts, 2, n_heads_c, dh)[:, :, :, ::-1, :], 2, 3)
        xs = _mlp(o.reshape(bs * ts, -1), xs, mod_s[l + 1], norm_w[l + 1, 1], w_out_c[a],
                  w_mlp_up[l + 1], w_mlp_down[l + 1], final_norm_w, tm=tm_s,
                  final_norm=(l + 2 == depth), n_parts=1)

    y_p = xp.reshape(bp, tp, d)
    y_s = xs.reshape(bs, ts, d)
    hg_p, rt_p, k_p, v_p, hg_s, rt_s, k_s, v_s = (
        jnp.stack(t) for t in (hg_p, rt_p, k_p, v_p, hg_s, rt_s, k_s, v_s))

    n_odd = k_p.shape[0]

    def values_from_stored_order(v, lead):
        v = jnp.swapaxes(v.reshape(n_odd, *lead, 2, n_heads_c, dh), -2, -3)
        return v.reshape(n_odd, *lead, n_heads_c, 2 * dh)

    k_p = k_p.reshape(n_odd, bp, tp // page, page, n_heads_c, 2, dh)
    v_p = values_from_stored_order(v_p, (bp, tp // page, page))
    k_s = k_s.reshape(n_odd, bs, ts, n_heads_c, 2, dh)
    v_s = values_from_stored_order(v_s, (bs, ts))
    return (y_p, y_s, hg_p, rt_p, k_p, v_p, hg_s, rt_s, k_s, v_s)
```

```python
import functools
import math

import jax
import jax.numpy as jnp
import numpy as np
from jax import lax
from jax.experimental import pallas as pl
from jax.experimental.pallas import tpu as pltpu

F32 = jnp.float32
BF16 = jnp.bfloat16

HEAD_DIM = 128
SUBLANES = 8
NORM_EPS = 1e-6
ROPE_BASE = 10000.0
V7X_VMEM_LIMIT = 52 * 1024 * 1024

_NT = (((1,), (1,)), ((), ()))
_TN = (((0,), (0,)), ((), ()))


def _params(*semantics):
    return pltpu.CompilerParams(dimension_semantics=semantics, vmem_limit_bytes=V7X_VMEM_LIMIT)


def _resident(shape):
    zeros = (0,) * len(shape)
    return pl.BlockSpec(shape, lambda *_: zeros, pipeline_mode=pl.Buffered(1))


def _sigmoid(x):
    return 0.5 + 0.5 * jnp.tanh(0.5 * x)


def _silu(x):
    half = 0.5 * x
    return half + half * jnp.tanh(half)


def _rms(x):
    return x * lax.rsqrt(jnp.mean(x * x, axis=-1, keepdims=True) + NORM_EPS)


def _mm(a, b):
    return jnp.dot(a, b, preferred_element_type=F32)


def _ada_kernel(c_ref, w_ref, b_ref, o_ref):
    s = _silu(c_ref[...]).astype(BF16)
    o_ref[0] = _mm(s, w_ref[0].astype(BF16)) + b_ref[0]


def _ada(c, w_ada, b_ada):
    depth, d, n = w_ada.shape
    r = c.shape[0]
    tn = n // 4
    return pl.pallas_call(
        _ada_kernel,
        grid=(depth, n // tn),
        in_specs=[pl.BlockSpec((r, d), lambda l, j: (0, 0)),
                  pl.BlockSpec((1, d, tn), lambda l, j: (l, 0, j)),
                  pl.BlockSpec((1, 1, tn), lambda l, j: (l, 0, j))],
        out_specs=pl.BlockSpec((1, r, tn), lambda l, j: (l, 0, j)),
        out_shape=jax.ShapeDtypeStruct((depth, r, n), F32),
        compiler_params=_params("arbitrary", "arbitrary"),
        name="ada_mod",
    )(c, w_ada, b_ada.reshape(depth, 1, n))


def _modulated_norm(x, nw_ref, mod_ref, shift_col, scale_col):
    d = x.shape[-1]
    shift = mod_ref[0, :, shift_col * d:(shift_col + 1) * d]
    scale = mod_ref[0, :, scale_col * d:(scale_col + 1) * d]
    return _rms(x) * nw_ref[...] * (1.0 + scale) + shift


def _inproj_kernel(x_ref, mod_ref, nw_ref, w_ref, *out_refs, shift_col, scale_col, out_scales, tc):
    h = _modulated_norm(x_ref[...], nw_ref, mod_ref, shift_col, scale_col).astype(BF16)
    off = 0
    for o_ref, out_scale in zip(out_refs, out_scales):
        width = o_ref.shape[-1]
        for c in range(0, width, tc):
            y = _mm(h, w_ref[:, off + c:off + c + tc])
            if out_scale != 1.0:
                y = y * out_scale
            o_ref[:, c:c + tc] = y.astype(o_ref.dtype)
        off += width


def _inproj(x, mod, nw, w, outs, *, tm, shift_col, scale_col):
    m, d = x.shape
    n = w.shape[1]
    rows_per_mod = m // mod.shape[0]
    tiles_per_mod = rows_per_mod // tm
    kern = functools.partial(_inproj_kernel, shift_col=shift_col, scale_col=scale_col,
                             out_scales=tuple(o[2] for o in outs), tc=512)
    return pl.pallas_call(
        kern,
        grid=(m // tm,),
        in_specs=[pl.BlockSpec((tm, d), lambda i: (i, 0)),
                  pl.BlockSpec((1, mod.shape[1], mod.shape[2]), lambda i: (i // tiles_per_mod, 0, 0)),
                  _resident((1, d)),
                  _resident((d, n))],
        out_specs=[pl.BlockSpec((tm, o[0]), lambda i: (i, 0)) for o in outs],
        out_shape=[jax.ShapeDtypeStruct((m, o[0]), o[1]) for o in outs],
        compiler_params=_params("parallel"),
        name="norm_mod_inproj",
    )(x, mod, nw.reshape(1, d), w)


def _qkv_kernel(x_ref, mod_ref, nw_ref, w_ref, q_ref, kb_ref, vb_ref, kr_ref, vr_ref, *,
                q_scale, n_heads, tc):
    h = _modulated_norm(x_ref[...], nw_ref, mod_ref, 0, 1).astype(BF16)
    tm, n_c = q_ref.shape
    lanes = HEAD_DIM
    group = 2 * n_heads
    for c in range(0, n_c, tc):
        q_ref[:, c:c + tc] = (_mm(h, w_ref[:, c:c + tc]) * q_scale).astype(q_ref.dtype)
    for c in range(0, n_c, tc):
        y = _mm(h, w_ref[:, n_c + c:n_c + c + tc])
        kb_ref[:, c:c + tc] = y.astype(kb_ref.dtype)
        for i in range(tc // lanes):
            kr_ref[pl.ds(c // lanes + i, tm, stride=group), :] = y[:, i * lanes:(i + 1) * lanes]
    for c in range(0, n_c, tc):
        y = _mm(h, w_ref[:, 2 * n_c + c:2 * n_c + c + tc])
        vb_ref[:, c:c + tc] = y.astype(vb_ref.dtype)
        for i in range(tc // lanes):
            head, half = divmod(c // lanes + i, 2)
            vr_ref[pl.ds(half * n_heads + head, tm, stride=group), :] = y[:, i * lanes:(i + 1) * lanes]


def _qkv(x, mod, nw, w, *, tm, q_scale, n_heads):
    m, d = x.shape
    n_c = w.shape[1] // 3
    group = 2 * n_heads
    tiles_per_mod = (m // mod.shape[0]) // tm
    kern = functools.partial(_qkv_kernel, q_scale=q_scale, n_heads=n_heads, tc=512)
    cols = pl.BlockSpec((tm, n_c), lambda i: (i, 0))
    rows = pl.BlockSpec((tm * group, HEAD_DIM), lambda i: (i, 0))
    return pl.pallas_call(
        kern,
        grid=(m // tm,),
        in_specs=[pl.BlockSpec((tm, d), lambda i: (i, 0)),
                  pl.BlockSpec((1, mod.shape[1], mod.shape[2]), lambda i: (i // tiles_per_mod, 0, 0)),
                  _resident((1, d)),
                  _resident((d, 3 * n_c))],
        out_specs=[cols, cols, cols, rows, rows],
        out_shape=[jax.ShapeDtypeStruct((m, n_c), BF16)] * 3
                  + [jax.ShapeDtypeStruct((m * group, HEAD_DIM), F32)] * 2,
        compiler_params=_params("parallel"),
        name="norm_mod_qkv",
    )(x, mod, nw.reshape(1, d), w)


N_MLP_INPUTS = 8
N_ATTENTION_INPUTS = 7


def _mlp_kernel(*refs, n_parts, final_norm, attention):
    if attention is None:
        o_ref, x_ref, mod_ref, nw_ref, wo_ref, wu_ref, wd_ref, fw_ref, out_ref, h_ref, acc_ref = refs
    else:
        pt_ref, refs = refs[0], refs[1:]
        o_ref, x_ref, mod_ref, nw_ref, wo_ref, wu_ref, wd_ref, fw_ref = refs[:N_MLP_INPUTS]
        attention_in = refs[N_MLP_INPUTS:N_MLP_INPUTS + N_ATTENTION_INPUTS]
        out_ref, attention_out, h_ref, acc_ref = refs[N_MLP_INPUTS + N_ATTENTION_INPUTS:][:4]
        attention_refs = (pt_ref, *attention_in, attention_out,
                          *refs[N_MLP_INPUTS + N_ATTENTION_INPUTS + 4:])
    d = x_ref.shape[-1]
    cw = wu_ref.shape[1] // n_parts

    def mod_col(col):
        return mod_ref[0, :, col * d:(col + 1) * d]

    hidden = [None] * n_parts

    def mlp_up(r):
        if r == 0:
            x1 = x_ref[...] + mod_col(2) * _mm(o_ref[...].astype(BF16), wo_ref[...])
            out_ref[...] = x1
            h_ref[...] = _modulated_norm(x1, nw_ref, mod_ref, 3, 4).astype(BF16)
        u = jnp.maximum(_mm(h_ref[...], wu_ref[:, r * cw:(r + 1) * cw]), 0.0)
        hidden[r] = (u * u).astype(BF16)

    def mlp_down(r):
        part = _mm(hidden[r], wd_ref[r * cw:(r + 1) * cw, :])
        acc_ref[...] = part if r == 0 else acc_ref[...] + part

    def finish():
        y = out_ref[...] + mod_col(5) * acc_ref[...]
        if final_norm:
            y = _rms(y) * fw_ref[...]
        out_ref[...] = y

    if attention is None:
        for r in range(n_parts):
            mlp_up(r)
            mlp_down(r)
        finish()
    else:
        aligned = attention["steps_per_seq"] % n_parts == 0
        mlp_up(0)
        for r in range(n_parts):
            last = r == n_parts - 1
            _paged_attention_step(pl.program_id(0) * n_parts + r, pl.num_programs(0) * n_parts,
                                  attention_refs,
                                  (functools.partial(mlp_down, r),
                                   finish if last else functools.partial(mlp_up, r + 1),
                                   lambda: None),
                                  may_open=(r == 0 or not aligned),
                                  may_close=(last or not aligned), **attention)


def _mlp(o, x, mod, nw, w_out, w_up, w_down, final_w, *, tm, final_norm, n_parts, paged=None):
    m, d = x.shape
    k = o.shape[1]
    dff = w_up.shape[1]
    n_tiles = m // tm
    tiles_per_mod = (m // mod.shape[0]) // tm
    in_specs = [pl.BlockSpec((tm, k), lambda i, *_: (i, 0)),
                pl.BlockSpec((tm, d), lambda i, *_: (i, 0)),
                pl.BlockSpec((1, mod.shape[1], mod.shape[2]), lambda i, *_: (i // tiles_per_mod, 0, 0)),
                _resident((1, d)),
                _resident((k, d)),
                _resident((d, dff)),
                _resident((dff, d)),
                _resident((1, d))]
    operands = [o, x, mod, nw.reshape(1, d), w_out, w_up, w_down, final_w.reshape(1, d)]
    out_specs = pl.BlockSpec((tm, d), lambda i, *_: (i, 0))
    out_shape = jax.ShapeDtypeStruct((m, d), F32)
    scratch = [pltpu.VMEM((tm, d), BF16), pltpu.VMEM((tm, d), F32)]
    if paged is None:
        kern = functools.partial(_mlp_kernel, n_parts=n_parts, final_norm=final_norm, attention=None)
        return pl.pallas_call(
            kern, grid=(n_tiles,), in_specs=in_specs, out_specs=out_specs, out_shape=out_shape,
            scratch_shapes=scratch, compiler_params=_params("arbitrary"), name="outproj_mlp_residual",
        )(*operands)

    q_rows, k_new, v_new = paged["q_rows"], paged["k_new"], paged["v_new"]
    cache_k, cache_v, page_table = paged["cache_k"], paged["cache_v"], paged["page_table"]
    n_seq = paged["n_seq"]
    n_rows, dh = q_rows.shape[1:]
    n_pages = page_table.shape[1]
    page_rows = cache_k.shape[1]
    assert 2 * paged["n_heads"] == SUBLANES, "one (map, token) block of query rows must fill one vreg"
    pages_per_step = (n_seq * n_pages) // (n_tiles * n_parts)
    assert pages_per_step * n_tiles * n_parts == n_seq * n_pages and n_pages % pages_per_step == 0
    attention = dict(pages_per_step=pages_per_step, page_base=paged["page_base"],
                     steps_per_seq=n_pages // pages_per_step, first_seq=paged["first_seq"],
                     tl=paged["tl"], n_heads=paged["n_heads"], lam_init=paged["lam_init"])
    kern = functools.partial(_mlp_kernel, n_parts=n_parts, final_norm=final_norm, attention=attention)
    ring = (PAGE_RING_DEPTH, pages_per_step, page_rows, dh)
    grid_spec = pltpu.PrefetchScalarGridSpec(
        num_scalar_prefetch=1,
        grid=(n_tiles,),
        in_specs=in_specs + [_resident(paged["lam_p"].shape), _resident((1, 2 * dh)),
                             _resident(q_rows.shape), _resident(k_new.shape), _resident(v_new.shape),
                             pl.BlockSpec(memory_space=pl.ANY), pl.BlockSpec(memory_space=pl.ANY)],
        out_specs=[out_specs, pl.BlockSpec((n_seq, n_rows // 2, dh), lambda i, *_: (0, 0, 0))],
        scratch_shapes=scratch + [pltpu.VMEM((n_rows, dh), F32), pltpu.VMEM((n_rows, dh), F32),
                                  pltpu.VMEM((n_rows, dh), F32),
                                  pltpu.VMEM((dh, dh), F32), pltpu.VMEM((dh, dh), F32),
                                  pltpu.VMEM(ring, F32), pltpu.VMEM(ring, F32),
                                  pltpu.SemaphoreType.DMA((2, PAGE_RING_DEPTH))],
    )
    return pl.pallas_call(
        kern,
        grid_spec=grid_spec,
        out_shape=[out_shape, jax.ShapeDtypeStruct((n_seq, n_rows // 2, dh), F32)],
        compiler_params=_params("arbitrary"),
        name="outproj_mlp_paged_attention",
    )(page_table.reshape(-1), *operands, paged["lam_p"], paged["subln_w"].reshape(1, 2 * dh),
      q_rows, k_new, v_new, cache_k, cache_v)


def _hgrn_lower_bound(logits_ref, layer_slot):
    lg = logits_ref[...]
    ex = jnp.exp(lg - jnp.max(lg, axis=0, keepdims=True))
    pr = ex / jnp.sum(ex, axis=0, keepdims=True)
    return jnp.sum(pr[:layer_slot + 1], axis=0, keepdims=True)


def _rotate_pairs(x, cos, sin_signed):
    n = x.shape[-1]
    lane = lax.broadcasted_iota(jnp.int32, x.shape, 1)
    swapped = jnp.where((lane & 1) == 0, pltpu.roll(x, n - 1, 1), pltpu.roll(x, 1, 1))
    return x * cos + swapped * sin_signed


def _cumsum_rows(x):
    n = x.shape[0]
    row = lax.broadcasted_iota(jnp.int32, x.shape, 0)
    d = 1
    while d < n:
        x = x + jnp.where(row >= d, pltpu.roll(x, d, 0), 0.0)
        d *= 2
    return x


def _split_row(g, b):
    n, lanes = g.shape
    if 2 * b >= SUBLANES:
        g3 = g.reshape(n // (2 * b), 2 * b, lanes)
        return jnp.broadcast_to(g3[:, b - 1:b, :], g3.shape).reshape(n, lanes)
    rmod = lax.broadcasted_iota(jnp.int32, g.shape, 0) & (2 * b - 1)
    out = g
    for r in range(2 * b):
        if r != b - 1:
            out = jnp.where(rmod == r, pltpu.roll(g, (r - (b - 1)) % n, 0), out)
    return out


def _level_matrix(n):
    t = np.arange(n)[:, None]
    s = np.arange(n)[None, :]
    hi = np.floor(np.log2(np.maximum(t ^ s, 1))).astype(np.int32)
    return jnp.asarray(np.where(s < t, hi, -1).astype(np.int32))


def _hgrn_chunk(q, k, v, g2, lv, st):
    n = q.shape[0]
    row = lax.broadcasted_iota(jnp.int32, q.shape, 0)
    ri = lax.broadcasted_iota(jnp.int32, (n, n), 0)
    ci = lax.broadcasted_iota(jnp.int32, (n, n), 1)
    a = jnp.where(ri == ci, jnp.sum(q * k, axis=-1, keepdims=True), 0.0)
    b, level = 1, 0
    while b < n:
        if b >= SUBLANES:
            parts = []
            for lo in range(0, n, 2 * b):
                edge = g2[lo + b - 1:lo + b, :]
                parts.append(k[lo:lo + b] * jnp.exp2(edge - g2[lo:lo + b]))
                parts.append(q[lo + b:lo + 2 * b] * jnp.exp2(g2[lo + b:lo + 2 * b] - edge))
            z = jnp.concatenate(parts, axis=0).astype(BF16)
        else:
            is_query = (row & b) != 0
            sign = jnp.where(is_query, 1.0, -1.0)
            z = (jnp.where(is_query, q, k) * jnp.exp2((g2 - _split_row(g2, b)) * sign)).astype(BF16)
        a = jnp.where(lv == level, lax.dot_general(z, z, _NT, preferred_element_type=F32), a)
        b, level = 2 * b, level + 1
    carried = lax.dot_general((q * jnp.exp2(g2)).astype(BF16), st.astype(BF16), _NT,
                              preferred_element_type=F32)
    o = _mm(a.astype(BF16), v.astype(BF16)) + carried
    g_end = g2[n - 1:n, :]
    k_dec = (k * jnp.exp2(g_end - g2)).astype(BF16)
    st_new = st * jnp.exp2(g_end) + lax.dot_general(v.astype(BF16), k_dec, _TN,
                                                    preferred_element_type=F32)
    return o, st_new


def _retention_chunk(q, k, v, log_gamma, st):
    n = q.shape[0]
    rowf = lax.broadcasted_iota(jnp.int32, q.shape, 0).astype(F32)
    ri = lax.broadcasted_iota(jnp.int32, (n, n), 0)
    ci = lax.broadcasted_iota(jnp.int32, (n, n), 1)
    decay = jnp.where(ci <= ri, jnp.exp((ri - ci).astype(F32) * log_gamma), 0.0)
    a = lax.dot_general(q.astype(BF16), k.astype(BF16), _NT, preferred_element_type=F32) * decay
    q_dec = (q * jnp.exp((rowf + 1.0) * log_gamma)).astype(BF16)
    o = _mm(a.astype(BF16), v.astype(BF16)) + _mm(q_dec, st.astype(BF16))
    k_dec = (k * jnp.exp((n - 1.0 - rowf) * log_gamma)).astype(BF16)
    st_new = st * math.exp(n * log_gamma) + lax.dot_general(k_dec, v.astype(BF16), _TN,
                                                            preferred_element_type=F32)
    return o, st_new


def _retention_log_gamma(h):
    return math.log(1.0 - 2.0 ** (-5.0 - h))


def _mixer_prompt_kernel(x_ref, mod_ref, nw_ref, w_ref, cos_ref, sin_ref, lv_ref, lbl_ref, na_ref,
                         o_ref, sa_out, sb_out, proj_ref, sat_ref, sb_ref, *, layer_slot, n_heads, tc):
    t = pl.program_id(1)
    hd = HEAD_DIM
    w = n_heads * hd
    d = x_ref.shape[-1]

    @pl.when(t == 0)
    def _():
        sat_ref[...] = jnp.zeros_like(sat_ref)
        sb_ref[...] = jnp.zeros_like(sb_ref)

    def project(r):
        shift = mod_ref[r, :, 0:d]
        scale = mod_ref[r, :, d:2 * d]
        h = (_rms(x_ref[r]) * nw_ref[...] * (1.0 + scale) + shift).astype(BF16)
        for c in range(0, proj_ref.shape[-1], tc):
            proj_ref[r, :, c:c + tc] = _mm(h, w_ref[:, c:c + tc])

    lb = _hgrn_lower_bound(lbl_ref, layer_slot)
    lv = lv_ref[...]
    cos = cos_ref[...]
    sin = sin_ref[...]
    project(0)
    for r in range(proj_ref.shape[0]):
        def cols(group, h):
            return proj_ref[r, :, group * w + h * hd:group * w + (h + 1) * hd]

        fz = lb + (1.0 - lb) * _sigmoid(proj_ref[r, :, w:2 * w])
        g_all = _cumsum_rows(jnp.log2(fz))
        k_all = 1.0 - fz
        for h in range(n_heads):
            sl = slice(h * hd, (h + 1) * hd)
            q = _silu(cols(0, h)) * hd ** -0.5
            o, st = _hgrn_chunk(q, k_all[:, sl], cols(2, h), g_all[:, sl], lv, sat_ref[r, h])
            sat_ref[r, h] = st
            o_ref[r, :, sl] = (_rms(o) * na_ref[...] * _silu(cols(3, h))).astype(o_ref.dtype)
        if r + 1 < proj_ref.shape[0]:
            project(r + 1)
        for h in range(n_heads):
            q = _rotate_pairs(cols(4, h), cos, sin)
            k = _rotate_pairs(cols(5, h), cos, sin) * hd ** -0.5
            o, st = _retention_chunk(q, k, cols(6, h), _retention_log_gamma(h), sb_ref[r, h])
            sb_ref[r, h] = st
            o_ref[r, :, w + h * hd:w + (h + 1) * hd] = (_rms(o) * _silu(cols(7, h))).astype(o_ref.dtype)

    @pl.when(t == pl.num_programs(1) - 1)
    def _():
        for r in range(proj_ref.shape[0]):
            for h in range(n_heads):
                sa_out[r, h] = sat_ref[r, h].T
                sb_out[r, h] = sb_ref[r, h]


def _mixer_prompt(x, mod, nw, w_in, cos, sin, lb_logits, norm_a, *, layer_slot, chunk, rows_per_step):
    bsz, tlen, d = x.shape
    n = w_in.shape[1]
    hd = HEAD_DIM
    n_heads = n // (8 * hd)
    rb = rows_per_step
    kern = functools.partial(_mixer_prompt_kernel, layer_slot=layer_slot, n_heads=n_heads, tc=512)
    state = jax.ShapeDtypeStruct((bsz, n_heads, hd, hd), F32)
    state_spec = pl.BlockSpec((rb, n_heads, hd, hd), lambda b, t: (b, 0, 0, 0))
    return pl.pallas_call(
        kern,
        grid=(bsz // rb, tlen // chunk),
        in_specs=[pl.BlockSpec((rb, chunk, d), lambda b, t: (b, t, 0)),
                  pl.BlockSpec((rb, 1, mod.shape[2]), lambda b, t: (b, 0, 0)),
                  _resident((1, d)),
                  _resident((d, n)),
                  pl.BlockSpec((chunk, hd), lambda b, t: (t, 0)),
                  pl.BlockSpec((chunk, hd), lambda b, t: (t, 0)),
                  pl.BlockSpec((chunk, chunk), lambda b, t: (0, 0)),
                  pl.BlockSpec(lb_logits.shape, lambda b, t: (0, 0)),
                  pl.BlockSpec((1, hd), lambda b, t: (0, 0))],
        out_specs=[pl.BlockSpec((rb, chunk, 2 * n_heads * hd), lambda b, t: (b, t, 0)),
                   state_spec, state_spec],
        out_shape=[jax.ShapeDtypeStruct((bsz, tlen, 2 * n_heads * hd), BF16), state, state],
        scratch_shapes=[pltpu.VMEM((rb, chunk, n), F32),
                        pltpu.VMEM((rb, n_heads, hd, hd), F32), pltpu.VMEM((rb, n_heads, hd, hd), F32)],
        compiler_params=_params("arbitrary", "arbitrary"),
        name="inproj_mixer_prompt",
    )(x, mod, nw.reshape(1, d), w_in, cos, sin, _level_matrix(chunk), lb_logits, norm_a.reshape(1, hd))


def _pair_chunk(q, k, v, g, s0, dec, kpad_ref, vpad_ref):
    tl = q.shape[0]
    row = lax.broadcasted_iota(jnp.int32, q.shape, 0)
    o = _mm((q * jnp.exp(g)).astype(BF16), s0.astype(BF16))
    for s in range(tl):
        live = row >= s
        w = jnp.exp(jnp.where(live, g - g[s:s + 1, :], 0.0))
        a = jnp.sum(jnp.where(live, q * w * k[s:s + 1, :], 0.0), axis=-1, keepdims=True)
        o = o + a * v[s:s + 1, :]
    kpad_ref[0:tl, :] = k * jnp.exp(g[tl - 1:tl, :] - g)
    vpad_ref[0:tl, :] = v
    upd = lax.dot_general(kpad_ref[...].astype(BF16), vpad_ref[...].astype(BF16), _TN,
                          preferred_element_type=F32)
    return o, s0 * dec + upd


def _mixer_sample_kernel(proj_ref, cos_ref, sin_ref, lbl_ref, na_ref, sa_ref, sb_ref,
                         o_ref, sa_out, sb_out, kpad_ref, vpad_ref, *, layer_slot, n_heads):
    hd = HEAD_DIM
    w = n_heads * hd
    tl = proj_ref.shape[1]
    kpad_ref[...] = jnp.zeros_like(kpad_ref)
    vpad_ref[...] = jnp.zeros_like(vpad_ref)

    def cols(group, h):
        return proj_ref[0, :, group * w + h * hd:group * w + (h + 1) * hd]

    lb = _hgrn_lower_bound(lbl_ref, layer_slot)
    fz = lb + (1.0 - lb) * _sigmoid(proj_ref[0, :, w:2 * w])
    lf = jnp.log(fz)
    row = lax.broadcasted_iota(jnp.int32, lf.shape, 0)
    g_all = jnp.zeros_like(lf)
    for j in range(tl):
        g_all = g_all + jnp.where(row >= j, lf[j:j + 1, :], 0.0)
    k_all = 1.0 - fz
    for h in range(n_heads):
        sl = slice(h * hd, (h + 1) * hd)
        g = g_all[:, sl]
        q = _silu(cols(0, h)) * hd ** -0.5
        dec = jnp.broadcast_to(jnp.exp(g[tl - 1:tl, :]), (hd, hd)).T
        o, st = _pair_chunk(q, k_all[:, sl], cols(2, h), g, sa_ref[0, h], dec, kpad_ref, vpad_ref)
        sa_out[0, h] = st
        o_ref[0, :, sl] = _rms(o) * na_ref[...] * _silu(cols(3, h))

    cos = cos_ref[...]
    sin = sin_ref[...]
    rowf = lax.broadcasted_iota(jnp.int32, (tl, hd), 0).astype(F32)
    for h in range(n_heads):
        lg = _retention_log_gamma(h)
        q = _rotate_pairs(cols(4, h), cos, sin)
        k = _rotate_pairs(cols(5, h), cos, sin) * hd ** -0.5
        o, st = _pair_chunk(q, k, cols(6, h), (rowf + 1.0) * lg, sb_ref[0, h], math.exp(tl * lg),
                            kpad_ref, vpad_ref)
        sb_out[0, h] = st
        o_ref[0, :, w + h * hd:w + (h + 1) * hd] = _rms(o) * _silu(cols(7, h))


def _mixer_sample(proj, cos, sin, lb_logits, norm_a, state_a, state_b, *, layer_slot):
    bsz, tl, n = proj.shape
    hd = HEAD_DIM
    n_heads = n // (8 * hd)
    kern = functools.partial(_mixer_sample_kernel, layer_slot=layer_slot, n_heads=n_heads)
    state = jax.ShapeDtypeStruct((bsz, n_heads, hd, hd), F32)
    state_spec = pl.BlockSpec((1, n_heads, hd, hd), lambda b: (b, 0, 0, 0))
    return pl.pallas_call(
        kern,
        grid=(bsz,),
        in_specs=[pl.BlockSpec((1, tl, n), lambda b: (b, 0, 0)),
                  pl.BlockSpec((tl, hd), lambda b: (0, 0)),
                  pl.BlockSpec((tl, hd), lambda b: (0, 0)),
                  pl.BlockSpec(lb_logits.shape, lambda b: (0, 0)),
                  pl.BlockSpec((1, hd), lambda b: (0, 0)),
                  state_spec, state_spec],
        out_specs=[pl.BlockSpec((1, tl, 2 * n_heads * hd), lambda b: (b, 0, 0)), state_spec, state_spec],
        out_shape=[jax.ShapeDtypeStruct((bsz, tl, 2 * n_heads * hd), F32), state, state],
        scratch_shapes=[pltpu.VMEM((hd, hd), F32), pltpu.VMEM((hd, hd), F32)],
        compiler_params=_params("arbitrary"),
        name="mixer_sample",
    )(proj, cos, sin, lb_logits, norm_a.reshape(1, hd), state_a, state_b)


def _diff_lambda(lam_ref, lam_init):
    lp = lam_ref[...]
    return (jnp.exp(jnp.sum(lp[0:1] * lp[1:2], axis=-1, keepdims=True))
            - jnp.exp(jnp.sum(lp[2:3] * lp[3:4], axis=-1, keepdims=True)) + lam_init)


LOG2_E = math.log2(math.e)


def _attn_prompt_kernel(lam_ref, sw_ref, q_ref, k_ref, v_ref, o_ref, m_ref, l_ref, acc_ref, *,
                        tq, lam_init):
    dh = HEAD_DIM
    lanes = m_ref.shape[-1]
    n_tiles = q_ref.shape[1] // tq
    m_ref[...] = jnp.full_like(m_ref, -jnp.inf)
    l_ref[...] = jnp.zeros_like(l_ref)
    acc_ref[...] = jnp.zeros_like(acc_ref)

    def tile_pair(q0, nq, k0, nk):
        rq = slice(q0, q0 + nq)
        rk = slice(k0, k0 + nk)
        v = v_ref[0, rk, :]
        scores = []
        for j in range(2):
            s = lax.dot_general(q_ref[0, rq, j * dh:(j + 1) * dh], k_ref[0, rk, j * dh:(j + 1) * dh],
                                _NT, preferred_element_type=F32)
            if k0 + nk - 1 > q0:
                ri = lax.broadcasted_iota(jnp.int32, s.shape, 0)
                ci = lax.broadcasted_iota(jnp.int32, s.shape, 1)
                s = jnp.where(ci - ri <= q0 - k0, s, -jnp.inf)
            scores.append(s)
        probs, alphas = [], []
        for j in range(2):
            s = scores[j]
            m_old = m_ref[j, rq]
            m_new = jnp.maximum(m_old, jnp.max(s, axis=-1, keepdims=True))
            alpha = jnp.exp2(m_old - m_new)
            l_new = alpha * l_ref[j, rq]
            tiles = []
            for c in range(0, s.shape[1], lanes):
                p = jnp.exp2(s[:, c:c + lanes] - m_new)
                l_new = l_new + p
                tiles.append(p.astype(BF16))
            l_ref[j, rq] = l_new
            m_ref[j, rq] = m_new
            probs.append(jnp.concatenate(tiles, axis=1))
            alphas.append(alpha)
        for j in range(2):
            pv = _mm(probs[j], v)
            for c in range(0, pv.shape[1], lanes):
                acc_ref[j, rq, c:c + lanes] = (alphas[j] * acc_ref[j, rq, c:c + lanes]
                                               + pv[:, c:c + lanes])

    for kj in range(n_tiles):
        for qi in range(kj, n_tiles):
            tile_pair(qi * tq, tq, kj * tq, tq)

    lam = _diff_lambda(lam_ref, lam_init)
    for qi in range(n_tiles):
        rq = slice(qi * tq, (qi + 1) * tq)
        l0 = jnp.sum(l_ref[0, rq], axis=-1, keepdims=True)
        l1 = jnp.sum(l_ref[1, rq], axis=-1, keepdims=True)
        o = acc_ref[0, rq] / l0 - lam * (acc_ref[1, rq] / l1)
        o_ref[0, rq] = (_rms(o) * sw_ref[...] * (1.0 - lam_init)).astype(o_ref.dtype)


def _attn_prompt(q, k, v, lam_p, subln_w, *, lam_init, tq):
    bsz, tlen, n = q.shape
    e = 2 * HEAD_DIM
    n_heads = n // e
    kern = functools.partial(_attn_prompt_kernel, tq=tq, lam_init=lam_init)
    head_cols = pl.BlockSpec((1, tlen, e), lambda b, h: (b, 0, h))
    return pl.pallas_call(
        kern,
        grid=(bsz, n_heads),
        in_specs=[pl.BlockSpec(lam_p.shape, lambda b, h: (0, 0)),
                  pl.BlockSpec((1, e), lambda b, h: (0, 0)),
                  head_cols, head_cols, head_cols],
        out_specs=head_cols,
        out_shape=jax.ShapeDtypeStruct((bsz, tlen, n), BF16),
        scratch_shapes=[pltpu.VMEM((2, tlen, HEAD_DIM), F32), pltpu.VMEM((2, tlen, HEAD_DIM), F32),
                        pltpu.VMEM((2, tlen, e), F32)],
        compiler_params=_params("parallel", "parallel"),
        name="diff_attn_prompt",
    )(lam_p, subln_w.reshape(1, e), q, k, v)


def _sample_row_fields(shape, tl, n_heads):
    r = lax.broadcasted_iota(jnp.int32, shape, 0)
    head = n_heads - 1 - r % n_heads
    half = (r // n_heads) % 2
    token = (r // (2 * n_heads)) % tl
    qmap = r // (2 * n_heads * tl)
    return head, qmap, token, half


PAGE_RING_DEPTH = 3


def _paged_attention_step(step, n_steps, refs, overlapped_work, *, pages_per_step, page_base,
                          steps_per_seq, first_seq, tl, n_heads, lam_init, may_open, may_close):
    (pt_ref, lam_ref, sw_ref, q_ref, kn_ref, vn_ref, ck_hbm, cv_hbm, o_ref,
     m_ref, l_ref, acc_ref, kpad_ref, vpad_ref, kbuf, vbuf, sems) = refs
    seq = step // steps_per_seq
    p = step % steps_per_seq
    group = 2 * n_heads
    n_rows = q_ref.shape[1]
    rows_per_map = n_rows // 2

    def page_copies(of_step, slot):
        out = []
        for i in range(pages_per_step):
            page = page_base + pt_ref[first_seq * steps_per_seq * pages_per_step
                                      + of_step * pages_per_step + i]
            out.append(pltpu.make_async_copy(ck_hbm.at[page], kbuf.at[slot, i], sems.at[0, slot]))
            out.append(pltpu.make_async_copy(cv_hbm.at[page], vbuf.at[slot, i], sems.at[1, slot]))
        return out

    def start_step(of_step):
        for cp in page_copies(of_step, of_step % PAGE_RING_DEPTH):
            cp.start()

    ahead = PAGE_RING_DEPTH - 1

    @pl.when(step == 0)
    def _():
        for s in range(ahead):
            @pl.when(s < n_steps)
            def _():
                start_step(s)

    @pl.when(step + ahead < n_steps)
    def _():
        start_step(step + ahead)

    slot = step % PAGE_RING_DEPTH
    for cp in page_copies(step, slot):
        cp.wait()

    if may_open:
        @pl.when(p == 0)
        def _():
            m_ref[...] = jnp.full_like(m_ref, -jnp.inf)
            l_ref[...] = jnp.zeros_like(l_ref)
            acc_ref[...] = jnp.zeros_like(acc_ref)

    q = q_ref[first_seq + seq]

    def own_lanes(n_keys, causal):
        head, qmap, token, _ = _sample_row_fields((n_rows, n_keys), tl, n_heads)
        key = lax.broadcasted_iota(jnp.int32, (n_rows, n_keys), 1)
        own = (key % group) == 2 * head + qmap
        return jnp.logical_and(own, key // group <= token) if causal else own

    def partial_softmax(pages, causal, hooks=(None, None)):
        before_scores, after_scores = hooks
        own = own_lanes(pages[0][0].shape[0], causal)
        keys = [k_rows.astype(BF16) for k_rows, _ in pages]
        if before_scores is not None:
            before_scores()
        scores = [lax.dot_general(q, k, _NT, preferred_element_type=F32) for k in keys]
        if after_scores is not None:
            after_scores()
        stats, probs = [], []
        for s in scores:
            s = jnp.where(own, s, -jnp.inf)
            m = jnp.max(s, axis=-1, keepdims=True)
            pexp = jnp.exp2(s - m)
            stats.append((m, jnp.sum(pexp, axis=-1, keepdims=True)))
            aligned = []
            for blk in range(n_rows // SUBLANES):
                shift = group - (n_heads - 1) - (blk * SUBLANES) // rows_per_map
                aligned.append(pltpu.roll(pexp[blk * SUBLANES:(blk + 1) * SUBLANES, :], shift, 1,
                                          stride=1, stride_axis=0))
            probs.append(jnp.concatenate(aligned, axis=0).astype(BF16))
        return [(m, l, _mm(p_rows, pltpu.roll(v_rows, group, 0).astype(BF16)))
                for (m, l), p_rows, (_, v_rows) in zip(stats, probs, pages)]

    def merge(parts):
        m_run = m_ref[...]
        m_new = m_run
        for m, _, _ in parts:
            m_new = jnp.maximum(m_new, m)
        alpha = jnp.exp2(m_run - m_new)
        l_new = alpha * l_ref[...]
        acc_new = alpha * acc_ref[...]
        for m, l, o in parts:
            wgt = jnp.exp2(m - m_new)
            l_new = l_new + wgt * l
            acc_new = acc_new + wgt * o
        m_ref[...] = m_new
        l_ref[...] = l_new
        acc_ref[...] = acc_new

    before_scores, after_scores, after_values = overlapped_work
    parts = partial_softmax([(kbuf[slot, i], vbuf[slot, i]) for i in range(pages_per_step)], False,
                            (before_scores, after_scores))
    after_values()
    merge(parts)

    if may_close:
        @pl.when(p == steps_per_seq - 1)
        def _():
            n_new = kn_ref.shape[1]
            kpad_ref[...] = jnp.zeros_like(kpad_ref)
            vpad_ref[...] = jnp.zeros_like(vpad_ref)
            kpad_ref[0:n_new, :] = kn_ref[first_seq + seq]
            vpad_ref[0:n_new, :] = vn_ref[first_seq + seq]
            merge(partial_softmax([(kpad_ref[...], vpad_ref[...])], True))

            o = (acc_ref[0:rows_per_map, :] / l_ref[0:rows_per_map, :]
                 - _diff_lambda(lam_ref, lam_init) * (acc_ref[rows_per_map:, :] / l_ref[rows_per_map:, :]))
            _, _, _, half = _sample_row_fields(o.shape, tl, n_heads)
            ss = jnp.broadcast_to(jnp.sum(o * o, axis=-1, keepdims=True), o.shape)
            other = jnp.where(half == 0, pltpu.roll(ss, rows_per_map - n_heads, 0),
                              pltpu.roll(ss, n_heads, 0))
            inv = lax.rsqrt((ss + other) / (2 * HEAD_DIM) + NORM_EPS)
            sw = jnp.where(half == 0, sw_ref[:, 0:HEAD_DIM], sw_ref[:, HEAD_DIM:])
            o_ref[seq] = o * inv * sw * (1.0 - lam_init)


def _sample_query_rows(q, n_heads):
    bsz, tl, _ = q.shape
    r = np.arange(4 * n_heads * tl)
    head = n_heads - 1 - r % n_heads
    token = (r // (2 * n_heads)) % tl
    qmap = r // (2 * n_heads * tl)
    return q.reshape(bsz, tl, n_heads, 2, HEAD_DIM)[:, token, head, qmap, :]


def _values_in_stored_order(v, n_heads):
    lead = v.shape[:-2]
    tokens = v.shape[-2]
    v = v.reshape(*lead, tokens, n_heads, 2, HEAD_DIM)
    return jnp.swapaxes(v, -2, -3).reshape(*lead, tokens * 2 * n_heads, HEAD_DIM)


def _rope_tables(pos0, tlen):
    inv = 1.0 / (ROPE_BASE ** jnp.linspace(0.0, 1.0, HEAD_DIM // 2, dtype=F32))
    ang = (pos0 + jnp.arange(tlen)).astype(F32)[:, None] * inv[None, :]
    sign = jnp.tile(jnp.asarray([-1.0, 1.0], F32), HEAD_DIM // 2)
    return jnp.repeat(jnp.cos(ang), 2, axis=1), jnp.repeat(jnp.sin(ang), 2, axis=1) * sign


PROMPT_ROW_TILE = 512
PROMPT_MLP_PARTS = 8


def kernel(x_prompt, x_sample, state_hgrn, state_ret, cache_k, cache_v, page_table, c_prompt, c_sample,
           w_ada, b_ada, norm_w, w_in_ab, w_out_ab, hgrn_lb_logits, hgrn_norm_w, w_in_c, w_out_c,
           diff_lambda, diff_subln_w, w_mlp_up, w_mlp_down, final_norm_w):
    bp, tp, d = x_prompt.shape
    bs, ts, _ = x_sample.shape
    n_slots, n_phys, page, n_heads_c, _, dh = cache_k.shape

    mod = _ada(jnp.concatenate([c_prompt, c_sample], axis=0), w_ada, b_ada)
    depth = mod.shape[0]
    assert depth % 2 == 0, "layers come in (recurrent mixer, attention) pairs"
    mod_p = mod[:, :bp].reshape(depth, bp, 1, 6 * d)
    mod_s = jnp.repeat(mod[:, bp:], ts, axis=1).reshape(depth, 1, bs * ts, 6 * d)

    w_in_ab, w_out_ab, w_in_c, w_out_c, w_mlp_up, w_mlp_down = (
        t.astype(BF16) for t in (w_in_ab, w_out_ab, w_in_c, w_out_c, w_mlp_up, w_mlp_down))
    n_in = w_in_ab.shape[2]
    n_c = w_in_c.shape[2] // 3
    group = 2 * n_heads_c
    q_scale = dh ** -0.5 * LOG2_E

    past_len = page_table.shape[1] * page
    page_rows = page * group
    cache_k_rows = cache_k.reshape(n_slots * n_phys, page_rows, dh)
    cache_v_rows = _values_in_stored_order(
        cache_v.reshape(n_slots * n_phys, page, n_heads_c * 2 * dh), n_heads_c)
    cos_p, sin_p = _rope_tables(0, tp)
    cos_s, sin_s = _rope_tables(past_len, ts)

    xp = x_prompt.reshape(bp * tp, d)
    xs = x_sample.reshape(bs * ts, d)
    tm_p, tm_s = PROMPT_ROW_TILE, bs * ts
    hg_p, rt_p, k_p, v_p, hg_s, rt_s, k_s, v_s = ([] for _ in range(8))
    for l in range(0, depth, 2):
        e = a = l // 2
        lam_init = 0.8 - 0.6 * math.exp(-0.3 * (l + 1))

        (proj,) = _inproj(xs, mod_s[l], norm_w[l, 0], w_in_ab[e], [(n_in, F32, 1.0)],
                          tm=tm_s, shift_col=0, scale_col=1)
        o, s_a, s_b = _mixer_sample(proj.reshape(bs, ts, n_in), cos_s, sin_s, hgrn_lb_logits,
                                    hgrn_norm_w[e], state_hgrn[e], state_ret[e], layer_slot=e)
        hg_s.append(s_a)
        rt_s.append(s_b)
        xs = _mlp(o.reshape(bs * ts, -1), xs, mod_s[l], norm_w[l, 1], w_out_ab[e], w_mlp_up[l],
                  w_mlp_down[l], final_norm_w, tm=tm_s, final_norm=False, n_parts=1)
        q, _, _, k_rows, v_rows = _qkv(xs, mod_s[l + 1], norm_w[l + 1, 0], w_in_c[a], tm=tm_s,
                                       q_scale=q_scale, n_heads=n_heads_c)
        k_s.append(k_rows)
        v_s.append(v_rows)
        paged = dict(q_rows=_sample_query_rows(q.reshape(bs, ts, n_c), n_heads_c),
                     k_new=k_rows.reshape(bs, ts * group, dh), v_new=v_rows.reshape(bs, ts * group, dh),
                     cache_k=cache_k_rows, cache_v=cache_v_rows, page_table=page_table,
                     lam_p=diff_lambda[a], subln_w=diff_subln_w[a], page_base=a * n_phys, tl=ts,
                     n_heads=n_heads_c, lam_init=lam_init)
        n_lo = bs // 2

        o, s_a, s_b = _mixer_prompt(xp.reshape(bp, tp, d), mod_p[l], norm_w[l, 0], w_in_ab[e], cos_p, sin_p,
                                    hgrn_lb_logits, hgrn_norm_w[e], layer_slot=e, chunk=128,
                                    rows_per_step=min(4, bp))
        hg_p.append(s_a)
        rt_p.append(s_b)
        xp, attn_lo = _mlp(o.reshape(bp * tp, -1), xp, mod_p[l], norm_w[l, 1], w_out_ab[e], w_mlp_up[l],
                           w_mlp_down[l], final_norm_w, tm=tm_p, final_norm=False,
                           n_parts=PROMPT_MLP_PARTS, paged=dict(paged, first_seq=0, n_seq=n_lo))

        q, k, v, k_rows, v_rows = _qkv(xp, mod_p[l + 1], norm_w[l + 1, 0], w_in_c[a], tm=tm_p,
                                       q_scale=q_scale, n_heads=n_heads_c)
        k_p.append(k_rows)
        v_p.append(v_rows)
        q, k, v = (t.reshape(bp, tp, n_c) for t in (q, k, v))
        o = _attn_prompt(q, k, v, diff_lambda[a], diff_subln_w[a], lam_init=lam_init, tq=min(512, tp))
        xp, attn_hi = _mlp(o.reshape(bp * tp, -1), xp, mod_p[l + 1], norm_w[l + 1, 1], w_out_c[a],
                           w_mlp_up[l + 1], w_mlp_down[l + 1], final_norm_w, tm=tm_p,
                           final_norm=(l + 2 == depth), n_parts=PROMPT_MLP_PARTS,
                           paged=dict(paged, first_seq=n_lo, n_seq=bs - n_lo))

        o = jnp.concatenate([attn_lo, attn_hi], axis=0)
        o = jnp.swapaxes(o.reshape(bs, ts, 2, n_heads_c, dh)[:, :, :, ::-1, :], 2, 3)
        xs = _mlp(o.reshape(bs * ts, -1), xs, mod_s[l + 1], norm_w[l + 1, 1], w_out_c[a],
                  w_mlp_up[l + 1], w_mlp_down[l + 1], final_norm_w, tm=tm_s,
                  final_norm=(l + 2 == depth), n_parts=1)

    y_p = xp.reshape(bp, tp, d)
    y_s = xs.reshape(bs, ts, d)
    hg_p, rt_p, k_p, v_p, hg_s, rt_s, k_s, v_s = (
        jnp.stack(t) for t in (hg_p, rt_p, k_p, v_p, hg_s, rt_s, k_s, v_s))

    n_odd = k_p.shape[0]

    def values_from_stored_order(v, lead):
        v = jnp.swapaxes(v.reshape(n_odd, *lead, 2, n_heads_c, dh), -2, -3)
        return v.reshape(n_odd, *lead, n_heads_c, 2 * dh)

    k_p = k_p.reshape(n_odd, bp, tp // page, page, n_heads_c, 2, dh)
    v_p = values_from_stored_order(v_p, (bp, tp // page, page))
    k_s = k_s.reshape(n_odd, bs, ts, n_heads_c, 2, dh)
    v_s = values_from_stored_order(v_s, (bs, ts))
    return (y_p, y_s, hg_p, rt_p, k_p, v_p, hg_s, rt_s, k_s, v_s)
```

```python
import functools
import math

import jax
import jax.numpy as jnp
import numpy as np
from jax import lax
from jax.experimental import pallas as pl
from jax.experimental.pallas import tpu as pltpu

F32 = jnp.float32
BF16 = jnp.bfloat16

HEAD_DIM = 128
SUBLANES = 8
NORM_EPS = 1e-6
ROPE_BASE = 10000.0
V7X_VMEM_LIMIT = 52 * 1024 * 1024

_NT = (((1,), (1,)), ((), ()))
_TN = (((0,), (0,)), ((), ()))


def _params(*semantics):
    return pltpu.CompilerParams(dimension_semantics=semantics, vmem_limit_bytes=V7X_VMEM_LIMIT)


def _resident(shape):
    zeros = (0,) * len(shape)
    return pl.BlockSpec(shape, lambda *_: zeros, pipeline_mode=pl.Buffered(1))


def _sigmoid(x):
    return 0.5 + 0.5 * jnp.tanh(0.5 * x)


def _silu(x):
    half = 0.5 * x
    return half + half * jnp.tanh(half)


def _rms(x):
    return x * lax.rsqrt(jnp.mean(x * x, axis=-1, keepdims=True) + NORM_EPS)


def _mm(a, b):
    return jnp.dot(a, b, preferred_element_type=F32)


def _ada_kernel(c_ref, w_ref, b_ref, o_ref):
    s = _silu(c_ref[...]).astype(BF16)
    o_ref[0] = _mm(s, w_ref[0].astype(BF16)) + b_ref[0]


def _ada(c, w_ada, b_ada):
    depth, d, n = w_ada.shape
    r = c.shape[0]
    tn = n // 4
    return pl.pallas_call(
        _ada_kernel,
        grid=(depth, n // tn),
        in_specs=[pl.BlockSpec((r, d), lambda l, j: (0, 0)),
                  pl.BlockSpec((1, d, tn), lambda l, j: (l, 0, j)),
                  pl.BlockSpec((1, 1, tn), lambda l, j: (l, 0, j))],
        out_specs=pl.BlockSpec((1, r, tn), lambda l, j: (l, 0, j)),
        out_shape=jax.ShapeDtypeStruct((depth, r, n), F32),
        compiler_params=_params("arbitrary", "arbitrary"),
        name="ada_mod",
    )(c, w_ada, b_ada.reshape(depth, 1, n))


def _modulated_norm(x, nw_ref, mod_ref, shift_col, scale_col):
    d = x.shape[-1]
    shift = mod_ref[0, :, shift_col * d:(shift_col + 1) * d]
    scale = mod_ref[0, :, scale_col * d:(scale_col + 1) * d]
    return _rms(x) * nw_ref[...] * (1.0 + scale) + shift


def _inproj_kernel(x_ref, mod_ref, nw_ref, w_ref, *out_refs, shift_col, scale_col, out_scales, tc):
    h = _modulated_norm(x_ref[...], nw_ref, mod_ref, shift_col, scale_col).astype(BF16)
    off = 0
    for o_ref, out_scale in zip(out_refs, out_scales):
        width = o_ref.shape[-1]
        for c in range(0, width, tc):
            y = _mm(h, w_ref[:, off + c:off + c + tc])
            if out_scale != 1.0:
                y = y * out_scale
            o_ref[:, c:c + tc] = y.astype(o_ref.dtype)
        off += width


def _inproj(x, mod, nw, w, outs, *, tm, shift_col, scale_col):
    m, d = x.shape
    n = w.shape[1]
    rows_per_mod = m // mod.shape[0]
    tiles_per_mod = rows_per_mod // tm
    kern = functools.partial(_inproj_kernel, shift_col=shift_col, scale_col=scale_col,
                             out_scales=tuple(o[2] for o in outs), tc=512)
    return pl.pallas_call(
        kern,
        grid=(m // tm,),
        in_specs=[pl.BlockSpec((tm, d), lambda i: (i, 0)),
                  pl.BlockSpec((1, mod.shape[1], mod.shape[2]), lambda i: (i // tiles_per_mod, 0, 0)),
                  _resident((1, d)),
                  _resident((d, n))],
        out_specs=[pl.BlockSpec((tm, o[0]), lambda i: (i, 0)) for o in outs],
        out_shape=[jax.ShapeDtypeStruct((m, o[0]), o[1]) for o in outs],
        compiler_params=_params("parallel"),
        name="norm_mod_inproj",
    )(x, mod, nw.reshape(1, d), w)


def _qkv_kernel(x_ref, mod_ref, nw_ref, w_ref, q_ref, kb_ref, vb_ref, kr_ref, vr_ref, *,
                q_scale, n_heads, tc):
    h = _modulated_norm(x_ref[...], nw_ref, mod_ref, 0, 1).astype(BF16)
    tm, n_c = q_ref.shape
    lanes = HEAD_DIM
    group = 2 * n_heads
    for c in range(0, n_c, tc):
        q_ref[:, c:c + tc] = (_mm(h, w_ref[:, c:c + tc]) * q_scale).astype(q_ref.dtype)
    for c in range(0, n_c, tc):
        y = _mm(h, w_ref[:, n_c + c:n_c + c + tc])
        kb_ref[:, c:c + tc] = y.astype(kb_ref.dtype)
        for i in range(tc // lanes):
            kr_ref[pl.ds(c // lanes + i, tm, stride=group), :] = y[:, i * lanes:(i + 1) * lanes]
    for c in range(0, n_c, tc):
        y = _mm(h, w_ref[:, 2 * n_c + c:2 * n_c + c + tc])
        vb_ref[:, c:c + tc] = y.astype(vb_ref.dtype)
        for i in range(tc // lanes):
            head, half = divmod(c // lanes + i, 2)
            vr_ref[pl.ds(half * n_heads + head, tm, stride=group), :] = y[:, i * lanes:(i + 1) * lanes]


def _qkv(x, mod, nw, w, *, tm, q_scale, n_heads):
    m, d = x.shape
    n_c = w.shape[1] // 3
    group = 2 * n_heads
    tiles_per_mod = (m // mod.shape[0]) // tm
    kern = functools.partial(_qkv_kernel, q_scale=q_scale, n_heads=n_heads, tc=512)
    cols = pl.BlockSpec((tm, n_c), lambda i: (i, 0))
    rows = pl.BlockSpec((tm * group, HEAD_DIM), lambda i: (i, 0))
    return pl.pallas_call(
        kern,
        grid=(m // tm,),
        in_specs=[pl.BlockSpec((tm, d), lambda i: (i, 0)),
                  pl.BlockSpec((1, mod.shape[1], mod.shape[2]), lambda i: (i // tiles_per_mod, 0, 0)),
                  _resident((1, d)),
                  _resident((d, 3 * n_c))],
        out_specs=[cols, cols, cols, rows, rows],
        out_shape=[jax.ShapeDtypeStruct((m, n_c), BF16)] * 3
                  + [jax.ShapeDtypeStruct((m * group, HEAD_DIM), F32)] * 2,
        compiler_params=_params("parallel"),
        name="norm_mod_qkv",
    )(x, mod, nw.reshape(1, d), w)


N_MLP_INPUTS = 8
N_ATTENTION_INPUTS = 7


def _mlp_kernel(*refs, n_parts, final_norm, attention):
    if attention is None:
        o_ref, x_ref, mod_ref, nw_ref, wo_ref, wu_ref, wd_ref, fw_ref, out_ref, h_ref, acc_ref = refs
    else:
        pt_ref, refs = refs[0], refs[1:]
        o_ref, x_ref, mod_ref, nw_ref, wo_ref, wu_ref, wd_ref, fw_ref = refs[:N_MLP_INPUTS]
        attention_in = refs[N_MLP_INPUTS:N_MLP_INPUTS + N_ATTENTION_INPUTS]
        out_ref, attention_out, h_ref, acc_ref = refs[N_MLP_INPUTS + N_ATTENTION_INPUTS:][:4]
        attention_refs = (pt_ref, *attention_in, attention_out,
                          *refs[N_MLP_INPUTS + N_ATTENTION_INPUTS + 4:])
    d = x_ref.shape[-1]
    cw = wu_ref.shape[1] // n_parts

    def mod_col(col):
        return mod_ref[0, :, col * d:(col + 1) * d]

    hidden = [None] * n_parts

    def mlp_up(r):
        if r == 0:
            x1 = x_ref[...] + mod_col(2) * _mm(o_ref[...].astype(BF16), wo_ref[...])
            out_ref[...] = x1
            h_ref[...] = _modulated_norm(x1, nw_ref, mod_ref, 3, 4).astype(BF16)
        u = jnp.maximum(_mm(h_ref[...], wu_ref[:, r * cw:(r + 1) * cw]), 0.0)
        hidden[r] = (u * u).astype(BF16)

    def mlp_down(r):
        part = _mm(hidden[r], wd_ref[r * cw:(r + 1) * cw, :])
        acc_ref[...] = part if r == 0 else acc_ref[...] + part

    def finish():
        y = out_ref[...] + mod_col(5) * acc_ref[...]
        if final_norm:
            y = _rms(y) * fw_ref[...]
        out_ref[...] = y

    if attention is None:
        for r in range(n_parts):
            mlp_up(r)
            mlp_down(r)
        finish()
    else:
        assert n_parts % PAGE_FETCH_STEPS == 0
        aligned = attention["steps_per_seq"] % n_parts == 0
        mlp_up(0)
        for r in range(n_parts):
            last = r == n_parts - 1
            _paged_attention_step(pl.program_id(0) * n_parts + r, pl.num_programs(0) * n_parts,
                                  attention_refs,
                                  (functools.partial(mlp_down, r),
                                   finish if last else functools.partial(mlp_up, r + 1),
                                   lambda: None),
                                  sub_step=r % PAGE_FETCH_STEPS,
                                  may_open=(r == 0 or not aligned),
                                  may_close=(last or not aligned), **attention)


def _mlp(o, x, mod, nw, w_out, w_up, w_down, final_w, *, tm, final_norm, n_parts, paged=None):
    m, d = x.shape
    k = o.shape[1]
    dff = w_up.shape[1]
    n_tiles = m // tm
    tiles_per_mod = (m // mod.shape[0]) // tm
    in_specs = [pl.BlockSpec((tm, k), lambda i, *_: (i, 0)),
                pl.BlockSpec((tm, d), lambda i, *_: (i, 0)),
                pl.BlockSpec((1, mod.shape[1], mod.shape[2]), lambda i, *_: (i // tiles_per_mod, 0, 0)),
                _resident((1, d)),
                _resident((k, d)),
                _resident((d, dff)),
                _resident((dff, d)),
                _resident((1, d))]
    operands = [o, x, mod, nw.reshape(1, d), w_out, w_up, w_down, final_w.reshape(1, d)]
    out_specs = pl.BlockSpec((tm, d), lambda i, *_: (i, 0))
    out_shape = jax.ShapeDtypeStruct((m, d), F32)
    scratch = [pltpu.VMEM((tm, d), BF16), pltpu.VMEM((tm, d), F32)]
    if paged is None:
        kern = functools.partial(_mlp_kernel, n_parts=n_parts, final_norm=final_norm, attention=None)
        return pl.pallas_call(
            kern, grid=(n_tiles,), in_specs=in_specs, out_specs=out_specs, out_shape=out_shape,
            scratch_shapes=scratch, compiler_params=_params("arbitrary"), name="outproj_mlp_residual",
        )(*operands)

    q_rows, k_new, v_new = paged["q_rows"], paged["k_new"], paged["v_new"]
    cache_k, cache_v, page_table = paged["cache_k"], paged["cache_v"], paged["page_table"]
    n_seq = paged["n_seq"]
    n_rows, dh = q_rows.shape[1:]
    n_pages = page_table.shape[1]
    page_rows = cache_k.shape[1]
    assert 2 * paged["n_heads"] == SUBLANES, "one (map, token) block of query rows must fill one vreg"
    pages_per_step = (n_seq * n_pages) // (n_tiles * n_parts)
    assert pages_per_step * n_tiles * n_parts == n_seq * n_pages and n_pages % pages_per_step == 0
    attention = dict(pages_per_step=pages_per_step, page_base=paged["page_base"],
                     steps_per_seq=n_pages // pages_per_step, first_seq=paged["first_seq"],
                     tl=paged["tl"], n_heads=paged["n_heads"], lam_init=paged["lam_init"])
    kern = functools.partial(_mlp_kernel, n_parts=n_parts, final_norm=final_norm, attention=attention)
    ring = (PAGE_RING_DEPTH, PAGE_FETCH_STEPS * pages_per_step, page_rows, dh)
    grid_spec = pltpu.PrefetchScalarGridSpec(
        num_scalar_prefetch=1,
        grid=(n_tiles,),
        in_specs=in_specs + [_resident(paged["lam_p"].shape), _resident((1, 2 * dh)),
                             _resident(q_rows.shape), _resident(k_new.shape), _resident(v_new.shape),
                             pl.BlockSpec(memory_space=pl.ANY), pl.BlockSpec(memory_space=pl.ANY)],
        out_specs=[out_specs, pl.BlockSpec((n_seq, n_rows // 2, dh), lambda i, *_: (0, 0, 0))],
        scratch_shapes=scratch + [pltpu.VMEM((n_rows, dh), F32), pltpu.VMEM((n_rows, dh), F32),
                                  pltpu.VMEM((n_rows, dh), F32),
                                  pltpu.VMEM((dh, dh), F32), pltpu.VMEM((dh, dh), F32),
                                  pltpu.VMEM(ring, F32), pltpu.VMEM(ring, F32),
                                  pltpu.SemaphoreType.DMA((2, PAGE_RING_DEPTH))],
    )
    return pl.pallas_call(
        kern,
        grid_spec=grid_spec,
        out_shape=[out_shape, jax.ShapeDtypeStruct((n_seq, n_rows // 2, dh), F32)],
        compiler_params=_params("arbitrary"),
        name="outproj_mlp_paged_attention",
    )(page_table.reshape(-1), *operands, paged["lam_p"], paged["subln_w"].reshape(1, 2 * dh),
      q_rows, k_new, v_new, cache_k, cache_v)


def _hgrn_lower_bound(logits_ref, layer_slot):
    lg = logits_ref[...]
    ex = jnp.exp(lg - jnp.max(lg, axis=0, keepdims=True))
    pr = ex / jnp.sum(ex, axis=0, keepdims=True)
    return jnp.sum(pr[:layer_slot + 1], axis=0, keepdims=True)


def _rotate_pairs(x, cos, sin_signed):
    n = x.shape[-1]
    lane = lax.broadcasted_iota(jnp.int32, x.shape, 1)
    swapped = jnp.where((lane & 1) == 0, pltpu.roll(x, n - 1, 1), pltpu.roll(x, 1, 1))
    return x * cos + swapped * sin_signed


def _cumsum_rows(x):
    n = x.shape[0]
    row = lax.broadcasted_iota(jnp.int32, x.shape, 0)
    d = 1
    while d < n:
        x = x + jnp.where(row >= d, pltpu.roll(x, d, 0), 0.0)
        d *= 2
    return x


def _split_row(g, b):
    n, lanes = g.shape
    if 2 * b >= SUBLANES:
        g3 = g.reshape(n // (2 * b), 2 * b, lanes)
        return jnp.broadcast_to(g3[:, b - 1:b, :], g3.shape).reshape(n, lanes)
    rmod = lax.broadcasted_iota(jnp.int32, g.shape, 0) & (2 * b - 1)
    out = g
    for r in range(2 * b):
        if r != b - 1:
            out = jnp.where(rmod == r, pltpu.roll(g, (r - (b - 1)) % n, 0), out)
    return out


def _level_matrix(n):
    t = np.arange(n)[:, None]
    s = np.arange(n)[None, :]
    hi = np.floor(np.log2(np.maximum(t ^ s, 1))).astype(np.int32)
    return jnp.asarray(np.where(s < t, hi, -1).astype(np.int32))


def _hgrn_chunk(q, k, v, g2, lv, st):
    n = q.shape[0]
    row = lax.broadcasted_iota(jnp.int32, q.shape, 0)
    ri = lax.broadcasted_iota(jnp.int32, (n, n), 0)
    ci = lax.broadcasted_iota(jnp.int32, (n, n), 1)
    a = jnp.where(ri == ci, jnp.sum(q * k, axis=-1, keepdims=True), 0.0)
    b, level = 1, 0
    while b < n:
        if b >= SUBLANES:
            parts = []
            for lo in range(0, n, 2 * b):
                edge = g2[lo + b - 1:lo + b, :]
                parts.append(k[lo:lo + b] * jnp.exp2(edge - g2[lo:lo + b]))
                parts.append(q[lo + b:lo + 2 * b] * jnp.exp2(g2[lo + b:lo + 2 * b] - edge))
            z = jnp.concatenate(parts, axis=0).astype(BF16)
        else:
            is_query = (row & b) != 0
            sign = jnp.where(is_query, 1.0, -1.0)
            z = (jnp.where(is_query, q, k) * jnp.exp2((g2 - _split_row(g2, b)) * sign)).astype(BF16)
        a = jnp.where(lv == level, lax.dot_general(z, z, _NT, preferred_element_type=F32), a)
        b, level = 2 * b, level + 1
    carried = lax.dot_general((q * jnp.exp2(g2)).astype(BF16), st.astype(BF16), _NT,
                              preferred_element_type=F32)
    o = _mm(a.astype(BF16), v.astype(BF16)) + carried
    g_end = g2[n - 1:n, :]
    k_dec = (k * jnp.exp2(g_end - g2)).astype(BF16)
    st_new = st * jnp.exp2(g_end) + lax.dot_general(v.astype(BF16), k_dec, _TN,
                                                    preferred_element_type=F32)
    return o, st_new


def _retention_chunk(q, k, v, log_gamma, st):
    n = q.shape[0]
    rowf = lax.broadcasted_iota(jnp.int32, q.shape, 0).astype(F32)
    ri = lax.broadcasted_iota(jnp.int32, (n, n), 0)
    ci = lax.broadcasted_iota(jnp.int32, (n, n), 1)
    decay = jnp.where(ci <= ri, jnp.exp((ri - ci).astype(F32) * log_gamma), 0.0)
    a = lax.dot_general(q.astype(BF16), k.astype(BF16), _NT, preferred_element_type=F32) * decay
    q_dec = (q * jnp.exp((rowf + 1.0) * log_gamma)).astype(BF16)
    o = _mm(a.astype(BF16), v.astype(BF16)) + _mm(q_dec, st.astype(BF16))
    k_dec = (k * jnp.exp((n - 1.0 - rowf) * log_gamma)).astype(BF16)
    st_new = st * math.exp(n * log_gamma) + lax.dot_general(k_dec, v.astype(BF16), _TN,
                                                            preferred_element_type=F32)
    return o, st_new


def _retention_log_gamma(h):
    return math.log(1.0 - 2.0 ** (-5.0 - h))


def _mixer_prompt_kernel(proj_ref, cos_ref, sin_ref, lv_ref, lbl_ref, na_ref,
                         o_ref, sa_out, sb_out, sat_ref, sb_ref, *, layer_slot, n_heads):
    t = pl.program_id(1)
    hd = HEAD_DIM
    w = n_heads * hd

    @pl.when(t == 0)
    def _():
        sat_ref[...] = jnp.zeros_like(sat_ref)
        sb_ref[...] = jnp.zeros_like(sb_ref)

    lb = _hgrn_lower_bound(lbl_ref, layer_slot)
    lv = lv_ref[...]
    cos = cos_ref[...]
    sin = sin_ref[...]
    for r in range(proj_ref.shape[0]):
        def cols(group, h):
            return proj_ref[r, :, group * w + h * hd:group * w + (h + 1) * hd]

        fz = lb + (1.0 - lb) * _sigmoid(proj_ref[r, :, w:2 * w])
        g_all = _cumsum_rows(jnp.log2(fz))
        k_all = 1.0 - fz
        for h in range(n_heads):
            sl = slice(h * hd, (h + 1) * hd)
            q = _silu(cols(0, h)) * hd ** -0.5
            o, st = _hgrn_chunk(q, k_all[:, sl], cols(2, h), g_all[:, sl], lv, sat_ref[r, h])
            sat_ref[r, h] = st
            o_ref[r, :, sl] = (_rms(o) * na_ref[...] * _silu(cols(3, h))).astype(o_ref.dtype)
        for h in range(n_heads):
            q = _rotate_pairs(cols(4, h), cos, sin)
            k = _rotate_pairs(cols(5, h), cos, sin) * hd ** -0.5
            o, st = _retention_chunk(q, k, cols(6, h), _retention_log_gamma(h), sb_ref[r, h])
            sb_ref[r, h] = st
            o_ref[r, :, w + h * hd:w + (h + 1) * hd] = (_rms(o) * _silu(cols(7, h))).astype(o_ref.dtype)

    @pl.when(t == pl.num_programs(1) - 1)
    def _():
        for r in range(proj_ref.shape[0]):
            for h in range(n_heads):
                sa_out[r, h] = sat_ref[r, h].T
                sb_out[r, h] = sb_ref[r, h]


def _mixer_prompt(proj, cos, sin, lb_logits, norm_a, *, layer_slot, chunk, rows_per_step):
    bsz, tlen, n = proj.shape
    hd = HEAD_DIM
    n_heads = n // (8 * hd)
    rb = rows_per_step
    kern = functools.partial(_mixer_prompt_kernel, layer_slot=layer_slot, n_heads=n_heads)
    state = jax.ShapeDtypeStruct((bsz, n_heads, hd, hd), F32)
    state_spec = pl.BlockSpec((rb, n_heads, hd, hd), lambda b, t: (b, 0, 0, 0))
    return pl.pallas_call(
        kern,
        grid=(bsz // rb, tlen // chunk),
        in_specs=[pl.BlockSpec((rb, chunk, n), lambda b, t: (b, t, 0)),
                  pl.BlockSpec((chunk, hd), lambda b, t: (t, 0)),
                  pl.BlockSpec((chunk, hd), lambda b, t: (t, 0)),
                  pl.BlockSpec((chunk, chunk), lambda b, t: (0, 0)),
                  pl.BlockSpec(lb_logits.shape, lambda b, t: (0, 0)),
                  pl.BlockSpec((1, hd), lambda b, t: (0, 0))],
        out_specs=[pl.BlockSpec((rb, chunk, 2 * n_heads * hd), lambda b, t: (b, t, 0)),
                   state_spec, state_spec],
        out_shape=[jax.ShapeDtypeStruct((bsz, tlen, 2 * n_heads * hd), BF16), state, state],
        scratch_shapes=[pltpu.VMEM((rb, n_heads, hd, hd), F32), pltpu.VMEM((rb, n_heads, hd, hd), F32)],
        compiler_params=_params("arbitrary", "arbitrary"),
        name="mixer_prompt",
    )(proj, cos, sin, _level_matrix(chunk), lb_logits, norm_a.reshape(1, hd))


def _pair_chunk(q, k, v, g, s0, dec, kpad_ref, vpad_ref):
    tl = q.shape[0]
    row = lax.broadcasted_iota(jnp.int32, q.shape, 0)
    o = _mm((q * jnp.exp(g)).astype(BF16), s0.astype(BF16))
    for s in range(tl):
        live = row >= s
        w = jnp.exp(jnp.where(live, g - g[s:s + 1, :], 0.0))
        a = jnp.sum(jnp.where(live, q * w * k[s:s + 1, :], 0.0), axis=-1, keepdims=True)
        o = o + a * v[s:s + 1, :]
    kpad_ref[0:tl, :] = k * jnp.exp(g[tl - 1:tl, :] - g)
    vpad_ref[0:tl, :] = v
    upd = lax.dot_general(kpad_ref[...].astype(BF16), vpad_ref[...].astype(BF16), _TN,
                          preferred_element_type=F32)
    return o, s0 * dec + upd


def _mixer_sample_kernel(proj_ref, cos_ref, sin_ref, lbl_ref, na_ref, sa_ref, sb_ref,
                         o_ref, sa_out, sb_out, kpad_ref, vpad_ref, *, layer_slot, n_heads):
    hd = HEAD_DIM
    w = n_heads * hd
    tl = proj_ref.shape[1]
    kpad_ref[...] = jnp.zeros_like(kpad_ref)
    vpad_ref[...] = jnp.zeros_like(vpad_ref)

    def cols(group, h):
        return proj_ref[0, :, group * w + h * hd:group * w + (h + 1) * hd]

    lb = _hgrn_lower_bound(lbl_ref, layer_slot)
    fz = lb + (1.0 - lb) * _sigmoid(proj_ref[0, :, w:2 * w])
    lf = jnp.log(fz)
    row = lax.broadcasted_iota(jnp.int32, lf.shape, 0)
    g_all = jnp.zeros_like(lf)
    for j in range(tl):
        g_all = g_all + jnp.where(row >= j, lf[j:j + 1, :], 0.0)
    k_all = 1.0 - fz
    for h in range(n_heads):
        sl = slice(h * hd, (h + 1) * hd)
        g = g_all[:, sl]
        q = _silu(cols(0, h)) * hd ** -0.5
        dec = jnp.broadcast_to(jnp.exp(g[tl - 1:tl, :]), (hd, hd)).T
        o, st = _pair_chunk(q, k_all[:, sl], cols(2, h), g, sa_ref[0, h], dec, kpad_ref, vpad_ref)
        sa_out[0, h] = st
        o_ref[0, :, sl] = _rms(o) * na_ref[...] * _silu(cols(3, h))

    cos = cos_ref[...]
    sin = sin_ref[...]
    rowf = lax.broadcasted_iota(jnp.int32, (tl, hd), 0).astype(F32)
    for h in range(n_heads):
        lg = _retention_log_gamma(h)
        q = _rotate_pairs(cols(4, h), cos, sin)
        k = _rotate_pairs(cols(5, h), cos, sin) * hd ** -0.5
        o, st = _pair_chunk(q, k, cols(6, h), (rowf + 1.0) * lg, sb_ref[0, h], math.exp(tl * lg),
                            kpad_ref, vpad_ref)
        sb_out[0, h] = st
        o_ref[0, :, w + h * hd:w + (h + 1) * hd] = _rms(o) * _silu(cols(7, h))


def _mixer_sample(proj, cos, sin, lb_logits, norm_a, state_a, state_b, *, layer_slot):
    bsz, tl, n = proj.shape
    hd = HEAD_DIM
    n_heads = n // (8 * hd)
    kern = functools.partial(_mixer_sample_kernel, layer_slot=layer_slot, n_heads=n_heads)
    state = jax.ShapeDtypeStruct((bsz, n_heads, hd, hd), F32)
    state_spec = pl.BlockSpec((1, n_heads, hd, hd), lambda b: (b, 0, 0, 0))
    return pl.pallas_call(
        kern,
        grid=(bsz,),
        in_specs=[pl.BlockSpec((1, tl, n), lambda b: (b, 0, 0)),
                  pl.BlockSpec((tl, hd), lambda b: (0, 0)),
                  pl.BlockSpec((tl, hd), lambda b: (0, 0)),
                  pl.BlockSpec(lb_logits.shape, lambda b: (0, 0)),
                  pl.BlockSpec((1, hd), lambda b: (0, 0)),
                  state_spec, state_spec],
        out_specs=[pl.BlockSpec((1, tl, 2 * n_heads * hd), lambda b: (b, 0, 0)), state_spec, state_spec],
        out_shape=[jax.ShapeDtypeStruct((bsz, tl, 2 * n_heads * hd), F32), state, state],
        scratch_shapes=[pltpu.VMEM((hd, hd), F32), pltpu.VMEM((hd, hd), F32)],
        compiler_params=_params("arbitrary"),
        name="mixer_sample",
    )(proj, cos, sin, lb_logits, norm_a.reshape(1, hd), state_a, state_b)


def _diff_lambda(lam_ref, lam_init):
    lp = lam_ref[...]
    return (jnp.exp(jnp.sum(lp[0:1] * lp[1:2], axis=-1, keepdims=True))
            - jnp.exp(jnp.sum(lp[2:3] * lp[3:4], axis=-1, keepdims=True)) + lam_init)


LOG2_E = math.log2(math.e)


def _attn_prompt_kernel(lam_ref, sw_ref, q_ref, k_ref, v_ref, o_ref, m_ref, l_ref, acc_ref, *,
                        tq, lam_init):
    dh = HEAD_DIM
    lanes = m_ref.shape[-1]
    n_tiles = q_ref.shape[1] // tq
    m_ref[...] = jnp.full_like(m_ref, -jnp.inf)
    l_ref[...] = jnp.zeros_like(l_ref)
    acc_ref[...] = jnp.zeros_like(acc_ref)

    def tile_pair(q0, nq, k0, nk):
        rq = slice(q0, q0 + nq)
        rk = slice(k0, k0 + nk)
        v = v_ref[0, rk, :]
        scores = []
        for j in range(2):
            s = lax.dot_general(q_ref[0, rq, j * dh:(j + 1) * dh], k_ref[0, rk, j * dh:(j + 1) * dh],
                                _NT, preferred_element_type=F32)
            if k0 + nk - 1 > q0:
                ri = lax.broadcasted_iota(jnp.int32, s.shape, 0)
                ci = lax.broadcasted_iota(jnp.int32, s.shape, 1)
                s = jnp.where(ci - ri <= q0 - k0, s, -jnp.inf)
            scores.append(s)
        probs, alphas = [], []
        for j in range(2):
            s = scores[j]
            m_old = m_ref[j, rq]
            m_new = jnp.maximum(m_old, jnp.max(s, axis=-1, keepdims=True))
            alpha = jnp.exp2(m_old - m_new)
            l_new = alpha * l_ref[j, rq]
            tiles = []
            for c in range(0, s.shape[1], lanes):
                p = jnp.exp2(s[:, c:c + lanes] - m_new)
                l_new = l_new + p
                tiles.append(p.astype(BF16))
            l_ref[j, rq] = l_new
            m_ref[j, rq] = m_new
            probs.append(jnp.concatenate(tiles, axis=1))
            alphas.append(alpha)
        for j in range(2):
            pv = _mm(probs[j], v)
            for c in range(0, pv.shape[1], lanes):
                acc_ref[j, rq, c:c + lanes] = (alphas[j] * acc_ref[j, rq, c:c + lanes]
                                               + pv[:, c:c + lanes])

    for kj in range(n_tiles):
        for qi in range(kj, n_tiles):
            tile_pair(qi * tq, tq, kj * tq, tq)

    lam = _diff_lambda(lam_ref, lam_init)
    for qi in range(n_tiles):
        rq = slice(qi * tq, (qi + 1) * tq)
        l0 = jnp.sum(l_ref[0, rq], axis=-1, keepdims=True)
        l1 = jnp.sum(l_ref[1, rq], axis=-1, keepdims=True)
        o = acc_ref[0, rq] / l0 - lam * (acc_ref[1, rq] / l1)
        o_ref[0, rq] = (_rms(o) * sw_ref[...] * (1.0 - lam_init)).astype(o_ref.dtype)


def _attn_prompt(q, k, v, lam_p, subln_w, *, lam_init, tq):
    bsz, tlen, n = q.shape
    e = 2 * HEAD_DIM
    n_heads = n // e
    kern = functools.partial(_attn_prompt_kernel, tq=tq, lam_init=lam_init)
    head_cols = pl.BlockSpec((1, tlen, e), lambda b, h: (b, 0, h))
    return pl.pallas_call(
        kern,
        grid=(bsz, n_heads),
        in_specs=[pl.BlockSpec(lam_p.shape, lambda b, h: (0, 0)),
                  pl.BlockSpec((1, e), lambda b, h: (0, 0)),
                  head_cols, head_cols, head_cols],
        out_specs=head_cols,
        out_shape=jax.ShapeDtypeStruct((bsz, tlen, n), BF16),
        scratch_shapes=[pltpu.VMEM((2, tlen, HEAD_DIM), F32), pltpu.VMEM((2, tlen, HEAD_DIM), F32),
                        pltpu.VMEM((2, tlen, e), F32)],
        compiler_params=_params("parallel", "parallel"),
        name="diff_attn_prompt",
    )(lam_p, subln_w.reshape(1, e), q, k, v)


def _sample_row_fields(shape, tl, n_heads):
    r = lax.broadcasted_iota(jnp.int32, shape, 0)
    head = n_heads - 1 - r % n_heads
    half = (r // n_heads) % 2
    token = (r // (2 * n_heads)) % tl
    qmap = r // (2 * n_heads * tl)
    return head, qmap, token, half


PAGE_RING_DEPTH = 2
PAGE_FETCH_STEPS = 2


def _paged_attention_step(step, n_steps, refs, overlapped_work, *, pages_per_step, page_base,
                          steps_per_seq, first_seq, tl, n_heads, lam_init, may_open, may_close,
                          sub_step):
    (pt_ref, lam_ref, sw_ref, q_ref, kn_ref, vn_ref, ck_hbm, cv_hbm, o_ref,
     m_ref, l_ref, acc_ref, kpad_ref, vpad_ref, kbuf, vbuf, sems) = refs
    seq = step // steps_per_seq
    p = step % steps_per_seq
    group = 2 * n_heads
    n_rows = q_ref.shape[1]
    rows_per_map = n_rows // 2

    fetch_pages = PAGE_FETCH_STEPS * pages_per_step
    fetch = step // PAGE_FETCH_STEPS
    n_fetches = n_steps // PAGE_FETCH_STEPS

    def page_copies(of_fetch, slot):
        out = []
        for i in range(fetch_pages):
            page = page_base + pt_ref[first_seq * steps_per_seq * pages_per_step
                                      + of_fetch * fetch_pages + i]
            out.append(pltpu.make_async_copy(ck_hbm.at[page], kbuf.at[slot, i], sems.at[0, slot]))
            out.append(pltpu.make_async_copy(cv_hbm.at[page], vbuf.at[slot, i], sems.at[1, slot]))
        return out

    def start_fetch(of_fetch):
        for cp in page_copies(of_fetch, of_fetch % PAGE_RING_DEPTH):
            cp.start()

    slot = fetch % PAGE_RING_DEPTH
    if sub_step == 0:
        ahead = PAGE_RING_DEPTH - 1

        @pl.when(fetch == 0)
        def _():
            for f in range(ahead):
                @pl.when(f < n_fetches)
                def _():
                    start_fetch(f)

        @pl.when(fetch + ahead < n_fetches)
        def _():
            start_fetch(fetch + ahead)

        for cp in page_copies(fetch, slot):
            cp.wait()

    if may_open:
        @pl.when(p == 0)
        def _():
            m_ref[...] = jnp.full_like(m_ref, -jnp.inf)
            l_ref[...] = jnp.zeros_like(l_ref)
            acc_ref[...] = jnp.zeros_like(acc_ref)

    q = q_ref[first_seq + seq]

    def own_lanes(n_keys, causal):
        head, qmap, token, _ = _sample_row_fields((n_rows, n_keys), tl, n_heads)
        key = lax.broadcasted_iota(jnp.int32, (n_rows, n_keys), 1)
        own = (key % group) == 2 * head + qmap
        return jnp.logical_and(own, key // group <= token) if causal else own

    def partial_softmax(pages, causal, hooks=(None, None)):
        before_scores, after_scores = hooks
        own = own_lanes(pages[0][0].shape[0], causal)
        keys = [k_rows.astype(BF16) for k_rows, _ in pages]
        if before_scores is not None:
            before_scores()
        scores = [lax.dot_general(q, k, _NT, preferred_element_type=F32) for k in keys]
        if after_scores is not None:
            after_scores()
        stats, probs = [], []
        for s in scores:
            s = jnp.where(own, s, -jnp.inf)
            m = jnp.max(s, axis=-1, keepdims=True)
            pexp = jnp.exp2(s - m)
            stats.append((m, jnp.sum(pexp, axis=-1, keepdims=True)))
            aligned = []
            for blk in range(n_rows // SUBLANES):
                shift = group - (n_heads - 1) - (blk * SUBLANES) // rows_per_map
                aligned.append(pltpu.roll(pexp[blk * SUBLANES:(blk + 1) * SUBLANES, :], shift, 1,
                                          stride=1, stride_axis=0))
            probs.append(jnp.concatenate(aligned, axis=0).astype(BF16))
        return [(m, l, _mm(p_rows, pltpu.roll(v_rows, group, 0).astype(BF16)))
                for (m, l), p_rows, (_, v_rows) in zip(stats, probs, pages)]

    def merge(parts):
        m_run = m_ref[...]
        m_new = m_run
        for m, _, _ in parts:
            m_new = jnp.maximum(m_new, m)
        alpha = jnp.exp2(m_run - m_new)
        l_new = alpha * l_ref[...]
        acc_new = alpha * acc_ref[...]
        for m, l, o in parts:
            wgt = jnp.exp2(m - m_new)
            l_new = l_new + wgt * l
            acc_new = acc_new + wgt * o
        m_ref[...] = m_new
        l_ref[...] = l_new
        acc_ref[...] = acc_new

    before_scores, after_scores, after_values = overlapped_work
    first_page = sub_step * pages_per_step
    parts = partial_softmax([(kbuf[slot, first_page + i], vbuf[slot, first_page + i])
                             for i in range(pages_per_step)], False,
                            (before_scores, after_scores))
    after_values()
    merge(parts)

    if may_close:
        @pl.when(p == steps_per_seq - 1)
        def _():
            n_new = kn_ref.shape[1]
            kpad_ref[...] = jnp.zeros_like(kpad_ref)
            vpad_ref[...] = jnp.zeros_like(vpad_ref)
            kpad_ref[0:n_new, :] = kn_ref[first_seq + seq]
            vpad_ref[0:n_new, :] = vn_ref[first_seq + seq]
            merge(partial_softmax([(kpad_ref[...], vpad_ref[...])], True))

            o = (acc_ref[0:rows_per_map, :] / l_ref[0:rows_per_map, :]
                 - _diff_lambda(lam_ref, lam_init) * (acc_ref[rows_per_map:, :] / l_ref[rows_per_map:, :]))
            _, _, _, half = _sample_row_fields(o.shape, tl, n_heads)
            ss = jnp.broadcast_to(jnp.sum(o * o, axis=-1, keepdims=True), o.shape)
            other = jnp.where(half == 0, pltpu.roll(ss, rows_per_map - n_heads, 0),
                              pltpu.roll(ss, n_heads, 0))
            inv = lax.rsqrt((ss + other) / (2 * HEAD_DIM) + NORM_EPS)
            sw = jnp.where(half == 0, sw_ref[:, 0:HEAD_DIM], sw_ref[:, HEAD_DIM:])
            o_ref[seq] = o * inv * sw * (1.0 - lam_init)


def _sample_query_rows(q, n_heads):
    bsz, tl, _ = q.shape
    r = np.arange(4 * n_heads * tl)
    head = n_heads - 1 - r % n_heads
    token = (r // (2 * n_heads)) % tl
    qmap = r // (2 * n_heads * tl)
    return q.reshape(bsz, tl, n_heads, 2, HEAD_DIM)[:, token, head, qmap, :]


def _values_in_stored_order(v, n_heads):
    lead = v.shape[:-2]
    tokens = v.shape[-2]
    v = v.reshape(*lead, tokens, n_heads, 2, HEAD_DIM)
    return jnp.swapaxes(v, -2, -3).reshape(*lead, tokens * 2 * n_heads, HEAD_DIM)


def _rope_tables(pos0, tlen):
    inv = 1.0 / (ROPE_BASE ** jnp.linspace(0.0, 1.0, HEAD_DIM // 2, dtype=F32))
    ang = (pos0 + jnp.arange(tlen)).astype(F32)[:, None] * inv[None, :]
    sign = jnp.tile(jnp.asarray([-1.0, 1.0], F32), HEAD_DIM // 2)
    return jnp.repeat(jnp.cos(ang), 2, axis=1), jnp.repeat(jnp.sin(ang), 2, axis=1) * sign


PROMPT_ROW_TILE = 512
PROMPT_MLP_PARTS = 8


def kernel(x_prompt, x_sample, state_hgrn, state_ret, cache_k, cache_v, page_table, c_prompt, c_sample,
           w_ada, b_ada, norm_w, w_in_ab, w_out_ab, hgrn_lb_logits, hgrn_norm_w, w_in_c, w_out_c,
           diff_lambda, diff_subln_w, w_mlp_up, w_mlp_down, final_norm_w):
    bp, tp, d = x_prompt.shape
    bs, ts, _ = x_sample.shape
    n_slots, n_phys, page, n_heads_c, _, dh = cache_k.shape

    mod = _ada(jnp.concatenate([c_prompt, c_sample], axis=0), w_ada, b_ada)
    depth = mod.shape[0]
    assert depth % 2 == 0, "layers come in (recurrent mixer, attention) pairs"
    mod_p = mod[:, :bp].reshape(depth, bp, 1, 6 * d)
    mod_s = jnp.repeat(mod[:, bp:], ts, axis=1).reshape(depth, 1, bs * ts, 6 * d)

    w_in_ab, w_out_ab, w_in_c, w_out_c, w_mlp_up, w_mlp_down = (
        t.astype(BF16) for t in (w_in_ab, w_out_ab, w_in_c, w_out_c, w_mlp_up, w_mlp_down))
    n_in = w_in_ab.shape[2]
    n_c = w_in_c.shape[2] // 3
    group = 2 * n_heads_c
    q_scale = dh ** -0.5 * LOG2_E

    past_len = page_table.shape[1] * page
    page_rows = page * group
    cache_k_rows = cache_k.reshape(n_slots * n_phys, page_rows, dh)
    cache_v_rows = _values_in_stored_order(
        cache_v.reshape(n_slots * n_phys, page, n_heads_c * 2 * dh), n_heads_c)
    cos_p, sin_p = _rope_tables(0, tp)
    cos_s, sin_s = _rope_tables(past_len, ts)

    xp = x_prompt.reshape(bp * tp, d)
    xs = x_sample.reshape(bs * ts, d)
    tm_p, tm_s = PROMPT_ROW_TILE, bs * ts
    hg_p, rt_p, k_p, v_p, hg_s, rt_s, k_s, v_s = ([] for _ in range(8))
    for l in range(0, depth, 2):
        e = a = l // 2
        lam_init = 0.8 - 0.6 * math.exp(-0.3 * (l + 1))

        (proj,) = _inproj(xs, mod_s[l], norm_w[l, 0], w_in_ab[e], [(n_in, F32, 1.0)],
                          tm=tm_s, shift_col=0, scale_col=1)
        o, s_a, s_b = _mixer_sample(proj.reshape(bs, ts, n_in), cos_s, sin_s, hgrn_lb_logits,
                                    hgrn_norm_w[e], state_hgrn[e], state_ret[e], layer_slot=e)
        hg_s.append(s_a)
        rt_s.append(s_b)
        xs = _mlp(o.reshape(bs * ts, -1), xs, mod_s[l], norm_w[l, 1], w_out_ab[e], w_mlp_up[l],
                  w_mlp_down[l], final_norm_w, tm=tm_s, final_norm=False, n_parts=1)
        q, _, _, k_rows, v_rows = _qkv(xs, mod_s[l + 1], norm_w[l + 1, 0], w_in_c[a], tm=tm_s,
                                       q_scale=q_scale, n_heads=n_heads_c)
        k_s.append(k_rows)
        v_s.append(v_rows)
        paged = dict(q_rows=_sample_query_rows(q.reshape(bs, ts, n_c), n_heads_c),
                     k_new=k_rows.reshape(bs, ts * group, dh), v_new=v_rows.reshape(bs, ts * group, dh),
                     cache_k=cache_k_rows, cache_v=cache_v_rows, page_table=page_table,
                     lam_p=diff_lambda[a], subln_w=diff_subln_w[a], page_base=a * n_phys, tl=ts,
                     n_heads=n_heads_c, lam_init=lam_init)
        n_lo = bs // 2

        (proj,) = _inproj(xp, mod_p[l], norm_w[l, 0], w_in_ab[e], [(n_in, F32, 1.0)],
                          tm=tm_p, shift_col=0, scale_col=1)
        o, s_a, s_b = _mixer_prompt(proj.reshape(bp, tp, n_in), cos_p, sin_p, hgrn_lb_logits,
                                    hgrn_norm_w[e], layer_slot=e, chunk=128, rows_per_step=min(4, bp))
        hg_p.append(s_a)
        rt_p.append(s_b)
        xp, attn_lo = _mlp(o.reshape(bp * tp, -1), xp, mod_p[l], norm_w[l, 1], w_out_ab[e], w_mlp_up[l],
                           w_mlp_down[l], final_norm_w, tm=tm_p, final_norm=False,
                           n_parts=PROMPT_MLP_PARTS, paged=dict(paged, first_seq=0, n_seq=n_lo))

        q, k, v, k_rows, v_rows = _qkv(xp, mod_p[l + 1], norm_w[l + 1, 0], w_in_c[a], tm=tm_p,
                                       q_scale=q_scale, n_heads=n_heads_c)
        k_p.append(k_rows)
        v_p.append(v_rows)
        q, k, v = (t.reshape(bp, tp, n_c) for t in (q, k, v))
        o = _attn_prompt(q, k, v, diff_lambda[a], diff_subln_w[a], lam_init=lam_init, tq=min(512, tp))
        xp, attn_hi = _mlp(o.reshape(bp * tp, -1), xp, mod_p[l + 1], norm_w[l + 1, 1], w_out_c[a],
                           w_mlp_up[l + 1], w_mlp_down[l + 1], final_norm_w, tm=tm_p,
                           final_norm=(l + 2 == depth), n_parts=PROMPT_MLP_PARTS,
                           paged=dict(paged, first_seq=n_lo, n_seq=bs - n_lo))

        o = jnp.concatenate([attn_lo, attn_hi], axis=0)
        o = jnp.swapaxes(o.reshape(bs, ts, 2, n_heads_c, dh)[:, :, :, ::-1, :], 2, 3)
        xs = _mlp(o.reshape(bs * ts, -1), xs, mod_s[l + 1], norm_w[l + 1, 1], w_out_c[a],
                  w_mlp_up[l + 1], w_mlp_down[l + 1], final_norm_w, tm=tm_s,
                  final_norm=(l + 2 == depth), n_parts=1)

    y_p = xp.reshape(bp, tp, d)
    y_s = xs.reshape(bs, ts, d)
    hg_p, rt_p, k_p, v_p, hg_s, rt_s, k_s, v_s = (
        jnp.stack(t) for t in (hg_p, rt_p, k_p, v_p, hg_s, rt_s, k_s, v_s))

    n_odd = k_p.shape[0]

    def values_from_stored_order(v, lead):
        v = jnp.swapaxes(v.reshape(n_odd, *lead, 2, n_heads_c, dh), -2, -3)
        return v.reshape(n_odd, *lead, n_heads_c, 2 * dh)

    k_p = k_p.reshape(n_odd, bp, tp // page, page, n_heads_c, 2, dh)
    v_p = values_from_stored_order(v_p, (bp, tp // page, page))
    k_s = k_s.reshape(n_odd, bs, ts, n_heads_c, 2, dh)
    v_s = values_from_stored_order(v_s, (bs, ts))
    return (y_p, y_s, hg_p, rt_p, k_p, v_p, hg_s, rt_s, k_s, v_s)
```

```python
import functools
import math

import jax
import jax.numpy as jnp
import numpy as np
from jax import lax
from jax.experimental import pallas as pl
from jax.experimental.pallas import tpu as pltpu

F32 = jnp.float32
BF16 = jnp.bfloat16

HEAD_DIM = 128
SUBLANES = 8
NORM_EPS = 1e-6
ROPE_BASE = 10000.0
V7X_VMEM_LIMIT = 52 * 1024 * 1024

_NT = (((1,), (1,)), ((), ()))
_TN = (((0,), (0,)), ((), ()))


def _params(*semantics):
    return pltpu.CompilerParams(dimension_semantics=semantics, vmem_limit_bytes=V7X_VMEM_LIMIT)


def _resident(shape):
    zeros = (0,) * len(shape)
    return pl.BlockSpec(shape, lambda *_: zeros, pipeline_mode=pl.Buffered(1))


def _resident_layer(stacked, layer):
    return pl.BlockSpec((None,) + stacked.shape[1:], lambda *_: (layer, 0, 0),
                        pipeline_mode=pl.Buffered(1))


def _sigmoid(x):
    return 0.5 + 0.5 * jnp.tanh(0.5 * x)


def _silu(x):
    half = 0.5 * x
    return half + half * jnp.tanh(half)


def _rms(x):
    return x * lax.rsqrt(jnp.mean(x * x, axis=-1, keepdims=True) + NORM_EPS)


def _mm(a, b):
    return jnp.dot(a, b, preferred_element_type=F32)


def _ada_kernel(cp_ref, cs_ref, w_ref, b_ref, op_ref, os_ref):
    w = w_ref[0].astype(BF16)
    op_ref[0] = _mm(_silu(cp_ref[...]).astype(BF16), w) + b_ref[0]
    os_ref[0] = _mm(_silu(cs_ref[...]).astype(BF16), w) + b_ref[0]


def _ada(c_prompt, c_sample_rows, w_ada, b_ada):
    depth, d, n = w_ada.shape
    bp, rs = c_prompt.shape[0], c_sample_rows.shape[0]
    tn = n // 4
    return pl.pallas_call(
        _ada_kernel,
        grid=(depth, n // tn),
        in_specs=[pl.BlockSpec((bp, d), lambda l, j: (0, 0)),
                  pl.BlockSpec((rs, d), lambda l, j: (0, 0)),
                  pl.BlockSpec((1, d, tn), lambda l, j: (l, 0, j)),
                  pl.BlockSpec((1, 1, tn), lambda l, j: (l, 0, j))],
        out_specs=[pl.BlockSpec((1, bp, tn), lambda l, j: (l, 0, j)),
                   pl.BlockSpec((1, rs, tn), lambda l, j: (l, 0, j))],
        out_shape=[jax.ShapeDtypeStruct((depth, bp, n), F32),
                   jax.ShapeDtypeStruct((depth, rs, n), F32)],
        compiler_params=_params("arbitrary", "arbitrary"),
        name="ada_mod",
    )(c_prompt, c_sample_rows, w_ada, b_ada.reshape(depth, 1, n))


def _mod_spec(mod, n_rows, tm):
    arr, layer = mod
    tiles_per_block = (n_rows // arr.shape[1]) // tm
    return pl.BlockSpec((None, 1, arr.shape[2], arr.shape[3]),
                        lambda i, *_: (layer, i // tiles_per_block, 0, 0))


def _modulated_norm(x, nw_ref, mod_ref, shift_col, scale_col):
    d = x.shape[-1]
    shift = mod_ref[0, :, shift_col * d:(shift_col + 1) * d]
    scale = mod_ref[0, :, scale_col * d:(scale_col + 1) * d]
    return _rms(x) * nw_ref[...] * (1.0 + scale) + shift


def _inproj_kernel(x_ref, mod_ref, nw_ref, w_ref, *out_refs, shift_col, scale_col, out_scales, tc):
    h = _modulated_norm(x_ref[...], nw_ref, mod_ref, shift_col, scale_col).astype(BF16)
    off = 0
    for o_ref, out_scale in zip(out_refs, out_scales):
        width = o_ref.shape[-1]
        for c in range(0, width, tc):
            y = _mm(h, w_ref[:, off + c:off + c + tc])
            if out_scale != 1.0:
                y = y * out_scale
            o_ref[:, c:c + tc] = y.astype(o_ref.dtype)
        off += width


def _inproj(x, mod, nw, w, outs, *, tm, shift_col, scale_col):
    m, d = x.shape
    n = w.shape[1]
    kern = functools.partial(_inproj_kernel, shift_col=shift_col, scale_col=scale_col,
                             out_scales=tuple(o[2] for o in outs), tc=512)
    return pl.pallas_call(
        kern,
        grid=(m // tm,),
        in_specs=[pl.BlockSpec((tm, d), lambda i: (i, 0)),
                  _mod_spec(mod, m, tm),
                  _resident((1, d)),
                  _resident((d, n))],
        out_specs=[pl.BlockSpec((tm, o[0]), lambda i: (i, 0)) for o in outs],
        out_shape=[jax.ShapeDtypeStruct((m, o[0]), o[1]) for o in outs],
        compiler_params=_params("parallel"),
        name="norm_mod_inproj",
    )(x, mod[0], nw.reshape(1, d), w)


def _qkv_kernel(x_ref, mod_ref, nw_ref, w_ref, q_ref, kb_ref, vb_ref, kr_ref, vr_ref, *,
                q_scale, n_heads, tc):
    h = _modulated_norm(x_ref[...], nw_ref, mod_ref, 0, 1).astype(BF16)
    tm, n_c = q_ref.shape
    lanes = HEAD_DIM
    group = 2 * n_heads
    for c in range(0, n_c, tc):
        q_ref[:, c:c + tc] = (_mm(h, w_ref[:, c:c + tc]) * q_scale).astype(q_ref.dtype)
    for c in range(0, n_c, tc):
        y = _mm(h, w_ref[:, n_c + c:n_c + c + tc])
        kb_ref[:, c:c + tc] = y.astype(kb_ref.dtype)
        for i in range(tc // lanes):
            kr_ref[pl.ds(c // lanes + i, tm, stride=group), :] = y[:, i * lanes:(i + 1) * lanes]
    for c in range(0, n_c, tc):
        y = _mm(h, w_ref[:, 2 * n_c + c:2 * n_c + c + tc])
        vb_ref[:, c:c + tc] = y.astype(vb_ref.dtype)
        for i in range(tc // lanes):
            head, half = divmod(c // lanes + i, 2)
            vr_ref[pl.ds(half * n_heads + head, tm, stride=group), :] = y[:, i * lanes:(i + 1) * lanes]


def _qkv(x, mod, nw, w, *, tm, q_scale, n_heads):
    m, d = x.shape
    n_c = w.shape[1] // 3
    group = 2 * n_heads
    kern = functools.partial(_qkv_kernel, q_scale=q_scale, n_heads=n_heads, tc=512)
    cols = pl.BlockSpec((tm, n_c), lambda i: (i, 0))
    rows = pl.BlockSpec((tm * group, HEAD_DIM), lambda i: (i, 0))
    return pl.pallas_call(
        kern,
        grid=(m // tm,),
        in_specs=[pl.BlockSpec((tm, d), lambda i: (i, 0)),
                  _mod_spec(mod, m, tm),
                  _resident((1, d)),
                  _resident((d, 3 * n_c))],
        out_specs=[cols, cols, cols, rows, rows],
        out_shape=[jax.ShapeDtypeStruct((m, n_c), BF16)] * 3
                  + [jax.ShapeDtypeStruct((m * group, HEAD_DIM), F32)] * 2,
        compiler_params=_params("parallel"),
        name="norm_mod_qkv",
    )(x, mod[0], nw.reshape(1, d), w)


N_MLP_INPUTS = 8
N_ATTENTION_INPUTS = 7


def _mlp_kernel(*refs, n_parts, final_norm, attention):
    if attention is None:
        o_ref, x_ref, mod_ref, nw_ref, wo_ref, wu_ref, wd_ref, fw_ref, out_ref, h_ref, acc_ref = refs
    else:
        pt_ref, refs = refs[0], refs[1:]
        o_ref, x_ref, mod_ref, nw_ref, wo_ref, wu_ref, wd_ref, fw_ref = refs[:N_MLP_INPUTS]
        attention_in = refs[N_MLP_INPUTS:N_MLP_INPUTS + N_ATTENTION_INPUTS]
        out_ref, attention_out, h_ref, acc_ref = refs[N_MLP_INPUTS + N_ATTENTION_INPUTS:][:4]
        attention_refs = (pt_ref, *attention_in, attention_out,
                          *refs[N_MLP_INPUTS + N_ATTENTION_INPUTS + 4:])
    d = x_ref.shape[-1]
    cw = wu_ref.shape[1] // n_parts

    def mod_col(col):
        return mod_ref[0, :, col * d:(col + 1) * d]

    hidden = [None] * n_parts

    def mlp_up(r):
        if r == 0:
            x1 = x_ref[...] + mod_col(2) * _mm(o_ref[...].astype(BF16), wo_ref[...])
            out_ref[...] = x1
            h_ref[...] = _modulated_norm(x1, nw_ref, mod_ref, 3, 4).astype(BF16)
        u = jnp.maximum(_mm(h_ref[...], wu_ref[:, r * cw:(r + 1) * cw]), 0.0)
        hidden[r] = (u * u).astype(BF16)

    def mlp_down(r):
        part = _mm(hidden[r], wd_ref[r * cw:(r + 1) * cw, :])
        acc_ref[...] = part if r == 0 else acc_ref[...] + part

    def finish():
        y = out_ref[...] + mod_col(5) * acc_ref[...]
        if final_norm:
            y = _rms(y) * fw_ref[...]
        out_ref[...] = y

    if attention is None:
        for r in range(n_parts):
            mlp_up(r)
            mlp_down(r)
        finish()
    else:
        assert n_parts % PAGE_FETCH_STEPS == 0
        aligned = attention["steps_per_seq"] % n_parts == 0
        mlp_up(0)
        for r in range(n_parts):
            last = r == n_parts - 1
            _paged_attention_step(pl.program_id(0) * n_parts + r, pl.num_programs(0) * n_parts,
                                  attention_refs,
                                  (functools.partial(mlp_down, r),
                                   finish if last else functools.partial(mlp_up, r + 1),
                                   lambda: None),
                                  sub_step=r % PAGE_FETCH_STEPS,
                                  may_open=(r == 0 or not aligned),
                                  may_close=(last or not aligned), **attention)


def _mlp(o, x, mod, nw, w_out, w_up, w_down, final_w, *, tm, final_norm, n_parts, paged=None):
    m, d = x.shape
    k = o.shape[1]
    (w_up, up_layer), (w_down, down_layer) = w_up, w_down
    dff = w_up.shape[2]
    n_tiles = m // tm
    in_specs = [pl.BlockSpec((tm, k), lambda i, *_: (i, 0)),
                pl.BlockSpec((tm, d), lambda i, *_: (i, 0)),
                _mod_spec(mod, m, tm),
                _resident((1, d)),
                _resident((k, d)),
                _resident_layer(w_up, up_layer),
                _resident_layer(w_down, down_layer),
                _resident((1, d))]
    operands = [o, x, mod[0], nw.reshape(1, d), w_out, w_up, w_down, final_w.reshape(1, d)]
    out_specs = pl.BlockSpec((tm, d), lambda i, *_: (i, 0))
    out_shape = jax.ShapeDtypeStruct((m, d), F32)
    scratch = [pltpu.VMEM((tm, d), BF16), pltpu.VMEM((tm, d), F32)]
    if paged is None:
        kern = functools.partial(_mlp_kernel, n_parts=n_parts, final_norm=final_norm, attention=None)
        return pl.pallas_call(
            kern, grid=(n_tiles,), in_specs=in_specs, out_specs=out_specs, out_shape=out_shape,
            scratch_shapes=scratch, compiler_params=_params("arbitrary"), name="outproj_mlp_residual",
        )(*operands)

    q_rows, k_new, v_new = paged["q_rows"], paged["k_new"], paged["v_new"]
    cache_k, cache_v, page_table = paged["cache_k"], paged["cache_v"], paged["page_table"]
    n_seq = paged["n_seq"]
    n_rows, dh = q_rows.shape[1:]
    n_pages = page_table.shape[1]
    page_rows = cache_k.shape[1]
    assert 2 * paged["n_heads"] == SUBLANES, "one (map, token) block of query rows must fill one vreg"
    pages_per_step = (n_seq * n_pages) // (n_tiles * n_parts)
    assert pages_per_step * n_tiles * n_parts == n_seq * n_pages and n_pages % pages_per_step == 0
    attention = dict(pages_per_step=pages_per_step, page_base=paged["page_base"],
                     steps_per_seq=n_pages // pages_per_step, first_seq=paged["first_seq"],
                     tl=paged["tl"], n_heads=paged["n_heads"], lam_init=paged["lam_init"])
    kern = functools.partial(_mlp_kernel, n_parts=n_parts, final_norm=final_norm, attention=attention)
    ring = (PAGE_RING_DEPTH, PAGE_FETCH_STEPS * pages_per_step, page_rows, dh)
    grid_spec = pltpu.PrefetchScalarGridSpec(
        num_scalar_prefetch=1,
        grid=(n_tiles,),
        in_specs=in_specs + [_resident(paged["lam_p"].shape), _resident((1, 2 * dh)),
                             _resident(q_rows.shape), _resident(k_new.shape), _resident(v_new.shape),
                             pl.BlockSpec(memory_space=pl.ANY), pl.BlockSpec(memory_space=pl.ANY)],
        out_specs=[out_specs, pl.BlockSpec((n_seq, n_rows // 2, dh), lambda i, *_: (0, 0, 0))],
        scratch_shapes=scratch + [pltpu.VMEM((n_rows, dh), F32), pltpu.VMEM((n_rows, dh), F32),
                                  pltpu.VMEM((n_rows, dh), F32),
                                  pltpu.VMEM((dh, dh), F32), pltpu.VMEM((dh, dh), F32),
                                  pltpu.VMEM(ring, F32), pltpu.VMEM(ring, F32),
                                  pltpu.SemaphoreType.DMA((2, PAGE_RING_DEPTH))],
    )
    return pl.pallas_call(
        kern,
        grid_spec=grid_spec,
        out_shape=[out_shape, jax.ShapeDtypeStruct((n_seq, n_rows // 2, dh), F32)],
        compiler_params=_params("arbitrary"),
        name="outproj_mlp_paged_attention",
    )(page_table.reshape(-1), *operands, paged["lam_p"], paged["subln_w"].reshape(1, 2 * dh),
      q_rows, k_new, v_new, cache_k, cache_v)


def _hgrn_lower_bound(logits_ref, layer_slot):
    lg = logits_ref[...]
    ex = jnp.exp(lg - jnp.max(lg, axis=0, keepdims=True))
    pr = ex / jnp.sum(ex, axis=0, keepdims=True)
    return jnp.sum(pr[:layer_slot + 1], axis=0, keepdims=True)


def _rotate_pairs(x, cos, sin_signed):
    n = x.shape[-1]
    lane = lax.broadcasted_iota(jnp.int32, x.shape, 1)
    swapped = jnp.where((lane & 1) == 0, pltpu.roll(x, n - 1, 1), pltpu.roll(x, 1, 1))
    return x * cos + swapped * sin_signed


def _cumsum_rows(x):
    n = x.shape[0]
    row = lax.broadcasted_iota(jnp.int32, x.shape, 0)
    d = 1
    while d < n:
        x = x + jnp.where(row >= d, pltpu.roll(x, d, 0), 0.0)
        d *= 2
    return x


def _split_row(g, b):
    n, lanes = g.shape
    if 2 * b >= SUBLANES:
        g3 = g.reshape(n // (2 * b), 2 * b, lanes)
        return jnp.broadcast_to(g3[:, b - 1:b, :], g3.shape).reshape(n, lanes)
    rmod = lax.broadcasted_iota(jnp.int32, g.shape, 0) & (2 * b - 1)
    out = g
    for r in range(2 * b):
        if r != b - 1:
            out = jnp.where(rmod == r, pltpu.roll(g, (r - (b - 1)) % n, 0), out)
    return out


def _level_matrix(n):
    t = np.arange(n)[:, None]
    s = np.arange(n)[None, :]
    hi = np.floor(np.log2(np.maximum(t ^ s, 1))).astype(np.int32)
    return jnp.asarray(np.where(s < t, hi, -1).astype(np.int32))


def _hgrn_chunk(q, k, v, g2, lv, st):
    n = q.shape[0]
    row = lax.broadcasted_iota(jnp.int32, q.shape, 0)
    ri = lax.broadcasted_iota(jnp.int32, (n, n), 0)
    ci = lax.broadcasted_iota(jnp.int32, (n, n), 1)
    a = jnp.where(ri == ci, jnp.sum(q * k, axis=-1, keepdims=True), 0.0)
    b, level = 1, 0
    while b < n:
        if b >= SUBLANES:
            parts = []
            for lo in range(0, n, 2 * b):
                edge = g2[lo + b - 1:lo + b, :]
                parts.append(k[lo:lo + b] * jnp.exp2(edge - g2[lo:lo + b]))
                parts.append(q[lo + b:lo + 2 * b] * jnp.exp2(g2[lo + b:lo + 2 * b] - edge))
            z = jnp.concatenate(parts, axis=0).astype(BF16)
        else:
            is_query = (row & b) != 0
            sign = jnp.where(is_query, 1.0, -1.0)
            z = (jnp.where(is_query, q, k) * jnp.exp2((g2 - _split_row(g2, b)) * sign)).astype(BF16)
        a = jnp.where(lv == level, lax.dot_general(z, z, _NT, preferred_element_type=F32), a)
        b, level = 2 * b, level + 1
    carried = lax.dot_general((q * jnp.exp2(g2)).astype(BF16), st.astype(BF16), _NT,
                              preferred_element_type=F32)
    o = _mm(a.astype(BF16), v.astype(BF16)) + carried
    g_end = g2[n - 1:n, :]
    k_dec = (k * jnp.exp2(g_end - g2)).astype(BF16)
    st_new = st * jnp.exp2(g_end) + lax.dot_general(v.astype(BF16), k_dec, _TN,
                                                    preferred_element_type=F32)
    return o, st_new


def _retention_chunk(q, k, v, log_gamma, st):
    n = q.shape[0]
    rowf = lax.broadcasted_iota(jnp.int32, q.shape, 0).astype(F32)
    ri = lax.broadcasted_iota(jnp.int32, (n, n), 0)
    ci = lax.broadcasted_iota(jnp.int32, (n, n), 1)
    decay = jnp.where(ci <= ri, jnp.exp((ri - ci).astype(F32) * log_gamma), 0.0)
    a = lax.dot_general(q.astype(BF16), k.astype(BF16), _NT, preferred_element_type=F32) * decay
    q_dec = (q * jnp.exp((rowf + 1.0) * log_gamma)).astype(BF16)
    o = _mm(a.astype(BF16), v.astype(BF16)) + _mm(q_dec, st.astype(BF16))
    k_dec = (k * jnp.exp((n - 1.0 - rowf) * log_gamma)).astype(BF16)
    st_new = st * math.exp(n * log_gamma) + lax.dot_general(k_dec, v.astype(BF16), _TN,
                                                            preferred_element_type=F32)
    return o, st_new


def _retention_log_gamma(h):
    return math.log(1.0 - 2.0 ** (-5.0 - h))


def _mixer_prompt_kernel(proj_ref, cos_ref, sin_ref, lv_ref, lbl_ref, na_ref,
                         o_ref, sa_out, sb_out, sat_ref, sb_ref, *, layer_slot, n_heads):
    t = pl.program_id(1)
    hd = HEAD_DIM
    w = n_heads * hd

    @pl.when(t == 0)
    def _():
        sat_ref[...] = jnp.zeros_like(sat_ref)
        sb_ref[...] = jnp.zeros_like(sb_ref)

    lb = _hgrn_lower_bound(lbl_ref, layer_slot)
    lv = lv_ref[...]
    cos = cos_ref[...]
    sin = sin_ref[...]
    for r in range(proj_ref.shape[0]):
        def cols(group, h):
            return proj_ref[r, :, group * w + h * hd:group * w + (h + 1) * hd]

        fz = lb + (1.0 - lb) * _sigmoid(proj_ref[r, :, w:2 * w])
        g_all = _cumsum_rows(jnp.log2(fz))
        k_all = 1.0 - fz
        for h in range(n_heads):
            sl = slice(h * hd, (h + 1) * hd)
            q = _silu(cols(0, h)) * hd ** -0.5
            o, st = _hgrn_chunk(q, k_all[:, sl], cols(2, h), g_all[:, sl], lv, sat_ref[r, h])
            sat_ref[r, h] = st
            o_ref[r, :, sl] = (_rms(o) * na_ref[...] * _silu(cols(3, h))).astype(o_ref.dtype)
        for h in range(n_heads):
            q = _rotate_pairs(cols(4, h), cos, sin)
            k = _rotate_pairs(cols(5, h), cos, sin) * hd ** -0.5
            o, st = _retention_chunk(q, k, cols(6, h), _retention_log_gamma(h), sb_ref[r, h])
            sb_ref[r, h] = st
            o_ref[r, :, w + h * hd:w + (h + 1) * hd] = (_rms(o) * _silu(cols(7, h))).astype(o_ref.dtype)

    @pl.when(t == pl.num_programs(1) - 1)
    def _():
        for r in range(proj_ref.shape[0]):
            for h in range(n_heads):
                sa_out[r, h] = sat_ref[r, h].T
                sb_out[r, h] = sb_ref[r, h]


def _mixer_prompt(proj, cos, sin, lb_logits, norm_a, *, layer_slot, chunk, rows_per_step):
    bsz, tlen, n = proj.shape
    hd = HEAD_DIM
    n_heads = n // (8 * hd)
    rb = rows_per_step
    kern = functools.partial(_mixer_prompt_kernel, layer_slot=layer_slot, n_heads=n_heads)
    state = jax.ShapeDtypeStruct((bsz, n_heads, hd, hd), F32)
    state_spec = pl.BlockSpec((rb, n_heads, hd, hd), lambda b, t: (b, 0, 0, 0))
    return pl.pallas_call(
        kern,
        grid=(bsz // rb, tlen // chunk),
        in_specs=[pl.BlockSpec((rb, chunk, n), lambda b, t: (b, t, 0)),
                  pl.BlockSpec((chunk, hd), lambda b, t: (t, 0)),
                  pl.BlockSpec((chunk, hd), lambda b, t: (t, 0)),
                  pl.BlockSpec((chunk, chunk), lambda b, t: (0, 0)),
                  pl.BlockSpec(lb_logits.shape, lambda b, t: (0, 0)),
                  pl.BlockSpec((1, hd), lambda b, t: (0, 0))],
        out_specs=[pl.BlockSpec((rb, chunk, 2 * n_heads * hd), lambda b, t: (b, t, 0)),
                   state_spec, state_spec],
        out_shape=[jax.ShapeDtypeStruct((bsz, tlen, 2 * n_heads * hd), BF16), state, state],
        scratch_shapes=[pltpu.VMEM((rb, n_heads, hd, hd), F32), pltpu.VMEM((rb, n_heads, hd, hd), F32)],
        compiler_params=_params("arbitrary", "arbitrary"),
        name="mixer_prompt",
    )(proj, cos, sin, _level_matrix(chunk), lb_logits, norm_a.reshape(1, hd))


def _pair_chunk(q, k, v, g, s0, dec, kpad_ref, vpad_ref):
    tl = q.shape[0]
    row = lax.broadcasted_iota(jnp.int32, q.shape, 0)
    o = _mm((q * jnp.exp(g)).astype(BF16), s0.astype(BF16))
    for s in range(tl):
        live = row >= s
        w = jnp.exp(jnp.where(live, g - g[s:s + 1, :], 0.0))
        a = jnp.sum(jnp.where(live, q * w * k[s:s + 1, :], 0.0), axis=-1, keepdims=True)
        o = o + a * v[s:s + 1, :]
    kpad_ref[0:tl, :] = k * jnp.exp(g[tl - 1:tl, :] - g)
    vpad_ref[0:tl, :] = v
    upd = lax.dot_general(kpad_ref[...].astype(BF16), vpad_ref[...].astype(BF16), _TN,
                          preferred_element_type=F32)
    return o, s0 * dec + upd


def _mixer_sample_kernel(proj_ref, cos_ref, sin_ref, lbl_ref, na_ref, sa_ref, sb_ref,
                         o_ref, sa_out, sb_out, kpad_ref, vpad_ref, *, layer_slot, n_heads):
    hd = HEAD_DIM
    w = n_heads * hd
    tl = proj_ref.shape[1]
    kpad_ref[...] = jnp.zeros_like(kpad_ref)
    vpad_ref[...] = jnp.zeros_like(vpad_ref)

    def cols(group, h):
        return proj_ref[0, :, group * w + h * hd:group * w + (h + 1) * hd]

    lb = _hgrn_lower_bound(lbl_ref, layer_slot)
    fz = lb + (1.0 - lb) * _sigmoid(proj_ref[0, :, w:2 * w])
    lf = jnp.log(fz)
    row = lax.broadcasted_iota(jnp.int32, lf.shape, 0)
    g_all = jnp.zeros_like(lf)
    for j in range(tl):
        g_all = g_all + jnp.where(row >= j, lf[j:j + 1, :], 0.0)
    k_all = 1.0 - fz
    for h in range(n_heads):
        sl = slice(h * hd, (h + 1) * hd)
        g = g_all[:, sl]
        q = _silu(cols(0, h)) * hd ** -0.5
        dec = jnp.broadcast_to(jnp.exp(g[tl - 1:tl, :]), (hd, hd)).T
        o, st = _pair_chunk(q, k_all[:, sl], cols(2, h), g, sa_ref[0, h], dec, kpad_ref, vpad_ref)
        sa_out[0, h] = st
        o_ref[0, :, sl] = _rms(o) * na_ref[...] * _silu(cols(3, h))

    cos = cos_ref[...]
    sin = sin_ref[...]
    rowf = lax.broadcasted_iota(jnp.int32, (tl, hd), 0).astype(F32)
    for h in range(n_heads):
        lg = _retention_log_gamma(h)
        q = _rotate_pairs(cols(4, h), cos, sin)
        k = _rotate_pairs(cols(5, h), cos, sin) * hd ** -0.5
        o, st = _pair_chunk(q, k, cols(6, h), (rowf + 1.0) * lg, sb_ref[0, h], math.exp(tl * lg),
                            kpad_ref, vpad_ref)
        sb_out[0, h] = st
        o_ref[0, :, w + h * hd:w + (h + 1) * hd] = _rms(o) * _silu(cols(7, h))


def _mixer_sample(proj, cos, sin, lb_logits, norm_a, state_a, state_b, *, layer_slot):
    bsz, tl, n = proj.shape
    hd = HEAD_DIM
    n_heads = n // (8 * hd)
    kern = functools.partial(_mixer_sample_kernel, layer_slot=layer_slot, n_heads=n_heads)
    state = jax.ShapeDtypeStruct((bsz, n_heads, hd, hd), F32)
    state_spec = pl.BlockSpec((1, n_heads, hd, hd), lambda b: (b, 0, 0, 0))
    return pl.pallas_call(
        kern,
        grid=(bsz,),
        in_specs=[pl.BlockSpec((1, tl, n), lambda b: (b, 0, 0)),
                  pl.BlockSpec((tl, hd), lambda b: (0, 0)),
                  pl.BlockSpec((tl, hd), lambda b: (0, 0)),
                  pl.BlockSpec(lb_logits.shape, lambda b: (0, 0)),
                  pl.BlockSpec((1, hd), lambda b: (0, 0)),
                  state_spec, state_spec],
        out_specs=[pl.BlockSpec((1, tl, 2 * n_heads * hd), lambda b: (b, 0, 0)), state_spec, state_spec],
        out_shape=[jax.ShapeDtypeStruct((bsz, tl, 2 * n_heads * hd), F32), state, state],
        scratch_shapes=[pltpu.VMEM((hd, hd), F32), pltpu.VMEM((hd, hd), F32)],
        compiler_params=_params("arbitrary"),
        name="mixer_sample",
    )(proj, cos, sin, lb_logits, norm_a.reshape(1, hd), state_a, state_b)


def _diff_lambda(lam_ref, lam_init):
    lp = lam_ref[...]
    return (jnp.exp(jnp.sum(lp[0:1] * lp[1:2], axis=-1, keepdims=True))
            - jnp.exp(jnp.sum(lp[2:3] * lp[3:4], axis=-1, keepdims=True)) + lam_init)


LOG2_E = math.log2(math.e)


def _attn_prompt_kernel(lam_ref, sw_ref, q_ref, k_ref, v_ref, o_ref, m_ref, l_ref, acc_ref, *,
                        tq, lam_init):
    dh = HEAD_DIM
    lanes = m_ref.shape[-1]
    n_tiles = q_ref.shape[1] // tq
    m_ref[...] = jnp.full_like(m_ref, -jnp.inf)
    l_ref[...] = jnp.zeros_like(l_ref)
    acc_ref[...] = jnp.zeros_like(acc_ref)

    def tile_pair(q0, nq, k0, nk):
        rq = slice(q0, q0 + nq)
        rk = slice(k0, k0 + nk)
        v = v_ref[0, rk, :]
        scores = []
        for j in range(2):
            s = lax.dot_general(q_ref[0, rq, j * dh:(j + 1) * dh], k_ref[0, rk, j * dh:(j + 1) * dh],
                                _NT, preferred_element_type=F32)
            if k0 + nk - 1 > q0:
                ri = lax.broadcasted_iota(jnp.int32, s.shape, 0)
                ci = lax.broadcasted_iota(jnp.int32, s.shape, 1)
                s = jnp.where(ci - ri <= q0 - k0, s, -jnp.inf)
            scores.append(s)
        probs, alphas = [], []
        for j in range(2):
            s = scores[j]
            m_old = m_ref[j, rq]
            m_new = jnp.maximum(m_old, jnp.max(s, axis=-1, keepdims=True))
            alpha = jnp.exp2(m_old - m_new)
            l_new = alpha * l_ref[j, rq]
            tiles = []
            for c in range(0, s.shape[1], lanes):
                p = jnp.exp2(s[:, c:c + lanes] - m_new)
                l_new = l_new + p
                tiles.append(p.astype(BF16))
            l_ref[j, rq] = l_new
            m_ref[j, rq] = m_new
            probs.append(jnp.concatenate(tiles, axis=1))
            alphas.append(alpha)
        for j in range(2):
            pv = _mm(probs[j], v)
            for c in range(0, pv.shape[1], lanes):
                acc_ref[j, rq, c:c + lanes] = (alphas[j] * acc_ref[j, rq, c:c + lanes]
                                               + pv[:, c:c + lanes])

    for kj in range(n_tiles):
        for qi in range(kj, n_tiles):
            tile_pair(qi * tq, tq, kj * tq, tq)

    lam = _diff_lambda(lam_ref, lam_init)
    for qi in range(n_tiles):
        rq = slice(qi * tq, (qi + 1) * tq)
        l0 = jnp.sum(l_ref[0, rq], axis=-1, keepdims=True)
        l1 = jnp.sum(l_ref[1, rq], axis=-1, keepdims=True)
        o = acc_ref[0, rq] / l0 - lam * (acc_ref[1, rq] / l1)
        o_ref[0, rq] = (_rms(o) * sw_ref[...] * (1.0 - lam_init)).astype(o_ref.dtype)


def _attn_prompt(q, k, v, lam_p, subln_w, *, lam_init, tq):
    bsz, tlen, n = q.shape
    e = 2 * HEAD_DIM
    n_heads = n // e
    kern = functools.partial(_attn_prompt_kernel, tq=tq, lam_init=lam_init)
    head_cols = pl.BlockSpec((1, tlen, e), lambda b, h: (b, 0, h))
    return pl.pallas_call(
        kern,
        grid=(bsz, n_heads),
        in_specs=[pl.BlockSpec(lam_p.shape, lambda b, h: (0, 0)),
                  pl.BlockSpec((1, e), lambda b, h: (0, 0)),
                  head_cols, head_cols, head_cols],
        out_specs=head_cols,
        out_shape=jax.ShapeDtypeStruct((bsz, tlen, n), BF16),
        scratch_shapes=[pltpu.VMEM((2, tlen, HEAD_DIM), F32), pltpu.VMEM((2, tlen, HEAD_DIM), F32),
                        pltpu.VMEM((2, tlen, e), F32)],
        compiler_params=_params("parallel", "parallel"),
        name="diff_attn_prompt",
    )(lam_p, subln_w.reshape(1, e), q, k, v)


def _sample_row_fields(shape, tl, n_heads):
    r = lax.broadcasted_iota(jnp.int32, shape, 0)
    head = n_heads - 1 - r % n_heads
    half = (r // n_heads) % 2
    token = (r // (2 * n_heads)) % tl
    qmap = r // (2 * n_heads * tl)
    return head, qmap, token, half


PAGE_RING_DEPTH = 2
PAGE_FETCH_STEPS = 2


def _paged_attention_step(step, n_steps, refs, overlapped_work, *, pages_per_step, page_base,
                          steps_per_seq, first_seq, tl, n_heads, lam_init, may_open, may_close,
                          sub_step):
    (pt_ref, lam_ref, sw_ref, q_ref, kn_ref, vn_ref, ck_hbm, cv_hbm, o_ref,
     m_ref, l_ref, acc_ref, kpad_ref, vpad_ref, kbuf, vbuf, sems) = refs
    seq = step // steps_per_seq
    p = step % steps_per_seq
    group = 2 * n_heads
    n_rows = q_ref.shape[1]
    rows_per_map = n_rows // 2

    fetch_pages = PAGE_FETCH_STEPS * pages_per_step
    fetch = step // PAGE_FETCH_STEPS
    n_fetches = n_steps // PAGE_FETCH_STEPS

    def page_copies(of_fetch, slot):
        out = []
        for i in range(fetch_pages):
            page = page_base + pt_ref[first_seq * steps_per_seq * pages_per_step
                                      + of_fetch * fetch_pages + i]
            out.append(pltpu.make_async_copy(ck_hbm.at[page], kbuf.at[slot, i], sems.at[0, slot]))
            out.append(pltpu.make_async_copy(cv_hbm.at[page], vbuf.at[slot, i], sems.at[1, slot]))
        return out

    def start_fetch(of_fetch):
        for cp in page_copies(of_fetch, of_fetch % PAGE_RING_DEPTH):
            cp.start()

    slot = fetch % PAGE_RING_DEPTH
    if sub_step == 0:
        ahead = PAGE_RING_DEPTH - 1

        @pl.when(fetch == 0)
        def _():
            for f in range(ahead):
                @pl.when(f < n_fetches)
                def _():
                    start_fetch(f)

        @pl.when(fetch + ahead < n_fetches)
        def _():
            start_fetch(fetch + ahead)

        for cp in page_copies(fetch, slot):
            cp.wait()

    if may_open:
        @pl.when(p == 0)
        def _():
            m_ref[...] = jnp.full_like(m_ref, -jnp.inf)
            l_ref[...] = jnp.zeros_like(l_ref)
            acc_ref[...] = jnp.zeros_like(acc_ref)

    q = q_ref[first_seq + seq]

    def own_lanes(n_keys, causal):
        head, qmap, token, _ = _sample_row_fields((n_rows, n_keys), tl, n_heads)
        key = lax.broadcasted_iota(jnp.int32, (n_rows, n_keys), 1)
        own = (key % group) == 2 * head + qmap
        return jnp.logical_and(own, key // group <= token) if causal else own

    def partial_softmax(pages, causal, hooks=(None, None)):
        before_scores, after_scores = hooks
        own = own_lanes(pages[0][0].shape[0], causal)
        keys = [k_rows.astype(BF16) for k_rows, _ in pages]
        if before_scores is not None:
            before_scores()
        scores = [lax.dot_general(q, k, _NT, preferred_element_type=F32) for k in keys]
        if after_scores is not None:
            after_scores()
        stats, probs = [], []
        for s in scores:
            s = jnp.where(own, s, -jnp.inf)
            m = jnp.max(s, axis=-1, keepdims=True)
            pexp = jnp.exp2(s - m)
            stats.append((m, jnp.sum(pexp, axis=-1, keepdims=True)))
            aligned = []
            for blk in range(n_rows // SUBLANES):
                shift = group - (n_heads - 1) - (blk * SUBLANES) // rows_per_map
                aligned.append(pltpu.roll(pexp[blk * SUBLANES:(blk + 1) * SUBLANES, :], shift, 1,
                                          stride=1, stride_axis=0))
            probs.append(jnp.concatenate(aligned, axis=0).astype(BF16))
        return [(m, l, _mm(p_rows, pltpu.roll(v_rows, group, 0).astype(BF16)))
                for (m, l), p_rows, (_, v_rows) in zip(stats, probs, pages)]

    def merge(parts):
        m_run = m_ref[...]
        m_new = m_run
        for m, _, _ in parts:
            m_new = jnp.maximum(m_new, m)
        alpha = jnp.exp2(m_run - m_new)
        l_new = alpha * l_ref[...]
        acc_new = alpha * acc_ref[...]
        for m, l, o in parts:
            wgt = jnp.exp2(m - m_new)
            l_new = l_new + wgt * l
            acc_new = acc_new + wgt * o
        m_ref[...] = m_new
        l_ref[...] = l_new
        acc_ref[...] = acc_new

    before_scores, after_scores, after_values = overlapped_work
    first_page = sub_step * pages_per_step
    parts = partial_softmax([(kbuf[slot, first_page + i], vbuf[slot, first_page + i])
                             for i in range(pages_per_step)], False,
                            (before_scores, after_scores))
    after_values()
    merge(parts)

    if may_close:
        @pl.when(p == steps_per_seq - 1)
        def _():
            n_new = kn_ref.shape[1]
            kpad_ref[...] = jnp.zeros_like(kpad_ref)
            vpad_ref[...] = jnp.zeros_like(vpad_ref)
            kpad_ref[0:n_new, :] = kn_ref[first_seq + seq]
            vpad_ref[0:n_new, :] = vn_ref[first_seq + seq]
            merge(partial_softmax([(kpad_ref[...], vpad_ref[...])], True))

            o = (acc_ref[0:rows_per_map, :] / l_ref[0:rows_per_map, :]
                 - _diff_lambda(lam_ref, lam_init) * (acc_ref[rows_per_map:, :] / l_ref[rows_per_map:, :]))
            _, _, _, half = _sample_row_fields(o.shape, tl, n_heads)
            ss = jnp.broadcast_to(jnp.sum(o * o, axis=-1, keepdims=True), o.shape)
            other = jnp.where(half == 0, pltpu.roll(ss, rows_per_map - n_heads, 0),
                              pltpu.roll(ss, n_heads, 0))
            inv = lax.rsqrt((ss + other) / (2 * HEAD_DIM) + NORM_EPS)
            sw = jnp.where(half == 0, sw_ref[:, 0:HEAD_DIM], sw_ref[:, HEAD_DIM:])
            o_ref[seq] = o * inv * sw * (1.0 - lam_init)


def _sample_query_rows(q, n_heads):
    bsz, tl, _ = q.shape
    r = np.arange(4 * n_heads * tl)
    head = n_heads - 1 - r % n_heads
    token = (r // (2 * n_heads)) % tl
    qmap = r // (2 * n_heads * tl)
    return q.reshape(bsz, tl, n_heads, 2, HEAD_DIM)[:, token, head, qmap, :]


def _values_in_stored_order(v, n_heads):
    lead = v.shape[:-2]
    tokens = v.shape[-2]
    v = v.reshape(*lead, tokens, n_heads, 2, HEAD_DIM)
    return jnp.swapaxes(v, -2, -3).reshape(*lead, tokens * 2 * n_heads, HEAD_DIM)


def _rope_tables(pos0, tlen):
    inv = 1.0 / (ROPE_BASE ** jnp.linspace(0.0, 1.0, HEAD_DIM // 2, dtype=F32))
    ang = (pos0 + jnp.arange(tlen)).astype(F32)[:, None] * inv[None, :]
    sign = jnp.tile(jnp.asarray([-1.0, 1.0], F32), HEAD_DIM // 2)
    return jnp.repeat(jnp.cos(ang), 2, axis=1), jnp.repeat(jnp.sin(ang), 2, axis=1) * sign


PROMPT_ROW_TILE = 512
PROMPT_MLP_PARTS = 8


def kernel(x_prompt, x_sample, state_hgrn, state_ret, cache_k, cache_v, page_table, c_prompt, c_sample,
           w_ada, b_ada, norm_w, w_in_ab, w_out_ab, hgrn_lb_logits, hgrn_norm_w, w_in_c, w_out_c,
           diff_lambda, diff_subln_w, w_mlp_up, w_mlp_down, final_norm_w):
    bp, tp, d = x_prompt.shape
    bs, ts, _ = x_sample.shape
    n_slots, n_phys, page, n_heads_c, _, dh = cache_k.shape

    mod_p, mod_s = _ada(c_prompt, jnp.repeat(c_sample, ts, axis=0), w_ada, b_ada)
    depth = mod_p.shape[0]
    assert depth % 2 == 0, "layers come in (recurrent mixer, attention) pairs"
    mod_p = mod_p.reshape(depth, bp, 1, 6 * d)
    mod_s = mod_s.reshape(depth, 1, bs * ts, 6 * d)

    def bf16(t):
        return t.astype(BF16)

    w_mlp_up_bf16, w_mlp_down_bf16 = bf16(w_mlp_up), bf16(w_mlp_down)

    n_in = w_in_ab.shape[2]
    n_c = w_in_c.shape[2] // 3
    group = 2 * n_heads_c
    q_scale = dh ** -0.5 * LOG2_E

    past_len = page_table.shape[1] * page
    page_rows = page * group
    cache_k_rows = cache_k.reshape(n_slots * n_phys, page_rows, dh)
    cache_v_rows = _values_in_stored_order(
        cache_v.reshape(n_slots * n_phys, page, n_heads_c * 2 * dh), n_heads_c)
    cos_p, sin_p = _rope_tables(0, tp)
    cos_s, sin_s = _rope_tables(past_len, ts)

    xp = x_prompt.reshape(bp * tp, d)
    xs = x_sample.reshape(bs * ts, d)
    tm_p, tm_s = PROMPT_ROW_TILE, bs * ts
    hg_p, rt_p, k_p, v_p, hg_s, rt_s, k_s, v_s = ([] for _ in range(8))
    for l in range(0, depth, 2):
        e = a = l // 2
        lam_init = 0.8 - 0.6 * math.exp(-0.3 * (l + 1))
        w_in_e, w_out_e, w_in_a, w_out_a = (bf16(t) for t in (w_in_ab[e], w_out_ab[e], w_in_c[a], w_out_c[a]))
        w_up = [(w_mlp_up_bf16, l), (w_mlp_up_bf16, l + 1)]
        w_down = [(w_mlp_down_bf16, l), (w_mlp_down_bf16, l + 1)]

        (proj,) = _inproj(xs, (mod_s, l), norm_w[l, 0], w_in_e, [(n_in, F32, 1.0)],
                          tm=tm_s, shift_col=0, scale_col=1)
        o, s_a, s_b = _mixer_sample(proj.reshape(bs, ts, n_in), cos_s, sin_s, hgrn_lb_logits,
                                    hgrn_norm_w[e], state_hgrn[e], state_ret[e], layer_slot=e)
        hg_s.append(s_a)
        rt_s.append(s_b)
        xs = _mlp(o.reshape(bs * ts, -1), xs, (mod_s, l), norm_w[l, 1], w_out_e, w_up[0],
                  w_down[0], final_norm_w, tm=tm_s, final_norm=False, n_parts=1)
        q, _, _, k_rows, v_rows = _qkv(xs, (mod_s, l + 1), norm_w[l + 1, 0], w_in_a, tm=tm_s,
                                       q_scale=q_scale, n_heads=n_heads_c)
        k_s.append(k_rows)
        v_s.append(v_rows)
        paged = dict(q_rows=_sample_query_rows(q.reshape(bs, ts, n_c), n_heads_c),
                     k_new=k_rows.reshape(bs, ts * group, dh), v_new=v_rows.reshape(bs, ts * group, dh),
                     cache_k=cache_k_rows, cache_v=cache_v_rows, page_table=page_table,
                     lam_p=diff_lambda[a], subln_w=diff_subln_w[a], page_base=a * n_phys, tl=ts,
                     n_heads=n_heads_c, lam_init=lam_init)
        n_lo = bs // 2

        (proj,) = _inproj(xp, (mod_p, l), norm_w[l, 0], w_in_e, [(n_in, F32, 1.0)],
                          tm=tm_p, shift_col=0, scale_col=1)
        o, s_a, s_b = _mixer_prompt(proj.reshape(bp, tp, n_in), cos_p, sin_p, hgrn_lb_logits,
                                    hgrn_norm_w[e], layer_slot=e, chunk=128, rows_per_step=min(4, bp))
        hg_p.append(s_a)
        rt_p.append(s_b)
        xp, attn_lo = _mlp(o.reshape(bp * tp, -1), xp, (mod_p, l), norm_w[l, 1], w_out_e, w_up[0],
                           w_down[0], final_norm_w, tm=tm_p, final_norm=False,
                           n_parts=PROMPT_MLP_PARTS, paged=dict(paged, first_seq=0, n_seq=n_lo))

        q, k, v, k_rows, v_rows = _qkv(xp, (mod_p, l + 1), norm_w[l + 1, 0], w_in_a, tm=tm_p,
                                       q_scale=q_scale, n_heads=n_heads_c)
        k_p.append(k_rows)
        v_p.append(v_rows)
        q, k, v = (t.reshape(bp, tp, n_c) for t in (q, k, v))
        o = _attn_prompt(q, k, v, diff_lambda[a], diff_subln_w[a], lam_init=lam_init, tq=min(512, tp))
        xp, attn_hi = _mlp(o.reshape(bp * tp, -1), xp, (mod_p, l + 1), norm_w[l + 1, 1], w_out_a,
                           w_up[1], w_down[1], final_norm_w, tm=tm_p,
                           final_norm=(l + 2 == depth), n_parts=PROMPT_MLP_PARTS,
                           paged=dict(paged, first_seq=n_lo, n_seq=bs - n_lo))

        o = jnp.concatenate([attn_lo, attn_hi], axis=0)
        o = jnp.swapaxes(o.reshape(bs, ts, 2, n_heads_c, dh)[:, :, :, ::-1, :], 2, 3)
        xs = _mlp(o.reshape(bs * ts, -1), xs, (mod_s, l + 1), norm_w[l + 1, 1], w_out_a,
                  w_up[1], w_down[1], final_norm_w, tm=tm_s,
                  final_norm=(l + 2 == depth), n_parts=1)

    y_p = xp.reshape(bp, tp, d)
    y_s = xs.reshape(bs, ts, d)
    hg_p, rt_p, k_p, v_p, hg_s, rt_s, k_s, v_s = (
        jnp.stack(t) for t in (hg_p, rt_p, k_p, v_p, hg_s, rt_s, k_s, v_s))

    n_odd = k_p.shape[0]

    def values_from_stored_order(v, lead):
        v = jnp.swapaxes(v.reshape(n_odd, *lead, 2, n_heads_c, dh), -2, -3)
        return v.reshape(n_odd, *lead, n_heads_c, 2 * dh)

    k_p = k_p.reshape(n_odd, bp, tp // page, page, n_heads_c, 2, dh)
    v_p = values_from_stored_order(v_p, (bp, tp // page, page))
    k_s = k_s.reshape(n_odd, bs, ts, n_heads_c, 2, dh)
    v_s = values_from_stored_order(v_s, (bs, ts))
    return (y_p, y_s, hg_p, rt_p, k_p, v_p, hg_s, rt_s, k_s, v_s)
```

```python
import functools
import math

import jax
import jax.numpy as jnp
import numpy as np
from jax import lax
from jax.experimental import pallas as pl
from jax.experimental.pallas import tpu as pltpu

F32 = jnp.float32
BF16 = jnp.bfloat16

HEAD_DIM = 128
SUBLANES = 8
NORM_EPS = 1e-6
ROPE_BASE = 10000.0
V7X_VMEM_LIMIT = 52 * 1024 * 1024

_NT = (((1,), (1,)), ((), ()))
_TN = (((0,), (0,)), ((), ()))


def _params(*semantics):
    return pltpu.CompilerParams(dimension_semantics=semantics, vmem_limit_bytes=V7X_VMEM_LIMIT)


def _resident(shape):
    zeros = (0,) * len(shape)
    return pl.BlockSpec(shape, lambda *_: zeros, pipeline_mode=pl.Buffered(1))


def _resident_layer(stacked, layer):
    return pl.BlockSpec((None,) + stacked.shape[1:], lambda *_: (layer, 0, 0),
                        pipeline_mode=pl.Buffered(1))


def _sigmoid(x):
    return 0.5 + 0.5 * jnp.tanh(0.5 * x)


def _silu(x):
    half = 0.5 * x
    return half + half * jnp.tanh(half)


def _rms(x):
    return x * lax.rsqrt(jnp.mean(x * x, axis=-1, keepdims=True) + NORM_EPS)


def _mm(a, b):
    return jnp.dot(a, b, preferred_element_type=F32)


def _ada_kernel(cp_ref, cs_ref, w_ref, b_ref, op_ref, os_ref):
    w = w_ref[0].astype(BF16)
    op_ref[0] = _mm(_silu(cp_ref[...]).astype(BF16), w) + b_ref[0]
    os_ref[0] = _mm(_silu(cs_ref[...]).astype(BF16), w) + b_ref[0]


def _ada(c_prompt, c_sample_rows, w_ada, b_ada):
    depth, d, n = w_ada.shape
    bp, rs = c_prompt.shape[0], c_sample_rows.shape[0]
    tn = n // 4
    return pl.pallas_call(
        _ada_kernel,
        grid=(depth, n // tn),
        in_specs=[pl.BlockSpec((bp, d), lambda l, j: (0, 0)),
                  pl.BlockSpec((rs, d), lambda l, j: (0, 0)),
                  pl.BlockSpec((1, d, tn), lambda l, j: (l, 0, j)),
                  pl.BlockSpec((1, 1, tn), lambda l, j: (l, 0, j))],
        out_specs=[pl.BlockSpec((1, bp, tn), lambda l, j: (l, 0, j)),
                   pl.BlockSpec((1, rs, tn), lambda l, j: (l, 0, j))],
        out_shape=[jax.ShapeDtypeStruct((depth, bp, n), F32),
                   jax.ShapeDtypeStruct((depth, rs, n), F32)],
        compiler_params=_params("arbitrary", "arbitrary"),
        name="ada_mod",
    )(c_prompt, c_sample_rows, w_ada, b_ada.reshape(depth, 1, n))


def _mod_spec(mod, n_rows, tm):
    arr, layer = mod
    tiles_per_block = (n_rows // arr.shape[1]) // tm
    return pl.BlockSpec((None, 1, arr.shape[2], arr.shape[3]),
                        lambda i, *_: (layer, i // tiles_per_block, 0, 0))


def _modulated_norm(x, nw_ref, mod_ref, shift_col, scale_col):
    d = x.shape[-1]
    shift = mod_ref[0, :, shift_col * d:(shift_col + 1) * d]
    scale = mod_ref[0, :, scale_col * d:(scale_col + 1) * d]
    return _rms(x) * nw_ref[...] * (1.0 + scale) + shift


def _inproj_kernel(x_ref, mod_ref, nw_ref, w_ref, *out_refs, shift_col, scale_col, out_scales, tc):
    h = _modulated_norm(x_ref[...], nw_ref, mod_ref, shift_col, scale_col).astype(BF16)
    off = 0
    for o_ref, out_scale in zip(out_refs, out_scales):
        width = o_ref.shape[-1]
        for c in range(0, width, tc):
            y = _mm(h, w_ref[:, off + c:off + c + tc])
            if out_scale != 1.0:
                y = y * out_scale
            o_ref[:, c:c + tc] = y.astype(o_ref.dtype)
        off += width


def _inproj(x, mod, nw, w, outs, *, tm, shift_col, scale_col):
    m, d = x.shape
    n = w.shape[1]
    kern = functools.partial(_inproj_kernel, shift_col=shift_col, scale_col=scale_col,
                             out_scales=tuple(o[2] for o in outs), tc=512)
    return pl.pallas_call(
        kern,
        grid=(m // tm,),
        in_specs=[pl.BlockSpec((tm, d), lambda i: (i, 0)),
                  _mod_spec(mod, m, tm),
                  _resident((1, d)),
                  _resident((d, n))],
        out_specs=[pl.BlockSpec((tm, o[0]), lambda i: (i, 0)) for o in outs],
        out_shape=[jax.ShapeDtypeStruct((m, o[0]), o[1]) for o in outs],
        compiler_params=_params("parallel"),
        name="norm_mod_inproj",
    )(x, mod[0], nw.reshape(1, d), w)


def _qkv_kernel(x_ref, mod_ref, nw_ref, w_ref, q_ref, kb_ref, vb_ref, kr_ref, vr_ref, *,
                q_scale, n_heads, tc):
    h = _modulated_norm(x_ref[...], nw_ref, mod_ref, 0, 1).astype(BF16)
    tm, n_c = q_ref.shape
    lanes = HEAD_DIM
    group = 2 * n_heads
    for c in range(0, n_c, tc):
        q_ref[:, c:c + tc] = (_mm(h, w_ref[:, c:c + tc]) * q_scale).astype(q_ref.dtype)
    for c in range(0, n_c, tc):
        y = _mm(h, w_ref[:, n_c + c:n_c + c + tc])
        kb_ref[:, c:c + tc] = y.astype(kb_ref.dtype)
        for i in range(tc // lanes):
            kr_ref[pl.ds(c // lanes + i, tm, stride=group), :] = y[:, i * lanes:(i + 1) * lanes]
    for c in range(0, n_c, tc):
        y = _mm(h, w_ref[:, 2 * n_c + c:2 * n_c + c + tc])
        vb_ref[:, c:c + tc] = y.astype(vb_ref.dtype)
        for i in range(tc // lanes):
            head, half = divmod(c // lanes + i, 2)
            vr_ref[pl.ds(half * n_heads + head, tm, stride=group), :] = y[:, i * lanes:(i + 1) * lanes]


def _qkv(x, mod, nw, w, *, tm, q_scale, n_heads):
    m, d = x.shape
    n_c = w.shape[1] // 3
    group = 2 * n_heads
    kern = functools.partial(_qkv_kernel, q_scale=q_scale, n_heads=n_heads, tc=512)
    cols = pl.BlockSpec((tm, n_c), lambda i: (i, 0))
    rows = pl.BlockSpec((tm * group, HEAD_DIM), lambda i: (i, 0))
    return pl.pallas_call(
        kern,
        grid=(m // tm,),
        in_specs=[pl.BlockSpec((tm, d), lambda i: (i, 0)),
                  _mod_spec(mod, m, tm),
                  _resident((1, d)),
                  _resident((d, 3 * n_c))],
        out_specs=[cols, cols, cols, rows, rows],
        out_shape=[jax.ShapeDtypeStruct((m, n_c), BF16)] * 3
                  + [jax.ShapeDtypeStruct((m * group, HEAD_DIM), F32)] * 2,
        compiler_params=_params("parallel"),
        name="norm_mod_qkv",
    )(x, mod[0], nw.reshape(1, d), w)


N_MLP_INPUTS = 8
N_ATTENTION_INPUTS = 7


def _mlp_kernel(*refs, n_parts, final_norm, attention):
    if attention is None:
        o_ref, x_ref, mod_ref, nw_ref, wo_ref, wu_ref, wd_ref, fw_ref, out_ref, h_ref, acc_ref = refs
    else:
        pt_ref, refs = refs[0], refs[1:]
        o_ref, x_ref, mod_ref, nw_ref, wo_ref, wu_ref, wd_ref, fw_ref = refs[:N_MLP_INPUTS]
        attention_in = refs[N_MLP_INPUTS:N_MLP_INPUTS + N_ATTENTION_INPUTS]
        out_ref, attention_out, h_ref, acc_ref = refs[N_MLP_INPUTS + N_ATTENTION_INPUTS:][:4]
        attention_refs = (pt_ref, *attention_in, attention_out,
                          *refs[N_MLP_INPUTS + N_ATTENTION_INPUTS + 4:])
    d = x_ref.shape[-1]
    cw = wu_ref.shape[1] // n_parts

    def mod_col(col):
        return mod_ref[0, :, col * d:(col + 1) * d]

    hidden = [None] * n_parts

    def mlp_up(r):
        if r == 0:
            x1 = x_ref[...] + mod_col(2) * _mm(o_ref[...].astype(BF16), wo_ref[...])
            out_ref[...] = x1
            h_ref[...] = _modulated_norm(x1, nw_ref, mod_ref, 3, 4).astype(BF16)
        u = jnp.maximum(_mm(h_ref[...], wu_ref[:, r * cw:(r + 1) * cw]), 0.0)
        hidden[r] = (u * u).astype(BF16)

    def mlp_down(r):
        part = _mm(hidden[r], wd_ref[r * cw:(r + 1) * cw, :])
        acc_ref[...] = part if r == 0 else acc_ref[...] + part

    def finish():
        y = out_ref[...] + mod_col(5) * acc_ref[...]
        if final_norm:
            y = _rms(y) * fw_ref[...]
        out_ref[...] = y

    if attention is None:
        for r in range(n_parts):
            mlp_up(r)
            mlp_down(r)
        finish()
    else:
        assert n_parts % PAGE_FETCH_STEPS == 0
        aligned = attention["steps_per_seq"] % n_parts == 0
        mlp_up(0)
        for r in range(n_parts):
            last = r == n_parts - 1
            _paged_attention_step(pl.program_id(0) * n_parts + r, pl.num_programs(0) * n_parts,
                                  attention_refs,
                                  (functools.partial(mlp_down, r),
                                   finish if last else functools.partial(mlp_up, r + 1),
                                   lambda: None),
                                  sub_step=r % PAGE_FETCH_STEPS,
                                  may_open=(r == 0 or not aligned),
                                  may_close=(last or not aligned), **attention)


def _mlp(o, x, mod, nw, w_out, w_up, w_down, final_w, *, tm, final_norm, n_parts, paged=None):
    m, d = x.shape
    k = o.shape[1]
    (w_up, up_layer), (w_down, down_layer) = w_up, w_down
    dff = w_up.shape[2]
    n_tiles = m // tm
    in_specs = [pl.BlockSpec((tm, k), lambda i, *_: (i, 0)),
                pl.BlockSpec((tm, d), lambda i, *_: (i, 0)),
                _mod_spec(mod, m, tm),
                _resident((1, d)),
                _resident((k, d)),
                _resident_layer(w_up, up_layer),
                _resident_layer(w_down, down_layer),
                _resident((1, d))]
    operands = [o, x, mod[0], nw.reshape(1, d), w_out, w_up, w_down, final_w.reshape(1, d)]
    out_specs = pl.BlockSpec((tm, d), lambda i, *_: (i, 0))
    out_shape = jax.ShapeDtypeStruct((m, d), F32)
    scratch = [pltpu.VMEM((tm, d), BF16), pltpu.VMEM((tm, d), F32)]
    if paged is None:
        kern = functools.partial(_mlp_kernel, n_parts=n_parts, final_norm=final_norm, attention=None)
        return pl.pallas_call(
            kern, grid=(n_tiles,), in_specs=in_specs, out_specs=out_specs, out_shape=out_shape,
            scratch_shapes=scratch, compiler_params=_params("arbitrary"), name="outproj_mlp_residual",
        )(*operands)

    q_rows, k_new, v_new = paged["q_rows"], paged["k_new"], paged["v_new"]
    cache_k, cache_v, page_table = paged["cache_k"], paged["cache_v"], paged["page_table"]
    n_seq = paged["n_seq"]
    n_rows, dh = q_rows.shape[1:]
    n_pages = page_table.shape[1]
    page_rows = cache_k.shape[1]
    assert 2 * paged["n_heads"] == SUBLANES, "one (map, token) block of query rows must fill one vreg"
    pages_per_step = (n_seq * n_pages) // (n_tiles * n_parts)
    assert pages_per_step * n_tiles * n_parts == n_seq * n_pages and n_pages % pages_per_step == 0
    attention = dict(pages_per_step=pages_per_step, page_base=paged["page_base"],
                     steps_per_seq=n_pages // pages_per_step, first_seq=paged["first_seq"],
                     tl=paged["tl"], n_heads=paged["n_heads"], lam_init=paged["lam_init"])
    kern = functools.partial(_mlp_kernel, n_parts=n_parts, final_norm=final_norm, attention=attention)
    ring = (PAGE_RING_DEPTH, PAGE_FETCH_STEPS * pages_per_step, page_rows, dh)
    grid_spec = pltpu.PrefetchScalarGridSpec(
        num_scalar_prefetch=1,
        grid=(n_tiles,),
        in_specs=in_specs + [_resident(paged["lam_p"].shape), _resident((1, 2 * dh)),
                             _resident(q_rows.shape), _resident(k_new.shape), _resident(v_new.shape),
                             pl.BlockSpec(memory_space=pl.ANY), pl.BlockSpec(memory_space=pl.ANY)],
        out_specs=[out_specs, pl.BlockSpec((n_seq, n_rows // 2, dh), lambda i, *_: (0, 0, 0))],
        scratch_shapes=scratch + [pltpu.VMEM((n_rows, dh), F32), pltpu.VMEM((n_rows, dh), F32),
                                  pltpu.VMEM((n_rows, dh), F32),
                                  pltpu.VMEM((dh, dh), F32), pltpu.VMEM((dh, dh), F32),
                                  pltpu.VMEM(ring, F32), pltpu.VMEM(ring, F32),
                                  pltpu.SemaphoreType.DMA((2, PAGE_RING_DEPTH))],
    )
    return pl.pallas_call(
        kern,
        grid_spec=grid_spec,
        out_shape=[out_shape, jax.ShapeDtypeStruct((n_seq, n_rows // 2, dh), F32)],
        compiler_params=_params("arbitrary"),
        name="outproj_mlp_paged_attention",
    )(page_table.reshape(-1), *operands, paged["lam_p"], paged["subln_w"].reshape(1, 2 * dh),
      q_rows, k_new, v_new, cache_k, cache_v)


def _hgrn_lower_bound(logits_ref, layer_slot):
    lg = logits_ref[...]
    ex = jnp.exp(lg - jnp.max(lg, axis=0, keepdims=True))
    pr = ex / jnp.sum(ex, axis=0, keepdims=True)
    return jnp.sum(pr[:layer_slot + 1], axis=0, keepdims=True)


def _rotate_pairs(x, cos, sin_signed):
    n = x.shape[-1]
    lane = lax.broadcasted_iota(jnp.int32, x.shape, 1)
    swapped = jnp.where((lane & 1) == 0, pltpu.roll(x, n - 1, 1), pltpu.roll(x, 1, 1))
    return x * cos + swapped * sin_signed


def _cumsum_rows(x):
    n = x.shape[0]
    row = lax.broadcasted_iota(jnp.int32, x.shape, 0)
    d = 1
    while d < n:
        x = x + jnp.where(row >= d, pltpu.roll(x, d, 0), 0.0)
        d *= 2
    return x


def _split_row(g, b):
    n, lanes = g.shape
    if 2 * b >= SUBLANES:
        g3 = g.reshape(n // (2 * b), 2 * b, lanes)
        return jnp.broadcast_to(g3[:, b - 1:b, :], g3.shape).reshape(n, lanes)
    rmod = lax.broadcasted_iota(jnp.int32, g.shape, 0) & (2 * b - 1)
    out = g
    for r in range(2 * b):
        if r != b - 1:
            out = jnp.where(rmod == r, pltpu.roll(g, (r - (b - 1)) % n, 0), out)
    return out


def _level_matrix(n):
    t = np.arange(n)[:, None]
    s = np.arange(n)[None, :]
    hi = np.floor(np.log2(np.maximum(t ^ s, 1))).astype(np.int32)
    return jnp.asarray(np.where(s < t, hi, -1).astype(np.int32))


def _hgrn_chunk(q, k, v, g2, lv, st):
    n = q.shape[0]
    row = lax.broadcasted_iota(jnp.int32, q.shape, 0)
    ri = lax.broadcasted_iota(jnp.int32, (n, n), 0)
    ci = lax.broadcasted_iota(jnp.int32, (n, n), 1)
    a = jnp.where(ri == ci, jnp.sum(q * k, axis=-1, keepdims=True), 0.0)
    b, level = 1, 0
    while b < n:
        if b >= SUBLANES:
            parts = []
            for lo in range(0, n, 2 * b):
                edge = g2[lo + b - 1:lo + b, :]
                parts.append(k[lo:lo + b] * jnp.exp2(edge - g2[lo:lo + b]))
                parts.append(q[lo + b:lo + 2 * b] * jnp.exp2(g2[lo + b:lo + 2 * b] - edge))
            z = jnp.concatenate(parts, axis=0).astype(BF16)
        else:
            is_query = (row & b) != 0
            sign = jnp.where(is_query, 1.0, -1.0)
            z = (jnp.where(is_query, q, k) * jnp.exp2((g2 - _split_row(g2, b)) * sign)).astype(BF16)
        a = jnp.where(lv == level, lax.dot_general(z, z, _NT, preferred_element_type=F32), a)
        b, level = 2 * b, level + 1
    carried = lax.dot_general((q * jnp.exp2(g2)).astype(BF16), st.astype(BF16), _NT,
                              preferred_element_type=F32)
    o = _mm(a.astype(BF16), v.astype(BF16)) + carried
    g_end = g2[n - 1:n, :]
    k_dec = (k * jnp.exp2(g_end - g2)).astype(BF16)
    st_new = st * jnp.exp2(g_end) + lax.dot_general(v.astype(BF16), k_dec, _TN,
                                                    preferred_element_type=F32)
    return o, st_new


def _retention_chunk(q, k, v, log_gamma, st):
    n = q.shape[0]
    rowf = lax.broadcasted_iota(jnp.int32, q.shape, 0).astype(F32)
    ri = lax.broadcasted_iota(jnp.int32, (n, n), 0)
    ci = lax.broadcasted_iota(jnp.int32, (n, n), 1)
    decay = jnp.where(ci <= ri, jnp.exp((ri - ci).astype(F32) * log_gamma), 0.0)
    a = lax.dot_general(q.astype(BF16), k.astype(BF16), _NT, preferred_element_type=F32) * decay
    q_dec = (q * jnp.exp((rowf + 1.0) * log_gamma)).astype(BF16)
    o = _mm(a.astype(BF16), v.astype(BF16)) + _mm(q_dec, st.astype(BF16))
    k_dec = (k * jnp.exp((n - 1.0 - rowf) * log_gamma)).astype(BF16)
    st_new = st * math.exp(n * log_gamma) + lax.dot_general(k_dec, v.astype(BF16), _TN,
                                                            preferred_element_type=F32)
    return o, st_new


def _retention_log_gamma(h):
    return math.log(1.0 - 2.0 ** (-5.0 - h))


def _mixer_prompt_kernel(proj_ref, cos_ref, sin_ref, lv_ref, lbl_ref, na_ref,
                         o_ref, sa_out, sb_out, sat_ref, sb_ref, *, layer_slot, n_heads):
    t = pl.program_id(1)
    hd = HEAD_DIM
    w = n_heads * hd

    @pl.when(t == 0)
    def _():
        sat_ref[...] = jnp.zeros_like(sat_ref)
        sb_ref[...] = jnp.zeros_like(sb_ref)

    lb = _hgrn_lower_bound(lbl_ref, layer_slot)
    lv = lv_ref[...]
    cos = cos_ref[...]
    sin = sin_ref[...]
    for r in range(proj_ref.shape[0]):
        def cols(group, h):
            return proj_ref[r, :, group * w + h * hd:group * w + (h + 1) * hd]

        fz = lb + (1.0 - lb) * _sigmoid(proj_ref[r, :, w:2 * w])
        g_all = _cumsum_rows(jnp.log2(fz))
        k_all = 1.0 - fz
        for h in range(n_heads):
            sl = slice(h * hd, (h + 1) * hd)
            q = _silu(cols(0, h)) * hd ** -0.5
            o, st = _hgrn_chunk(q, k_all[:, sl], cols(2, h), g_all[:, sl], lv, sat_ref[r, h])
            sat_ref[r, h] = st
            o_ref[r, :, sl] = (_rms(o) * na_ref[...] * _silu(cols(3, h))).astype(o_ref.dtype)
        for h in range(n_heads):
            q = _rotate_pairs(cols(4, h), cos, sin)
            k = _rotate_pairs(cols(5, h), cos, sin) * hd ** -0.5
            o, st = _retention_chunk(q, k, cols(6, h), _retention_log_gamma(h), sb_ref[r, h])
            sb_ref[r, h] = st
            o_ref[r, :, w + h * hd:w + (h + 1) * hd] = (_rms(o) * _silu(cols(7, h))).astype(o_ref.dtype)

    @pl.when(t == pl.num_programs(1) - 1)
    def _():
        for r in range(proj_ref.shape[0]):
            for h in range(n_heads):
                sa_out[r, h] = sat_ref[r, h].T
                sb_out[r, h] = sb_ref[r, h]


def _mixer_prompt(proj, cos, sin, lb_logits, norm_a, *, layer_slot, chunk, rows_per_step):
    bsz, tlen, n = proj.shape
    hd = HEAD_DIM
    n_heads = n // (8 * hd)
    rb = rows_per_step
    kern = functools.partial(_mixer_prompt_kernel, layer_slot=layer_slot, n_heads=n_heads)
    state = jax.ShapeDtypeStruct((bsz, n_heads, hd, hd), F32)
    state_spec = pl.BlockSpec((rb, n_heads, hd, hd), lambda b, t: (b, 0, 0, 0))
    return pl.pallas_call(
        kern,
        grid=(bsz // rb, tlen // chunk),
        in_specs=[pl.BlockSpec((rb, chunk, n), lambda b, t: (b, t, 0)),
                  pl.BlockSpec((chunk, hd), lambda b, t: (t, 0)),
                  pl.BlockSpec((chunk, hd), lambda b, t: (t, 0)),
                  pl.BlockSpec((chunk, chunk), lambda b, t: (0, 0)),
                  pl.BlockSpec(lb_logits.shape, lambda b, t: (0, 0)),
                  pl.BlockSpec((1, hd), lambda b, t: (0, 0))],
        out_specs=[pl.BlockSpec((rb, chunk, 2 * n_heads * hd), lambda b, t: (b, t, 0)),
                   state_spec, state_spec],
        out_shape=[jax.ShapeDtypeStruct((bsz, tlen, 2 * n_heads * hd), BF16), state, state],
        scratch_shapes=[pltpu.VMEM((rb, n_heads, hd, hd), F32), pltpu.VMEM((rb, n_heads, hd, hd), F32)],
        compiler_params=_params("arbitrary", "arbitrary"),
        name="mixer_prompt",
    )(proj, cos, sin, _level_matrix(chunk), lb_logits, norm_a.reshape(1, hd))


def _pair_chunk(q, k, v, g, s0, dec, kpad_ref, vpad_ref):
    tl = q.shape[0]
    row = lax.broadcasted_iota(jnp.int32, q.shape, 0)
    o = _mm((q * jnp.exp(g)).astype(BF16), s0.astype(BF16))
    for s in range(tl):
        live = row >= s
        w = jnp.exp(jnp.where(live, g - g[s:s + 1, :], 0.0))
        a = jnp.sum(jnp.where(live, q * w * k[s:s + 1, :], 0.0), axis=-1, keepdims=True)
        o = o + a * v[s:s + 1, :]
    kpad_ref[0:tl, :] = k * jnp.exp(g[tl - 1:tl, :] - g)
    vpad_ref[0:tl, :] = v
    upd = lax.dot_general(kpad_ref[...].astype(BF16), vpad_ref[...].astype(BF16), _TN,
                          preferred_element_type=F32)
    return o, s0 * dec + upd


def _mixer_sample_kernel(proj_ref, cos_ref, sin_ref, lbl_ref, na_ref, sa_ref, sb_ref,
                         o_ref, sa_out, sb_out, kpad_ref, vpad_ref, *, layer_slot, n_heads):
    hd = HEAD_DIM
    w = n_heads * hd
    tl = proj_ref.shape[1]
    kpad_ref[...] = jnp.zeros_like(kpad_ref)
    vpad_ref[...] = jnp.zeros_like(vpad_ref)

    def cols(group, h):
        return proj_ref[0, :, group * w + h * hd:group * w + (h + 1) * hd]

    lb = _hgrn_lower_bound(lbl_ref, layer_slot)
    fz = lb + (1.0 - lb) * _sigmoid(proj_ref[0, :, w:2 * w])
    lf = jnp.log(fz)
    row = lax.broadcasted_iota(jnp.int32, lf.shape, 0)
    g_all = jnp.zeros_like(lf)
    for j in range(tl):
        g_all = g_all + jnp.where(row >= j, lf[j:j + 1, :], 0.0)
    k_all = 1.0 - fz
    for h in range(n_heads):
        sl = slice(h * hd, (h + 1) * hd)
        g = g_all[:, sl]
        q = _silu(cols(0, h)) * hd ** -0.5
        dec = jnp.broadcast_to(jnp.exp(g[tl - 1:tl, :]), (hd, hd)).T
        o, st = _pair_chunk(q, k_all[:, sl], cols(2, h), g, sa_ref[0, h], dec, kpad_ref, vpad_ref)
        sa_out[0, h] = st
        o_ref[0, :, sl] = _rms(o) * na_ref[...] * _silu(cols(3, h))

    cos = cos_ref[...]
    sin = sin_ref[...]
    rowf = lax.broadcasted_iota(jnp.int32, (tl, hd), 0).astype(F32)
    for h in range(n_heads):
        lg = _retention_log_gamma(h)
        q = _rotate_pairs(cols(4, h), cos, sin)
        k = _rotate_pairs(cols(5, h), cos, sin) * hd ** -0.5
        o, st = _pair_chunk(q, k, cols(6, h), (rowf + 1.0) * lg, sb_ref[0, h], math.exp(tl * lg),
                            kpad_ref, vpad_ref)
        sb_out[0, h] = st
        o_ref[0, :, w + h * hd:w + (h + 1) * hd] = _rms(o) * _silu(cols(7, h))


def _mixer_sample(proj, cos, sin, lb_logits, norm_a, state_a, state_b, *, layer_slot):
    bsz, tl, n = proj.shape
    hd = HEAD_DIM
    n_heads = n // (8 * hd)
    kern = functools.partial(_mixer_sample_kernel, layer_slot=layer_slot, n_heads=n_heads)
    state = jax.ShapeDtypeStruct((bsz, n_heads, hd, hd), F32)
    state_spec = pl.BlockSpec((1, n_heads, hd, hd), lambda b: (b, 0, 0, 0))
    return pl.pallas_call(
        kern,
        grid=(bsz,),
        in_specs=[pl.BlockSpec((1, tl, n), lambda b: (b, 0, 0)),
                  pl.BlockSpec((tl, hd), lambda b: (0, 0)),
                  pl.BlockSpec((tl, hd), lambda b: (0, 0)),
                  pl.BlockSpec(lb_logits.shape, lambda b: (0, 0)),
                  pl.BlockSpec((1, hd), lambda b: (0, 0)),
                  state_spec, state_spec],
        out_specs=[pl.BlockSpec((1, tl, 2 * n_heads * hd), lambda b: (b, 0, 0)), state_spec, state_spec],
        out_shape=[jax.ShapeDtypeStruct((bsz, tl, 2 * n_heads * hd), F32), state, state],
        scratch_shapes=[pltpu.VMEM((hd, hd), F32), pltpu.VMEM((hd, hd), F32)],
        compiler_params=_params("arbitrary"),
        name="mixer_sample",
    )(proj, cos, sin, lb_logits, norm_a.reshape(1, hd), state_a, state_b)


def _diff_lambda(lam_ref, lam_init):
    lp = lam_ref[...]
    return (jnp.exp(jnp.sum(lp[0:1] * lp[1:2], axis=-1, keepdims=True))
            - jnp.exp(jnp.sum(lp[2:3] * lp[3:4], axis=-1, keepdims=True)) + lam_init)


LOG2_E = math.log2(math.e)


def _attn_prompt_kernel(lam_ref, sw_ref, q_ref, k_ref, v_ref, o_ref, m_ref, l_ref, acc_ref, *,
                        tq, lam_init):
    dh = HEAD_DIM
    lanes = m_ref.shape[-1]
    n_tiles = q_ref.shape[1] // tq
    m_ref[...] = jnp.full_like(m_ref, -jnp.inf)
    l_ref[...] = jnp.zeros_like(l_ref)
    acc_ref[...] = jnp.zeros_like(acc_ref)

    def tile_pair(q0, nq, k0, nk):
        rq = slice(q0, q0 + nq)
        rk = slice(k0, k0 + nk)
        v = v_ref[0, rk, :]
        scores = []
        for j in range(2):
            s = lax.dot_general(q_ref[0, rq, j * dh:(j + 1) * dh], k_ref[0, rk, j * dh:(j + 1) * dh],
                                _NT, preferred_element_type=F32)
            if k0 + nk - 1 > q0:
                ri = lax.broadcasted_iota(jnp.int32, s.shape, 0)
                ci = lax.broadcasted_iota(jnp.int32, s.shape, 1)
                s = jnp.where(ci - ri <= q0 - k0, s, -jnp.inf)
            scores.append(s)
        probs, alphas = [], []
        for j in range(2):
            s = scores[j]
            m_old = m_ref[j, rq]
            m_new = jnp.maximum(m_old, jnp.max(s, axis=-1, keepdims=True))
            alpha = jnp.exp2(m_old - m_new)
            l_new = alpha * l_ref[j, rq]
            tiles = []
            for c in range(0, s.shape[1], lanes):
                p = jnp.exp2(s[:, c:c + lanes] - m_new)
                l_new = l_new + p
                tiles.append(p.astype(BF16))
            l_ref[j, rq] = l_new
            m_ref[j, rq] = m_new
            probs.append(jnp.concatenate(tiles, axis=1))
            alphas.append(alpha)
        for j in range(2):
            pv = _mm(probs[j], v)
            for c in range(0, pv.shape[1], lanes):
                acc_ref[j, rq, c:c + lanes] = (alphas[j] * acc_ref[j, rq, c:c + lanes]
                                               + pv[:, c:c + lanes])

    for kj in range(n_tiles):
        for qi in range(kj, n_tiles):
            tile_pair(qi * tq, tq, kj * tq, tq)

    lam = _diff_lambda(lam_ref, lam_init)
    for qi in range(n_tiles):
        rq = slice(qi * tq, (qi + 1) * tq)
        l0 = jnp.sum(l_ref[0, rq], axis=-1, keepdims=True)
        l1 = jnp.sum(l_ref[1, rq], axis=-1, keepdims=True)
        o = acc_ref[0, rq] / l0 - lam * (acc_ref[1, rq] / l1)
        o_ref[0, rq] = (_rms(o) * sw_ref[...] * (1.0 - lam_init)).astype(o_ref.dtype)


def _attn_prompt(q, k, v, lam_p, subln_w, *, lam_init, tq):
    bsz, tlen, n = q.shape
    e = 2 * HEAD_DIM
    n_heads = n // e
    kern = functools.partial(_attn_prompt_kernel, tq=tq, lam_init=lam_init)
    head_cols = pl.BlockSpec((1, tlen, e), lambda b, h: (b, 0, h))
    return pl.pallas_call(
        kern,
        grid=(bsz, n_heads),
        in_specs=[pl.BlockSpec(lam_p.shape, lambda b, h: (0, 0)),
                  pl.BlockSpec((1, e), lambda b, h: (0, 0)),
                  head_cols, head_cols, head_cols],
        out_specs=head_cols,
        out_shape=jax.ShapeDtypeStruct((bsz, tlen, n), BF16),
        scratch_shapes=[pltpu.VMEM((2, tlen, HEAD_DIM), F32), pltpu.VMEM((2, tlen, HEAD_DIM), F32),
                        pltpu.VMEM((2, tlen, e), F32)],
        compiler_params=_params("parallel", "parallel"),
        name="diff_attn_prompt",
    )(lam_p, subln_w.reshape(1, e), q, k, v)


def _sample_row_fields(shape, tl, n_heads):
    r = lax.broadcasted_iota(jnp.int32, shape, 0)
    head = n_heads - 1 - r % n_heads
    half = (r // n_heads) % 2
    token = (r // (2 * n_heads)) % tl
    qmap = r // (2 * n_heads * tl)
    return head, qmap, token, half


PAGE_RING_DEPTH = 2
PAGE_FETCH_STEPS = 2


def _paged_attention_step(step, n_steps, refs, overlapped_work, *, pages_per_step, page_base,
                          steps_per_seq, first_seq, tl, n_heads, lam_init, may_open, may_close,
                          sub_step):
    (pt_ref, lam_ref, sw_ref, q_ref, kn_ref, vn_ref, ck_hbm, cv_hbm, o_ref,
     m_ref, l_ref, acc_ref, kpad_ref, vpad_ref, kbuf, vbuf, sems) = refs
    seq = step // steps_per_seq
    p = step % steps_per_seq
    group = 2 * n_heads
    n_rows = q_ref.shape[1]
    rows_per_map = n_rows // 2

    fetch_pages = PAGE_FETCH_STEPS * pages_per_step
    fetch = step // PAGE_FETCH_STEPS
    n_fetches = n_steps // PAGE_FETCH_STEPS

    def page_copies(of_fetch, slot):
        out = []
        for i in range(fetch_pages):
            page = page_base + pt_ref[first_seq * steps_per_seq * pages_per_step
                                      + of_fetch * fetch_pages + i]
            out.append(pltpu.make_async_copy(ck_hbm.at[page], kbuf.at[slot, i], sems.at[0, slot]))
            out.append(pltpu.make_async_copy(cv_hbm.at[page], vbuf.at[slot, i], sems.at[1, slot]))
        return out

    def start_fetch(of_fetch):
        for n, cp in enumerate(page_copies(of_fetch, of_fetch % PAGE_RING_DEPTH)):
            cp.start(priority=n % 2)

    slot = fetch % PAGE_RING_DEPTH
    if sub_step == 0:
        ahead = PAGE_RING_DEPTH - 1

        @pl.when(fetch == 0)
        def _():
            for f in range(ahead):
                @pl.when(f < n_fetches)
                def _():
                    start_fetch(f)

        @pl.when(fetch + ahead < n_fetches)
        def _():
            start_fetch(fetch + ahead)

        for cp in page_copies(fetch, slot):
            cp.wait()

    if may_open:
        @pl.when(p == 0)
        def _():
            m_ref[...] = jnp.full_like(m_ref, -jnp.inf)
            l_ref[...] = jnp.zeros_like(l_ref)
            acc_ref[...] = jnp.zeros_like(acc_ref)

    q = q_ref[first_seq + seq]

    def own_lanes(n_keys, causal):
        head, qmap, token, _ = _sample_row_fields((n_rows, n_keys), tl, n_heads)
        key = lax.broadcasted_iota(jnp.int32, (n_rows, n_keys), 1)
        own = (key % group) == 2 * head + qmap
        return jnp.logical_and(own, key // group <= token) if causal else own

    def partial_softmax(pages, causal, hooks=(None, None)):
        before_scores, after_scores = hooks
        own = own_lanes(pages[0][0].shape[0], causal)
        keys = [k_rows.astype(BF16) for k_rows, _ in pages]
        if before_scores is not None:
            before_scores()
        scores = [lax.dot_general(q, k, _NT, preferred_element_type=F32) for k in keys]
        if after_scores is not None:
            after_scores()
        stats, probs = [], []
        for s in scores:
            s = jnp.where(own, s, -jnp.inf)
            m = jnp.max(s, axis=-1, keepdims=True)
            pexp = jnp.exp2(s - m)
            stats.append((m, jnp.sum(pexp, axis=-1, keepdims=True)))
            aligned = []
            for blk in range(n_rows // SUBLANES):
                shift = group - (n_heads - 1) - (blk * SUBLANES) // rows_per_map
                aligned.append(pltpu.roll(pexp[blk * SUBLANES:(blk + 1) * SUBLANES, :], shift, 1,
                                          stride=1, stride_axis=0))
            probs.append(jnp.concatenate(aligned, axis=0).astype(BF16))
        return [(m, l, _mm(p_rows, pltpu.roll(v_rows, group, 0).astype(BF16)))
                for (m, l), p_rows, (_, v_rows) in zip(stats, probs, pages)]

    def merge(parts):
        m_run = m_ref[...]
        m_new = m_run
        for m, _, _ in parts:
            m_new = jnp.maximum(m_new, m)
        alpha = jnp.exp2(m_run - m_new)
        l_new = alpha * l_ref[...]
        acc_new = alpha * acc_ref[...]
        for m, l, o in parts:
            wgt = jnp.exp2(m - m_new)
            l_new = l_new + wgt * l
            acc_new = acc_new + wgt * o
        m_ref[...] = m_new
        l_ref[...] = l_new
        acc_ref[...] = acc_new

    before_scores, after_scores, after_values = overlapped_work
    first_page = sub_step * pages_per_step
    parts = partial_softmax([(kbuf[slot, first_page + i], vbuf[slot, first_page + i])
                             for i in range(pages_per_step)], False,
                            (before_scores, after_scores))
    after_values()
    merge(parts)

    if may_close:
        @pl.when(p == steps_per_seq - 1)
        def _():
            n_new = kn_ref.shape[1]
            kpad_ref[...] = jnp.zeros_like(kpad_ref)
            vpad_ref[...] = jnp.zeros_like(vpad_ref)
            kpad_ref[0:n_new, :] = kn_ref[first_seq + seq]
            vpad_ref[0:n_new, :] = vn_ref[first_seq + seq]
            merge(partial_softmax([(kpad_ref[...], vpad_ref[...])], True))

            o = (acc_ref[0:rows_per_map, :] / l_ref[0:rows_per_map, :]
                 - _diff_lambda(lam_ref, lam_init) * (acc_ref[rows_per_map:, :] / l_ref[rows_per_map:, :]))
            _, _, _, half = _sample_row_fields(o.shape, tl, n_heads)
            ss = jnp.broadcast_to(jnp.sum(o * o, axis=-1, keepdims=True), o.shape)
            other = jnp.where(half == 0, pltpu.roll(ss, rows_per_map - n_heads, 0),
                              pltpu.roll(ss, n_heads, 0))
            inv = lax.rsqrt((ss + other) / (2 * HEAD_DIM) + NORM_EPS)
            sw = jnp.where(half == 0, sw_ref[:, 0:HEAD_DIM], sw_ref[:, HEAD_DIM:])
            o_ref[seq] = o * inv * sw * (1.0 - lam_init)


def _sample_query_rows(q, n_heads):
    bsz, tl, _ = q.shape
    r = np.arange(4 * n_heads * tl)
    head = n_heads - 1 - r % n_heads
    token = (r // (2 * n_heads)) % tl
    qmap = r // (2 * n_heads * tl)
    return q.reshape(bsz, tl, n_heads, 2, HEAD_DIM)[:, token, head, qmap, :]


def _values_in_stored_order(v, n_heads):
    lead = v.shape[:-2]
    tokens = v.shape[-2]
    v = v.reshape(*lead, tokens, n_heads, 2, HEAD_DIM)
    return jnp.swapaxes(v, -2, -3).reshape(*lead, tokens * 2 * n_heads, HEAD_DIM)


def _rope_tables(pos0, tlen):
    inv = 1.0 / (ROPE_BASE ** jnp.linspace(0.0, 1.0, HEAD_DIM // 2, dtype=F32))
    ang = (pos0 + jnp.arange(tlen)).astype(F32)[:, None] * inv[None, :]
    sign = jnp.tile(jnp.asarray([-1.0, 1.0], F32), HEAD_DIM // 2)
    return jnp.repeat(jnp.cos(ang), 2, axis=1), jnp.repeat(jnp.sin(ang), 2, axis=1) * sign


PROMPT_ROW_TILE = 512
PROMPT_MLP_PARTS = 8


def kernel(x_prompt, x_sample, state_hgrn, state_ret, cache_k, cache_v, page_table, c_prompt, c_sample,
           w_ada, b_ada, norm_w, w_in_ab, w_out_ab, hgrn_lb_logits, hgrn_norm_w, w_in_c, w_out_c,
           diff_lambda, diff_subln_w, w_mlp_up, w_mlp_down, final_norm_w):
    bp, tp, d = x_prompt.shape
    bs, ts, _ = x_sample.shape
    n_slots, n_phys, page, n_heads_c, _, dh = cache_k.shape

    mod_p, mod_s = _ada(c_prompt, jnp.repeat(c_sample, ts, axis=0), w_ada, b_ada)
    depth = mod_p.shape[0]
    assert depth % 2 == 0, "layers come in (recurrent mixer, attention) pairs"
    mod_p = mod_p.reshape(depth, bp, 1, 6 * d)
    mod_s = mod_s.reshape(depth, 1, bs * ts, 6 * d)

    def bf16(t):
        return t.astype(BF16)

    w_mlp_up_bf16, w_mlp_down_bf16 = bf16(w_mlp_up), bf16(w_mlp_down)

    n_in = w_in_ab.shape[2]
    n_c = w_in_c.shape[2] // 3
    group = 2 * n_heads_c
    q_scale = dh ** -0.5 * LOG2_E

    past_len = page_table.shape[1] * page
    page_rows = page * group
    cache_k_rows = cache_k.reshape(n_slots * n_phys, page_rows, dh)
    cache_v_rows = _values_in_stored_order(
        cache_v.reshape(n_slots * n_phys, page, n_heads_c * 2 * dh), n_heads_c)
    cos_p, sin_p = _rope_tables(0, tp)
    cos_s, sin_s = _rope_tables(past_len, ts)

    xp = x_prompt.reshape(bp * tp, d)
    xs = x_sample.reshape(bs * ts, d)
    tm_p, tm_s = PROMPT_ROW_TILE, bs * ts
    hg_p, rt_p, k_p, v_p, hg_s, rt_s, k_s, v_s = ([] for _ in range(8))
    for l in range(0, depth, 2):
        e = a = l // 2
        lam_init = 0.8 - 0.6 * math.exp(-0.3 * (l + 1))
        w_in_e, w_out_e, w_in_a, w_out_a = (bf16(t) for t in (w_in_ab[e], w_out_ab[e], w_in_c[a], w_out_c[a]))
        w_up = [(w_mlp_up_bf16, l), (w_mlp_up_bf16, l + 1)]
        w_down = [(w_mlp_down_bf16, l), (w_mlp_down_bf16, l + 1)]

        (proj,) = _inproj(xs, (mod_s, l), norm_w[l, 0], w_in_e, [(n_in, F32, 1.0)],
                          tm=tm_s, shift_col=0, scale_col=1)
        o, s_a, s_b = _mixer_sample(proj.reshape(bs, ts, n_in), cos_s, sin_s, hgrn_lb_logits,
                                    hgrn_norm_w[e], state_hgrn[e], state_ret[e], layer_slot=e)
        hg_s.append(s_a)
        rt_s.append(s_b)
        xs = _mlp(o.reshape(bs * ts, -1), xs, (mod_s, l), norm_w[l, 1], w_out_e, w_up[0],
                  w_down[0], final_norm_w, tm=tm_s, final_norm=False, n_parts=1)
        q, _, _, k_rows, v_rows = _qkv(xs, (mod_s, l + 1), norm_w[l + 1, 0], w_in_a, tm=tm_s,
                                       q_scale=q_scale, n_heads=n_heads_c)
        k_s.append(k_rows)
        v_s.append(v_rows)
        paged = dict(q_rows=_sample_query_rows(q.reshape(bs, ts, n_c), n_heads_c),
                     k_new=k_rows.reshape(bs, ts * group, dh), v_new=v_rows.reshape(bs, ts * group, dh),
                     cache_k=cache_k_rows, cache_v=cache_v_rows, page_table=page_table,
                     lam_p=diff_lambda[a], subln_w=diff_subln_w[a], page_base=a * n_phys, tl=ts,
                     n_heads=n_heads_c, lam_init=lam_init)
        n_lo = bs // 2

        (proj,) = _inproj(xp, (mod_p, l), norm_w[l, 0], w_in_e, [(n_in, F32, 1.0)],
                          tm=tm_p, shift_col=0, scale_col=1)
        o, s_a, s_b = _mixer_prompt(proj.reshape(bp, tp, n_in), cos_p, sin_p, hgrn_lb_logits,
                                    hgrn_norm_w[e], layer_slot=e, chunk=128, rows_per_step=min(4, bp))
        hg_p.append(s_a)
        rt_p.append(s_b)
        xp, attn_lo = _mlp(o.reshape(bp * tp, -1), xp, (mod_p, l), norm_w[l, 1], w_out_e, w_up[0],
                           w_down[0], final_norm_w, tm=tm_p, final_norm=False,
                           n_parts=PROMPT_MLP_PARTS, paged=dict(paged, first_seq=0, n_seq=n_lo))

        q, k, v, k_rows, v_rows = _qkv(xp, (mod_p, l + 1), norm_w[l + 1, 0], w_in_a, tm=tm_p,
                                       q_scale=q_scale, n_heads=n_heads_c)
        k_p.append(k_rows)
        v_p.append(v_rows)
        q, k, v = (t.reshape(bp, tp, n_c) for t in (q, k, v))
        o = _attn_prompt(q, k, v, diff_lambda[a], diff_subln_w[a], lam_init=lam_init, tq=min(512, tp))
        xp, attn_hi = _mlp(o.reshape(bp * tp, -1), xp, (mod_p, l + 1), norm_w[l + 1, 1], w_out_a,
                           w_up[1], w_down[1], final_norm_w, tm=tm_p,
                           final_norm=(l + 2 == depth), n_parts=PROMPT_MLP_PARTS,
                           paged=dict(paged, first_seq=n_lo, n_seq=bs - n_lo))

        o = jnp.concatenate([attn_lo, attn_hi], axis=0)
        o = jnp.swapaxes(o.reshape(bs, ts, 2, n_heads_c, dh)[:, :, :, ::-1, :], 2, 3)
        xs = _mlp(o.reshape(bs * ts, -1), xs, (mod_s, l + 1), norm_w[l + 1, 1], w_out_a,
                  w_up[1], w_down[1], final_norm_w, tm=tm_s,
                  final_norm=(l + 2 == depth), n_parts=1)

    y_p = xp.reshape(bp, tp, d)
    y_s = xs.reshape(bs, ts, d)
    hg_p, rt_p, k_p, v_p, hg_s, rt_s, k_s, v_s = (
        jnp.stack(t) for t in (hg_p, rt_p, k_p, v_p, hg_s, rt_s, k_s, v_s))

    n_odd = k_p.shape[0]

    def values_from_stored_order(v, lead):
        v = jnp.swapaxes(v.reshape(n_odd, *lead, 2, n_heads_c, dh), -2, -3)
        return v.reshape(n_odd, *lead, n_heads_c, 2 * dh)

    k_p = k_p.reshape(n_odd, bp, tp // page, page, n_heads_c, 2, dh)
    v_p = values_from_stored_order(v_p, (bp, tp // page, page))
    k_s = k_s.reshape(n_odd, bs, ts, n_heads_c, 2, dh)
    v_s = values_from_stored_order(v_s, (bs, ts))
    return (y_p, y_s, hg_p, rt_p, k_p, v_p, hg_s, rt_s, k_s, v_s)
```
